```python
import math
import jax, jax.numpy as jnp
from jax import lax
import numpy as np

D_MODEL = 1024
BATCH = 4
SEQ = 4096
DEPTH = 4
DEC_BATCH = 128
DEC_SEQ = 8
PAST_LEN = 8192
PAGE_SIZE = 128

N_MIXERS = 2
N_ATTN_LAYERS = (DEPTH + 1) // 2
N_RWKV_LAYERS = DEPTH // 2
HEAD_DIM = 64
N_HEADS = D_MODEL // HEAD_DIM
N_KV_HEADS = 4
GROUP = N_HEADS // N_KV_HEADS
WINDOW = 128
BLOCK = WINDOW
N_BUCKETS = 32
MAX_DISTANCE = 128
RWKV_HEAD = 64
RWKV_HEADS = D_MODEL // RWKV_HEAD
LORA_DECAY = 64
LORA_AAA = 64
LORA_MV = 32
LORA_GATE = 160
GN_EPS = 64e-5
D_FF = -(-8 * D_MODEL // (3 * 256)) * 256
ALPHA = (2 * DEPTH) ** 0.25
BETA = (8 * DEPTH) ** -0.25
LN_EPS = 1e-5

kernel_name = 'hybrid_swa_sink_rwkv7_deepnorm_step'


def layer_norm(x, g, b):
    xf = x.astype(jnp.float32)
    mu = xf.mean(-1, keepdims=True)
    var = jnp.square(xf - mu).mean(-1, keepdims=True)
    return ((xf - mu) * lax.rsqrt(var + LN_EPS) * g + b).astype(x.dtype)


def t5_bucket(d):
    d = jnp.maximum(d, 0)
    exact = N_BUCKETS // 2
    df = jnp.maximum(d, 1).astype(jnp.float32)
    large = exact + (jnp.log(df / exact) / math.log(MAX_DISTANCE / exact)
                     * (N_BUCKETS - exact)).astype(jnp.int32)
    return jnp.where(d < exact, d, jnp.minimum(large, N_BUCKETS - 1))


def rel_bias_for(d, rel_bias):
    b = rel_bias[t5_bucket(d)]
    return jnp.moveaxis(b, -1, 0).reshape((N_KV_HEADS, GROUP) + d.shape)


def sink_attention(q, k, v, d, valid, sinks, rel_bias):
    s = jnp.einsum('...qkgd,...skd->...kgqs', q, k).astype(jnp.float32) * HEAD_DIM ** -0.5
    s = s + rel_bias_for(d, rel_bias).astype(jnp.float32)
    s = jnp.where(valid, s, -jnp.inf)
    sink = jnp.broadcast_to(sinks.astype(jnp.float32).reshape(N_KV_HEADS, GROUP, 1, 1),
                            s.shape[:-1] + (1,))
    p = jax.nn.softmax(jnp.concatenate([s, sink], axis=-1), axis=-1)[..., :-1]
    return jnp.einsum('...kgqs,...skd->...qkgd', p.astype(v.dtype), v)


def qkv_proj(x, w_qkv, b_qkv):
    lead = x.shape[:-1]
    qkv = x @ w_qkv + b_qkv
    nq, nk = N_HEADS * HEAD_DIM, N_KV_HEADS * HEAD_DIM
    q = qkv[..., :nq].reshape(lead + (N_KV_HEADS, GROUP, HEAD_DIM))
    k = qkv[..., nq:nq + nk].reshape(lead + (N_KV_HEADS, HEAD_DIM))
    v = qkv[..., nq + nk:].reshape(lead + (N_KV_HEADS, HEAD_DIM))
    return q, k, v


def swa_prompt(x, w_qkv, b_qkv, w_o, b_o, sinks, rel_bias):
    B, T, _ = x.shape
    nb = T // BLOCK
    q, k, v = qkv_proj(x, w_qkv, b_qkv)

    def band(t):
        prev = jnp.pad(t[:, :T - BLOCK], ((0, 0), (BLOCK, 0), (0, 0), (0, 0)))
        return jnp.concatenate([prev.reshape(B, nb, BLOCK, N_KV_HEADS, HEAD_DIM),
                                t.reshape(B, nb, BLOCK, N_KV_HEADS, HEAD_DIM)], axis=2)

    qi = jnp.arange(BLOCK)[:, None]
    sj = jnp.arange(2 * BLOCK)[None, :]
    d = qi + BLOCK - sj
    blk = jnp.arange(nb)[:, None, None]
    valid = (d >= 0) & (d < WINDOW) & (blk * BLOCK + sj - BLOCK >= 0)
    o = sink_attention(q.reshape(B, nb, BLOCK, N_KV_HEADS, GROUP, HEAD_DIM), band(k), band(v),
                       d, valid[:, None, None], sinks, rel_bias)
    y = o.reshape(B, T, N_HEADS * HEAD_DIM) @ w_o + b_o
    return y, k[:, T - WINDOW:], v[:, T - WINDOW:]


def swa_sample(x, ck, cv, w_qkv, b_qkv, w_o, b_o, sinks, rel_bias):
    B, T, _ = x.shape
    L = ck.shape[1]
    q, k, v = qkv_proj(x, w_qkv, b_qkv)
    keys = jnp.concatenate([ck, k.astype(ck.dtype)], axis=1)
    vals = jnp.concatenate([cv, v.astype(cv.dtype)], axis=1)
    d = jnp.arange(T)[:, None] + L - jnp.arange(L + T)[None, :]
    valid = (d >= 0) & (d < WINDOW)
    o = sink_attention(q, keys, vals, d, valid, sinks, rel_bias)
    y = o.reshape(B, T, N_HEADS * HEAD_DIM) @ w_o + b_o
    return y, keys[:, T:], vals[:, T:]


def rwkv7_time_mix(x, x_prev, s0, v_first, mix, w_rkv, w0, w1, w2, a0, a1, a2,
                   g1, g2, k_k, k_a, r_k, ln_w, ln_b, w_o, vres):
    B, T, D = x.shape
    xx = jnp.concatenate([x_prev[:, None].astype(x.dtype), x[:, :-1]], axis=1) - x
    xr, xw, xk, xv, xa, xg = (x + xx * mix[c] for c in range(6))
    r = xr @ w_rkv[0]
    k = xk @ w_rkv[1]
    v = xv @ w_rkv[2]
    w = -jax.nn.softplus(-(w0 + jnp.tanh(xw @ w1) @ w2)) - 0.5
    if vres is None:
        v_first = v
    else:
        v0, v1, v2 = vres
        v = v + (v_first - v) * jax.nn.sigmoid(v0 + (xv @ v1) @ v2)
    a = jax.nn.sigmoid(a0 + (xa @ a1) @ a2)
    g = jax.nn.sigmoid(xg @ g1) @ g2

    def heads(t):
        return t.reshape(B, T, RWKV_HEADS, RWKV_HEAD).astype(jnp.float32)

    kk = heads(k * k_k)
    kk = kk / jnp.maximum(jnp.sqrt(jnp.sum(kk * kk, axis=-1, keepdims=True)), 1e-12)
    k = k * (1 + (a - 1) * k_a)
    rh, kh, vh, ah = heads(r), heads(k), heads(v), heads(a)
    decay = jnp.exp(-jnp.exp(heads(w)))

    def step(S, inp):
        r_t, d_t, k_t, v_t, kk_t, a_t = inp
        sa = jnp.einsum('bhvk,bhk->bhv', S, -kk_t)
        S = (S * d_t[:, :, None, :] + sa[..., None] * (kk_t * a_t)[:, :, None, :]
             + v_t[..., None] * k_t[:, :, None, :])
        return S, jnp.einsum('bhvk,bhk->bhv', S, r_t)

    seq = tuple(jnp.moveaxis(t, 1, 0) for t in (rh, decay, kh, vh, kk, ah))
    S, o = lax.scan(step, s0.astype(jnp.float32), seq)
    o = jnp.moveaxis(o, 0, 1)
    mu = o.mean(-1, keepdims=True)
    var = jnp.square(o - mu).mean(-1, keepdims=True)
    o = ((o - mu) * lax.rsqrt(var + GN_EPS)).reshape(B, T, D) * ln_w + ln_b
    bonus = (jnp.sum(rh * kh * r_k.astype(jnp.float32), axis=-1, keepdims=True) * vh).reshape(B, T, D)
    y = ((o + bonus) * g).astype(x.dtype) @ w_o
    return y, v_first, S.astype(s0.dtype), x[:, -1]


def swiglu(x, w_gu, w_down):
    gu = x @ w_gu
    return (jax.nn.silu(gu[..., :D_FF]) * gu[..., D_FF:]) @ w_down


def trunk(x, win_k, win_v, shift, wkv, P):
    prompt = win_k is None
    B = x.shape[0]
    v_first = None
    nk, nv, ns, nw = [], [], [], []
    for i in range(DEPTH):
        j = i // N_MIXERS
        if i % N_MIXERS == 0:
            args = (P['attn_w_qkv'][j], P['attn_b_qkv'][j], P['attn_w_o'][j], P['attn_b_o'][j],
                    P['attn_sinks'][j], P['rel_bias'])
            if prompt:
                h, k_buf, v_buf = swa_prompt(x, *args)
            else:
                h, k_buf, v_buf = swa_sample(x, win_k[j], win_v[j], *args)
            nk.append(k_buf)
            nv.append(v_buf)
        else:
            if prompt:
                x_prev = jnp.zeros((B, D_MODEL), x.dtype)
                s0 = jnp.zeros((B, RWKV_HEADS, RWKV_HEAD, RWKV_HEAD), x.dtype)
            else:
                x_prev, s0 = shift[j], wkv[j]
            vres = None if j == 0 else (P['rwkv_v0'][j - 1], P['rwkv_v1'][j - 1], P['rwkv_v2'][j - 1])
            h, v_first, s_new, last = rwkv7_time_mix(
                x, x_prev, s0, v_first, P['rwkv_mix'][j], P['rwkv_w_rkv'][j],
                P['rwkv_w0'][j], P['rwkv_w1'][j], P['rwkv_w2'][j],
                P['rwkv_a0'][j], P['rwkv_a1'][j], P['rwkv_a2'][j],
                P['rwkv_g1'][j], P['rwkv_g2'][j], P['rwkv_k_k'][j], P['rwkv_k_a'][j],
                P['rwkv_r_k'][j], P['rwkv_ln_w'][j], P['rwkv_ln_b'][j], P['rwkv_w_o'][j], vres)
            ns.append(last)
            nw.append(s_new)
        x = layer_norm(ALPHA * x + h, P['ln_g'][i, 0], P['ln_b'][i, 0])
        x = layer_norm(ALPHA * x + swiglu(x, P['ffn_w_gu'][i], P['ffn_w_down'][i]),
                       P['ln_g'][i, 1], P['ln_b'][i, 1])
    return x, jnp.stack(nk), jnp.stack(nv), jnp.stack(ns), jnp.stack(nw)


def setup_inputs(seed: int = 0) -> dict:
    key = jax.random.key(seed)
    keys = jax.random.split(key, 48)
    it = (keys[i] for i in range(48))

    def nrm(shape, scale):
        return scale * jax.random.normal(next(it), shape, jnp.float32)

    def uni(shape, lo, hi):
        return jax.random.uniform(next(it), shape, jnp.float32, lo, hi)

    NA, NR, D = N_ATTN_LAYERS, N_RWKV_LAYERS, D_MODEL
    QKV = (N_HEADS + 2 * N_KV_HEADS) * HEAD_DIM
    win_rows = min(WINDOW, PAST_LEN)
    return {
        'x_prompt': nrm((BATCH, SEQ, D), 1.0),
        'x_sample': nrm((DEC_BATCH, DEC_SEQ, D), 1.0),
        'cache_win_k': nrm((NA, DEC_BATCH, win_rows, N_KV_HEADS, HEAD_DIM), 1.0),
        'cache_win_v': nrm((NA, DEC_BATCH, win_rows, N_KV_HEADS, HEAD_DIM), 1.0),
        'state_shift': nrm((NR, DEC_BATCH, D), 1.0),
        'state_wkv': nrm((NR, DEC_BATCH, RWKV_HEADS, RWKV_HEAD, RWKV_HEAD), 0.3),
        'rel_bias': nrm((N_BUCKETS, N_HEADS), 0.5),
        'ln_g': 1.0 + nrm((DEPTH, 2, D), 0.05),
        'ln_b': nrm((DEPTH, 2, D), 0.02),
        'attn_w_qkv': nrm((NA, D, QKV), D ** -0.5),
        'attn_b_qkv': nrm((NA, QKV), 0.02),
        'attn_w_o': nrm((NA, N_HEADS * HEAD_DIM, D), BETA * (N_HEADS * HEAD_DIM) ** -0.5),
        'attn_b_o': nrm((NA, D), 0.02),
        'attn_sinks': nrm((NA, N_HEADS), 1.0),
        'rwkv_mix': uni((NR, 6, D), 0.0, 1.0),
        'rwkv_w_rkv': nrm((NR, 3, D, D), D ** -0.5),
        'rwkv_w0': uni((NR, D), -6.0, -1.0),
        'rwkv_w1': nrm((NR, D, LORA_DECAY), D ** -0.5),
        'rwkv_w2': nrm((NR, LORA_DECAY, D), 0.1 * LORA_DECAY ** -0.5),
        'rwkv_a0': nrm((NR, D), 0.1),
        'rwkv_a1': nrm((NR, D, LORA_AAA), D ** -0.5),
        'rwkv_a2': nrm((NR, LORA_AAA, D), LORA_AAA ** -0.5),
        'rwkv_v0': nrm((NR - 1, D), 0.1),
        'rwkv_v1': nrm((NR - 1, D, LORA_MV), D ** -0.5),
        'rwkv_v2': nrm((NR - 1, LORA_MV, D), LORA_MV ** -0.5),
        'rwkv_g1': nrm((NR, D, LORA_GATE), D ** -0.5),
        'rwkv_g2': nrm((NR, LORA_GATE, D), LORA_GATE ** -0.5),
        'rwkv_k_k': 0.85 + nrm((NR, D), 0.05),
        'rwkv_k_a': 1.0 + nrm((NR, D), 0.05),
        'rwkv_r_k': nrm((NR, RWKV_HEADS, RWKV_HEAD), 0.1),
        'rwkv_ln_w': 1.0 + nrm((NR, D), 0.05),
        'rwkv_ln_b': nrm((NR, D), 0.02),
        'rwkv_w_o': nrm((NR, D, D), BETA * D ** -0.5),
        'ffn_w_gu': nrm((DEPTH, D, 2 * D_FF), D ** -0.5),
        'ffn_w_down': nrm((DEPTH, D_FF, D), BETA * D_FF ** -0.5),
    }


def reference(x_prompt, x_sample, cache_win_k, cache_win_v, state_shift, state_wkv,
              rel_bias, ln_g, ln_b, attn_w_qkv, attn_b_qkv, attn_w_o, attn_b_o, attn_sinks,
              rwkv_mix, rwkv_w_rkv, rwkv_w0, rwkv_w1, rwkv_w2, rwkv_a0, rwkv_a1, rwkv_a2,
              rwkv_v0, rwkv_v1, rwkv_v2, rwkv_g1, rwkv_g2, rwkv_k_k, rwkv_k_a, rwkv_r_k,
              rwkv_ln_w, rwkv_ln_b, rwkv_w_o, ffn_w_gu, ffn_w_down):
    P = dict(rel_bias=rel_bias, ln_g=ln_g, ln_b=ln_b,
             attn_w_qkv=attn_w_qkv, attn_b_qkv=attn_b_qkv, attn_w_o=attn_w_o,
             attn_b_o=attn_b_o, attn_sinks=attn_sinks,
             rwkv_mix=rwkv_mix, rwkv_w_rkv=rwkv_w_rkv, rwkv_w0=rwkv_w0, rwkv_w1=rwkv_w1,
             rwkv_w2=rwkv_w2, rwkv_a0=rwkv_a0, rwkv_a1=rwkv_a1, rwkv_a2=rwkv_a2,
             rwkv_v0=rwkv_v0, rwkv_v1=rwkv_v1, rwkv_v2=rwkv_v2, rwkv_g1=rwkv_g1,
             rwkv_g2=rwkv_g2, rwkv_k_k=rwkv_k_k, rwkv_k_a=rwkv_k_a, rwkv_r_k=rwkv_r_k,
             rwkv_ln_w=rwkv_ln_w, rwkv_ln_b=rwkv_ln_b, rwkv_w_o=rwkv_w_o,
             ffn_w_gu=ffn_w_gu, ffn_w_down=ffn_w_down)
    y_prompt, pk, pv, ps, pw = trunk(x_prompt, None, None, None, None, P)
    y_sample, sk, sv, ss, sw = trunk(x_sample, cache_win_k, cache_win_v, state_shift, state_wkv, P)
    return (y_prompt, y_sample, pk, pv, ps, pw, sk, sv, ss, sw)
```

```python
import functools
import math

import numpy as np
import jax
import jax.numpy as jnp
from jax import lax
from jax.experimental import pallas as pl
from jax.experimental.pallas import tpu as pltpu

D_MODEL = 1024
DEPTH = 4
HEAD_DIM = 64
N_HEADS = 16
N_KV_HEADS = 4
GROUP = 4
WINDOW = 128
N_BUCKETS = 32
MAX_DISTANCE = 128
RWKV_HEAD = 64
RWKV_HEADS = 16
GN_EPS = 64e-5
D_FF = 2816
ALPHA = (2 * DEPTH) ** 0.25
LN_EPS = 1e-5
NEG = -1e30

BF = jnp.bfloat16
F32 = jnp.float32
VMEM_LIMIT = 56 * 1024 * 1024

NT_DIMS = (((1,), (1,)), ((), ()))
TN_DIMS = (((0,), (0,)), ((), ()))


def _cparams(*sem):
    return pltpu.CompilerParams(dimension_semantics=sem, vmem_limit_bytes=VMEM_LIMIT)


def _dot(a, b):
    return jnp.dot(a, b, preferred_element_type=F32)


def _layer_norm(z, g, b):
    mu = jnp.mean(z, axis=-1, keepdims=True)
    zc = z - mu
    var = jnp.mean(zc * zc, axis=-1, keepdims=True)
    return zc * lax.rsqrt(var + LN_EPS) * g + b


def _sigmoid(z):
    return 1.0 / (1.0 + jnp.exp(-z))


def _split2(z):
    hi = z.astype(BF)
    lo = (z - hi.astype(F32)).astype(BF)
    return hi, lo


def _split3(z):
    hi = z.astype(BF)
    r1 = z - hi.astype(F32)
    mid = r1.astype(BF)
    lo = (r1 - mid.astype(F32)).astype(BF)
    return hi, mid, lo


def _full(shape):
    n = len(shape)
    return pl.BlockSpec(shape, lambda *_: (0,) * n)


def _bias_table_kernel(rb_ref, bucket_ref, valid_ref, o_ref):
    bucket = bucket_ref[...]
    valid = valid_ref[...] > 0
    for h in range(N_HEADS):
        acc = jnp.zeros(bucket.shape, F32)
        for b in range(N_BUCKETS):
            acc = jnp.where(bucket == b, rb_ref[b, h], acc)
        o_ref[h] = jnp.where(valid, acc, NEG)


def _bias_table(rel_bias, d):
    dc = np.maximum(d, 0)
    exact = N_BUCKETS // 2
    df = np.maximum(dc, 1).astype(np.float32)
    large = exact + (np.log(df / np.float32(exact)) / np.float32(math.log(MAX_DISTANCE / exact))
                     * np.float32(N_BUCKETS - exact)).astype(np.int32)
    bucket = np.where(dc < exact, dc, np.minimum(large, N_BUCKETS - 1)).astype(np.int32)
    valid = ((d >= 0) & (d < WINDOW)).astype(np.int32)
    return pl.pallas_call(
        _bias_table_kernel,
        out_shape=jax.ShapeDtypeStruct((N_HEADS,) + d.shape, F32),
        in_specs=[pl.BlockSpec(memory_space=pltpu.SMEM),
                  pl.BlockSpec(memory_space=pltpu.VMEM),
                  pl.BlockSpec(memory_space=pltpu.VMEM)],
        out_specs=pl.BlockSpec(memory_space=pltpu.VMEM),
        name="bias_table",
    )(rel_bias, jnp.asarray(bucket), jnp.asarray(valid))


def _qkv_kernel(x_ref, w_ref, b_ref, q_ref, k_ref, v_ref):
    acc = _dot(x_ref[...].astype(BF), w_ref[...]) + b_ref[...]
    nq = N_HEADS * HEAD_DIM
    nk = N_KV_HEADS * HEAD_DIM
    q_ref[...] = (acc[:, :nq] * HEAD_DIM ** -0.5).astype(q_ref.dtype)
    k_ref[...] = acc[:, nq:nq + nk]
    v_ref[...] = acc[:, nq + nk:]


def _qkv_proj(x2, w, b, q_dtype, tm=512):
    n = x2.shape[0]
    tm = min(tm, n)
    nq = N_HEADS * HEAD_DIM
    nk = N_KV_HEADS * HEAD_DIM
    return pl.pallas_call(
        _qkv_kernel,
        grid=(n // tm,),
        in_specs=[pl.BlockSpec((tm, D_MODEL), lambda i: (i, 0)),
                  _full(w.shape), _full(b.shape)],
        out_specs=[pl.BlockSpec((tm, nq), lambda i: (i, 0)),
                   pl.BlockSpec((tm, nk), lambda i: (i, 0)),
                   pl.BlockSpec((tm, nk), lambda i: (i, 0))],
        out_shape=[jax.ShapeDtypeStruct((n, nq), q_dtype),
                   jax.ShapeDtypeStruct((n, nk), F32),
                   jax.ShapeDtypeStruct((n, nk), F32)],
        compiler_params=_cparams("parallel"),
        name="qkv_proj",
    )(x2, w, b)


def _swa_prompt_kernel(sink_ref, q_ref, kc_ref, kp_ref, vc_ref, vp_ref, bias_ref, o_ref):
    blk = pl.program_id(1)
    kcat = jnp.concatenate([kp_ref[0], kc_ref[0]], axis=0).astype(BF)
    vcat = jnp.concatenate([vp_ref[0], vc_ref[0]], axis=0).astype(BF)
    col = lax.broadcasted_iota(jnp.int32, (WINDOW, 2 * WINDOW), 1)
    no_prev = jnp.logical_and(blk == 0, col < WINDOW)
    outs = []
    for h in range(N_HEADS):
        g = h // GROUP
        qh = q_ref[0, :, h * HEAD_DIM:(h + 1) * HEAD_DIM]
        kh = kcat[:, g * HEAD_DIM:(g + 1) * HEAD_DIM]
        vh = vcat[:, g * HEAD_DIM:(g + 1) * HEAD_DIM]
        s = lax.dot_general(qh, kh, NT_DIMS, preferred_element_type=F32) + bias_ref[h]
        s = jnp.where(no_prev, NEG, s)
        sink = sink_ref[h]
        m = jnp.maximum(jnp.max(s, axis=-1, keepdims=True), sink)
        p = jnp.exp(s - m)
        den = jnp.sum(p, axis=-1, keepdims=True) + jnp.exp(sink - m)
        outs.append(_dot(p.astype(BF), vh) / den)
    o_ref[0] = jnp.concatenate(outs, axis=-1).astype(o_ref.dtype)


def _swa_prompt(q, k, v, bias, sinks):
    B, T, _ = q.shape
    nb = T // WINDOW
    nk = N_KV_HEADS * HEAD_DIM
    cur = lambda b, i: (b, i, 0)
    prev = lambda b, i: (b, jnp.maximum(i - 1, 0), 0)
    return pl.pallas_call(
        _swa_prompt_kernel,
        grid=(B, nb),
        in_specs=[pl.BlockSpec(memory_space=pltpu.SMEM),
                  pl.BlockSpec((1, WINDOW, D_MODEL), cur),
                  pl.BlockSpec((1, WINDOW, nk), cur),
                  pl.BlockSpec((1, WINDOW, nk), prev),
                  pl.BlockSpec((1, WINDOW, nk), cur),
                  pl.BlockSpec((1, WINDOW, nk), prev),
                  _full(bias.shape)],
        out_specs=pl.BlockSpec((1, WINDOW, D_MODEL), cur),
        out_shape=jax.ShapeDtypeStruct((B, T, D_MODEL), BF),
        compiler_params=_cparams("parallel", "parallel"),
        name="swa_prompt",
    )(sinks, q, k, k, v, v, bias)


def _swa_sample_kernel(T, sink_ref, q_ref, kn_ref, vn_ref, ck_ref, cv_ref, bc_ref, bn_ref,
                       o_ref, nk_ref, nv_ref):
    L = ck_ref.shape[1]
    kn = kn_ref[0]
    vn = vn_ref[0]
    ck = ck_ref[0]
    cv = cv_ref[0]
    nk_ref[0, :L - T, :] = ck[T:, :]
    nk_ref[0, L - T:, :] = kn
    nv_ref[0, :L - T, :] = cv[T:, :]
    nv_ref[0, L - T:, :] = vn
    q = q_ref[0]
    groups = []
    for g in range(N_KV_HEADS):
        cs = slice(g * HEAD_DIM, (g + 1) * HEAD_DIM)
        qg = jnp.concatenate([q[:, (g * GROUP + j) * HEAD_DIM:(g * GROUP + j + 1) * HEAD_DIM]
                              for j in range(GROUP)], axis=0).astype(BF)
        sink = jnp.concatenate([jnp.full((T, 1), sink_ref[g * GROUP + j], F32)
                                for j in range(GROUP)], axis=0)
        sc = lax.dot_general(qg, ck[:, cs].astype(BF), NT_DIMS, preferred_element_type=F32) + bc_ref[g]
        sn = lax.dot_general(qg, kn[:, cs].astype(BF), NT_DIMS, preferred_element_type=F32) + bn_ref[g]
        m = jnp.maximum(jnp.maximum(jnp.max(sc, axis=-1, keepdims=True),
                                    jnp.max(sn, axis=-1, keepdims=True)), sink)
        pc = jnp.exp(sc - m)
        pn = jnp.exp(sn - m)
        den = (jnp.sum(pc, axis=-1, keepdims=True) + jnp.sum(pn, axis=-1, keepdims=True)
               + jnp.exp(sink - m))
        og = (_dot(pc.astype(BF), cv[:, cs].astype(BF)) + _dot(pn.astype(BF), vn[:, cs].astype(BF))) / den
        groups.extend(og[j * T:(j + 1) * T] for j in range(GROUP))
    o_ref[0] = jnp.concatenate(groups, axis=-1).astype(o_ref.dtype)


def _swa_sample(q, kn, vn, ck, cv, bias, sinks):
    B, T, _ = q.shape
    L = ck.shape[1]
    nk = N_KV_HEADS * HEAD_DIM
    bias = bias.reshape(N_KV_HEADS, GROUP * T, L + T)
    bc, bn = bias[:, :, :L], bias[:, :, L:]
    row = lambda b: (b, 0, 0)
    return pl.pallas_call(
        functools.partial(_swa_sample_kernel, T),
        grid=(B,),
        in_specs=[pl.BlockSpec(memory_space=pltpu.SMEM),
                  pl.BlockSpec((1, T, D_MODEL), row),
                  pl.BlockSpec((1, T, nk), row),
                  pl.BlockSpec((1, T, nk), row),
                  pl.BlockSpec((1, L, nk), row),
                  pl.BlockSpec((1, L, nk), row),
                  _full(bc.shape), _full(bn.shape)],
        out_specs=[pl.BlockSpec((1, T, D_MODEL), row),
                   pl.BlockSpec((1, L, nk), row),
                   pl.BlockSpec((1, L, nk), row)],
        out_shape=[jax.ShapeDtypeStruct((B, T, D_MODEL), BF),
                   jax.ShapeDtypeStruct((B, L, nk), F32),
                   jax.ShapeDtypeStruct((B, L, nk), F32)],
        compiler_params=_cparams("parallel"),
        name="swa_sample",
    )(sinks, q, kn, vn, ck, cv, bc, bn)


def _proj_ln_kernel(a_ref, w_ref, b_ref, x_ref, g_ref, beta_ref, o_ref):
    y = _dot(a_ref[...].astype(BF), w_ref[...]) + b_ref[...]
    o_ref[...] = _layer_norm(ALPHA * x_ref[...] + y, g_ref[...], beta_ref[...])


def _proj_ln(a, w, b, x2, g, beta, tm=512):
    n, kdim = a.shape
    tm = min(tm, n)
    rows = lambda i: (i, 0)
    return pl.pallas_call(
        _proj_ln_kernel,
        grid=(n // tm,),
        in_specs=[pl.BlockSpec((tm, kdim), rows), _full(w.shape), _full(b.shape),
                  pl.BlockSpec((tm, D_MODEL), rows), _full(g.shape), _full(beta.shape)],
        out_specs=pl.BlockSpec((tm, D_MODEL), rows),
        out_shape=jax.ShapeDtypeStruct((n, D_MODEL), F32),
        compiler_params=_cparams("parallel"),
        name="proj_ln",
    )(a, w, b, x2, g, beta)


def _ffn_kernel(x_ref, wg_ref, wu_ref, wd_ref, g_ref, beta_ref, o_ref, acc_ref, xb_ref):
    j = pl.program_id(1)

    @pl.when(j == 0)
    def _():
        acc_ref[...] = jnp.zeros_like(acc_ref)
        xb_ref[...] = x_ref[...].astype(BF)

    xb = xb_ref[...]
    gt = _dot(xb, wg_ref[...])
    ut = _dot(xb, wu_ref[...])
    h = (gt * _sigmoid(gt) * ut).astype(BF)
    acc_ref[...] += _dot(h, wd_ref[...])

    @pl.when(j == pl.num_programs(1) - 1)
    def _():
        o_ref[...] = _layer_norm(ALPHA * x_ref[...] + acc_ref[...], g_ref[...], beta_ref[...])


def _ffn(x2, w_gu, w_down, g, beta, tm=512, tf=1408):
    n = x2.shape[0]
    tm = min(tm, n)
    nf = D_FF // tf
    rows = lambda i, j: (i, 0)
    return pl.pallas_call(
        _ffn_kernel,
        grid=(n // tm, nf),
        in_specs=[pl.BlockSpec((tm, D_MODEL), rows),
                  pl.BlockSpec((D_MODEL, tf), lambda i, j: (0, j)),
                  pl.BlockSpec((D_MODEL, tf), lambda i, j: (0, j + nf)),
                  pl.BlockSpec((tf, D_MODEL), lambda i, j: (j, 0)),
                  pl.BlockSpec((1, D_MODEL), lambda i, j: (0, 0)),
                  pl.BlockSpec((1, D_MODEL), lambda i, j: (0, 0))],
        out_specs=pl.BlockSpec((tm, D_MODEL), rows),
        out_shape=jax.ShapeDtypeStruct((n, D_MODEL), F32),
        scratch_shapes=[pltpu.VMEM((tm, D_MODEL), F32), pltpu.VMEM((tm, D_MODEL), BF)],
        compiler_params=_cparams("parallel", "arbitrary"),
        name="ffn",
    )(x2, w_gu, w_gu, w_down, g, beta)


def _head_sum(z, m_ref):
    hi, lo = _split2(z)
    m = m_ref[...]
    return _dot(hi, m) + _dot(lo, m)


def _rwkv_proj_kernel(has_vres, *refs):
    if has_vres:
        (x_ref, xp_ref, mix_ref, vec_ref, wr_ref, wk_ref, wv_ref, w1_ref, w2_ref, a1_ref, a2_ref,
         g1_ref, g2_ref, m_ref, v1_ref, v2_ref, vf_ref,
         r_ref, w_ref, k_ref, v_ref, kk_ref, a_ref, g_ref, carry_ref) = refs
    else:
        (x_ref, xp_ref, mix_ref, vec_ref, wr_ref, wk_ref, wv_ref, w1_ref, w2_ref, a1_ref, a2_ref,
         g1_ref, g2_ref, m_ref,
         r_ref, w_ref, k_ref, v_ref, kk_ref, a_ref, g_ref, carry_ref) = refs
    bb, tt, D = x_ref.shape
    n = bb * tt
    t = pl.program_id(1)
    x3 = x_ref[...]
    prev = jnp.where(t == 0, xp_ref[...], carry_ref[...])
    carry_ref[...] = x3[:, tt - 1:tt, :]
    x = x3.reshape(n, D)
    prev_rows = jnp.broadcast_to(prev, (bb, tt, D)).reshape(n, D)
    row = lax.broadcasted_iota(jnp.int32, (n, D), 0)
    xs = jnp.where(row % tt == 0, prev_rows, pltpu.roll(x, 1, 0))
    xx = xs - x
    xr, xw, xk, xv, xa, xg = ((x + xx * mix_ref[c:c + 1, :]).astype(BF) for c in range(6))
    w0, a0, v0, k_k, k_a = (vec_ref[c:c + 1, :] for c in range(5))

    r = _dot(xr, wr_ref[...])
    k = _dot(xk, wk_ref[...])
    v = _dot(xv, wv_ref[...])
    z = w0 + _dot(jnp.tanh(_dot(xw, w1_ref[...])).astype(BF), w2_ref[...])
    w = -(jnp.maximum(-z, 0.0) + jnp.log1p(jnp.exp(-jnp.abs(z)))) - 0.5
    logd = -jnp.exp(w)
    if has_vres:
        gate = _sigmoid(v0 + _dot(_dot(xv, v1_ref[...]).astype(BF), v2_ref[...]))
        v = v + (vf_ref[...].reshape(n, D) - v) * gate
    a = _sigmoid(a0 + _dot(_dot(xa, a1_ref[...]).astype(BF), a2_ref[...]))
    g = _dot(_sigmoid(_dot(xg, g1_ref[...])).astype(BF), g2_ref[...])
    kk = k * k_k
    kk = kk / jnp.maximum(jnp.sqrt(_head_sum(kk * kk, m_ref)), 1e-12)
    k = k * (1.0 + (a - 1.0) * k_a)
    for ref, val in ((r_ref, r), (w_ref, logd), (k_ref, k), (v_ref, v), (kk_ref, kk), (a_ref, a),
                     (g_ref, g)):
        ref[...] = val.reshape(bb, tt, D)


def _rwkv_proj(x, x_prev, v_first, mix, vecs, mats, head_ones, bb, tt):
    B, T, D = x.shape
    has_vres = v_first is not None
    blk = pl.BlockSpec((bb, tt, D), lambda b, t: (b, t, 0))
    ins = [x, x_prev, mix, vecs] + list(mats[:9]) + [head_ones]
    in_specs = [blk, pl.BlockSpec((bb, 1, D), lambda b, t: (b, 0, 0))]
    in_specs += [_full(a.shape) for a in ins[2:]]
    if has_vres:
        ins += [mats[9], mats[10], v_first]
        in_specs += [_full(mats[9].shape), _full(mats[10].shape), blk]
    return pl.pallas_call(
        functools.partial(_rwkv_proj_kernel, has_vres),
        grid=(B // bb, T // tt),
        in_specs=in_specs,
        out_specs=[blk] * 7,
        out_shape=[jax.ShapeDtypeStruct((B, T, D), F32)] * 7,
        scratch_shapes=[pltpu.VMEM((bb, 1, D), F32)],
        compiler_params=_cparams("parallel", "arbitrary"),
        name="rwkv_proj",
    )(*ins)


def _wkv_kernel(has_s0, *refs):
    if has_s0:
        r_ref, w_ref, k_ref, v_ref, kk_ref, a_ref, s0_ref, o_ref, so_ref, s_scr = refs
    else:
        r_ref, w_ref, k_ref, v_ref, kk_ref, a_ref, o_ref, so_ref, s_scr = refs
    C = r_ref.shape[1]
    N = RWKV_HEAD
    t = pl.program_id(1)

    @pl.when(t == 0)
    def _():
        if has_s0:
            s_scr[...] = s0_ref[0]
        else:
            s_scr[...] = jnp.zeros_like(s_scr)

    logd = w_ref[0]
    ri = lax.broadcasted_iota(jnp.int32, (C, C), 0)
    ci = lax.broadcasted_iota(jnp.int32, (C, C), 1)
    incl = ci <= ri
    strict = ci < ri
    tril = incl.astype(BF)
    cum = sum(_dot(tril, part) for part in _split3(logd))
    g_in = jnp.exp(cum)
    g_ex = jnp.exp(cum - logd)
    g_inv = jnp.exp(-cum)
    kk = kk_ref[0]
    at = -(kk * g_ex)
    rt = r_ref[0] * g_in
    bt = kk * a_ref[0] * g_inv
    kt = k_ref[0] * g_inv
    v = v_ref[0]
    g_end = g_in[C - 1:C, :]
    nsteps = max(1, int(math.log2(C)))
    outs = []
    for h in range(RWKV_HEADS):
        cs = slice(h * N, (h + 1) * N)
        X = jnp.concatenate([at[:, cs], rt[:, cs]], axis=0).astype(BF)
        Y = jnp.concatenate([bt[:, cs], kt[:, cs]], axis=0).astype(BF)
        vh = v[:, cs]
        S0 = s_scr[h]
        G = lax.dot_general(X, Y, NT_DIMS, preferred_element_type=F32)
        Z = lax.dot_general(X, S0.astype(BF), NT_DIMS, preferred_element_type=F32)
        A = jnp.where(strict, G[:C, :C], 0.0)
        Aak = jnp.where(strict, G[:C, C:], 0.0)
        Trb = jnp.where(incl, G[C:, :C], 0.0)
        Trk = jnp.where(incl, G[C:, C:], 0.0)
        W = Z[:C] + _dot(Aak.astype(BF), vh.astype(BF))
        for s in range(nsteps):
            Ab = A.astype(BF)
            W = W + _dot(Ab, W.astype(BF))
            if s + 1 < nsteps:
                A = _dot(Ab, Ab)
        UV = jnp.concatenate([W, vh], axis=0).astype(BF)
        T2 = jnp.concatenate([Trb, Trk], axis=1).astype(BF)
        outs.append(Z[C:] + _dot(T2, UV))
        upd = lax.dot_general(UV, Y, TN_DIMS, preferred_element_type=F32)
        s_scr[h] = (S0 + upd) * g_end[:, cs]
    o_ref[0] = jnp.concatenate(outs, axis=-1)

    @pl.when(t == pl.num_programs(1) - 1)
    def _():
        so_ref[0] = s_scr[...]


def _wkv(r, logd, k, v, kk, a, s0, C):
    B, T, D = r.shape
    has_s0 = s0 is not None
    blk = pl.BlockSpec((1, C, D), lambda b, t: (b, t, 0))
    sblk = pl.BlockSpec((1, RWKV_HEADS, RWKV_HEAD, RWKV_HEAD), lambda b, t: (b, 0, 0, 0))
    ins = [r, logd, k, v, kk, a] + ([s0] if has_s0 else [])
    return pl.pallas_call(
        functools.partial(_wkv_kernel, has_s0),
        grid=(B, T // C),
        in_specs=[blk] * 6 + ([sblk] if has_s0 else []),
        out_specs=[blk, sblk],
        out_shape=[jax.ShapeDtypeStruct((B, T, D), F32),
                   jax.ShapeDtypeStruct((B, RWKV_HEADS, RWKV_HEAD, RWKV_HEAD), F32)],
        scratch_shapes=[pltpu.VMEM((RWKV_HEADS, RWKV_HEAD, RWKV_HEAD), F32)],
        compiler_params=_cparams("parallel", "arbitrary"),
        name="wkv",
    )(*ins)


def _rwkv_out_kernel(o_ref, r_ref, k_ref, v_ref, gate_ref, x_ref, vec_ref, m_ref, wo_ref,
                     g_ref, beta_ref, y_ref):
    r_k, ln_w, ln_b = (vec_ref[c:c + 1, :] for c in range(3))
    o = o_ref[...]
    inv_n = 1.0 / RWKV_HEAD
    mu = _head_sum(o, m_ref) * inv_n
    oc = o - mu
    var = _head_sum(oc * oc, m_ref) * inv_n
    on = oc * lax.rsqrt(var + GN_EPS) * ln_w + ln_b
    v = v_ref[...]
    bonus = _head_sum(r_ref[...] * k_ref[...] * r_k, m_ref) * v
    y = _dot(((on + bonus) * gate_ref[...]).astype(BF), wo_ref[...])
    y_ref[...] = _layer_norm(ALPHA * x_ref[...] + y, g_ref[...], beta_ref[...])


def _rwkv_out(o, r, k, v, gate, x2, vecs, head_ones, w_o, g, beta, tm=256):
    n = o.shape[0]
    tm = min(tm, n)
    rows = pl.BlockSpec((tm, D_MODEL), lambda i: (i, 0))
    return pl.pallas_call(
        _rwkv_out_kernel,
        grid=(n // tm,),
        in_specs=[rows] * 6 + [_full(vecs.shape), _full(head_ones.shape), _full(w_o.shape),
                               _full(g.shape), _full(beta.shape)],
        out_specs=rows,
        out_shape=jax.ShapeDtypeStruct((n, D_MODEL), F32),
        compiler_params=_cparams("parallel"),
        name="rwkv_out",
    )(o, r, k, v, gate, x2, vecs, head_ones, w_o, g, beta)


def _trunk(x, win_k, win_v, shift, wkv, P, bias):
    prompt = win_k is None
    B, T, D = x.shape
    n = B * T
    nkv = N_KV_HEADS * HEAD_DIM
    v_first = None
    nk, nv, ns, nw = [], [], [], []
    for i in range(DEPTH):
        j = i // 2
        ln_g, ln_b = P['ln_g'][i], P['ln_b'][i]
        x2 = x.reshape(n, D)
        if i % 2 == 0:
            q, k, v = _qkv_proj(x2, P['attn_w_qkv'][j], P['attn_b_qkv'][j:j + 1],
                                BF if prompt else F32)
            q, k, v = q.reshape(B, T, D), k.reshape(B, T, nkv), v.reshape(B, T, nkv)
            if prompt:
                o = _swa_prompt(q, k, v, bias, P['attn_sinks'][j])
                k_buf, v_buf = k[:, T - WINDOW:], v[:, T - WINDOW:]
            else:
                L = win_k.shape[2]
                o, k_buf, v_buf = _swa_sample(q, k, v, win_k[j].reshape(B, L, nkv),
                                              win_v[j].reshape(B, L, nkv), bias, P['attn_sinks'][j])
            nk.append(k_buf.reshape(B, -1, N_KV_HEADS, HEAD_DIM))
            nv.append(v_buf.reshape(B, -1, N_KV_HEADS, HEAD_DIM))
            x2 = _proj_ln(o.reshape(n, D), P['attn_w_o'][j], P['attn_b_o'][j:j + 1], x2,
                          ln_g[0:1], ln_b[0:1])
        else:
            if prompt:
                x_prev = jnp.zeros((B, 1, D), x.dtype)
                s0 = None
            else:
                x_prev, s0 = shift[j].reshape(B, 1, D), wkv[j]
            if j == 0:
                vecs = jnp.stack([P['rwkv_w0'][j], P['rwkv_a0'][j], P['rwkv_a0'][j],
                                  P['rwkv_k_k'][j], P['rwkv_k_a'][j]])
                vres = ()
            else:
                vecs = jnp.stack([P['rwkv_w0'][j], P['rwkv_a0'][j], P['rwkv_v0'][j - 1],
                                  P['rwkv_k_k'][j], P['rwkv_k_a'][j]])
                vres = (P['rwkv_v1'][j - 1], P['rwkv_v2'][j - 1])
            mats = (P['rwkv_w_rkv'][j, 0], P['rwkv_w_rkv'][j, 1], P['rwkv_w_rkv'][j, 2],
                    P['rwkv_w1'][j], P['rwkv_w2'][j], P['rwkv_a1'][j], P['rwkv_a2'][j],
                    P['rwkv_g1'][j], P['rwkv_g2'][j]) + vres
            bb, tt = (1, 256) if prompt else (min(32, B), T)
            r, logd, k, v, kk, a, gate = _rwkv_proj(
                x, x_prev, v_first if j > 0 else None, P['rwkv_mix'][j], vecs, mats,
                P['head_ones'], bb, tt)
            if j == 0:
                v_first = v
            o, s_new = _wkv(r, logd, k, v, kk, a, s0, 64 if prompt else T)
            ns.append(x[:, -1])
            nw.append(s_new)
            ovec = jnp.stack([P['rwkv_r_k'][j].reshape(D), P['rwkv_ln_w'][j], P['rwkv_ln_b'][j]])
            x2 = _rwkv_out(o.reshape(n, D), r.reshape(n, D), k.reshape(n, D), v.reshape(n, D),
                           gate.reshape(n, D), x2, ovec, P['head_ones'], P['rwkv_w_o'][j],
                           ln_g[0:1], ln_b[0:1])
        x2 = _ffn(x2, P['ffn_w_gu'][i], P['ffn_w_down'][i], ln_g[1:2], ln_b[1:2])
        x = x2.reshape(B, T, D)
    return x, jnp.stack(nk), jnp.stack(nv), jnp.stack(ns), jnp.stack(nw)


def kernel(x_prompt, x_sample, cache_win_k, cache_win_v, state_shift, state_wkv, rel_bias, ln_g, ln_b, attn_w_qkv, attn_b_qkv, attn_w_o, attn_b_o, attn_sinks, rwkv_mix, rwkv_w_rkv, rwkv_w0, rwkv_w1, rwkv_w2, rwkv_a0, rwkv_a1, rwkv_a2, rwkv_v0, rwkv_v1, rwkv_v2, rwkv_g1, rwkv_g2, rwkv_k_k, rwkv_k_a, rwkv_r_k, rwkv_ln_w, rwkv_ln_b, rwkv_w_o, ffn_w_gu, ffn_w_down):
    bf = lambda w: w.astype(BF)
    lane_head = np.arange(D_MODEL) // RWKV_HEAD
    head_ones = jnp.asarray(lane_head[:, None] == lane_head[None, :], dtype=BF)
    P = dict(ln_g=ln_g, ln_b=ln_b,
             attn_w_qkv=bf(attn_w_qkv), attn_b_qkv=attn_b_qkv, attn_w_o=bf(attn_w_o),
             attn_b_o=attn_b_o, attn_sinks=attn_sinks,
             rwkv_mix=rwkv_mix, rwkv_w_rkv=bf(rwkv_w_rkv), rwkv_w0=rwkv_w0, rwkv_w1=bf(rwkv_w1),
             rwkv_w2=bf(rwkv_w2), rwkv_a0=rwkv_a0, rwkv_a1=bf(rwkv_a1), rwkv_a2=bf(rwkv_a2),
             rwkv_v0=rwkv_v0, rwkv_v1=bf(rwkv_v1), rwkv_v2=bf(rwkv_v2), rwkv_g1=bf(rwkv_g1),
             rwkv_g2=bf(rwkv_g2), rwkv_k_k=rwkv_k_k, rwkv_k_a=rwkv_k_a, rwkv_r_k=rwkv_r_k,
             rwkv_ln_w=rwkv_ln_w, rwkv_ln_b=rwkv_ln_b, rwkv_w_o=bf(rwkv_w_o),
             ffn_w_gu=bf(ffn_w_gu), ffn_w_down=bf(ffn_w_down), head_ones=head_ones)

    T = x_sample.shape[1]
    L = cache_win_k.shape[2]
    d_prompt = np.arange(WINDOW)[:, None] + WINDOW - np.arange(2 * WINDOW)[None, :]
    d_sample = np.arange(T)[:, None] + L - np.arange(L + T)[None, :]
    bias_prompt = _bias_table(rel_bias, d_prompt)
    bias_sample = _bias_table(rel_bias, d_sample)

    y_prompt, pk, pv, ps, pw = _trunk(x_prompt, None, None, None, None, P, bias_prompt)
    y_sample, sk, sv, ss, sw = _trunk(x_sample, cache_win_k, cache_win_v, state_shift, state_wkv,
                                      P, bias_sample)
    return (y_prompt, y_sample, pk, pv, ps, pw, sk, sv, ss, sw)
```

```python
import functools
import math

import numpy as np
import jax
import jax.numpy as jnp
from jax import lax
from jax.experimental import pallas as pl
from jax.experimental.pallas import tpu as pltpu

D_MODEL = 1024
DEPTH = 4
HEAD_DIM = 64
N_HEADS = 16
N_KV_HEADS = 4
GROUP = 4
WINDOW = 128
N_BUCKETS = 32
MAX_DISTANCE = 128
RWKV_HEAD = 64
RWKV_HEADS = 16
GN_EPS = 64e-5
D_FF = 2816
ALPHA = (2 * DEPTH) ** 0.25
LN_EPS = 1e-5
NEG = -1e30
WKV_ROWS = 64

BF = jnp.bfloat16
F32 = jnp.float32
VMEM_LIMIT = 56 * 1024 * 1024

NT_DIMS = (((1,), (1,)), ((), ()))
TN_DIMS = (((0,), (0,)), ((), ()))


def _cparams(*sem):
    return pltpu.CompilerParams(dimension_semantics=sem, vmem_limit_bytes=VMEM_LIMIT)


def _dot(a, b):
    return jnp.dot(a, b, preferred_element_type=F32)


def _layer_norm(z, g, b):
    mu = jnp.mean(z, axis=-1, keepdims=True)
    zc = z - mu
    var = jnp.mean(zc * zc, axis=-1, keepdims=True)
    return zc * lax.rsqrt(var + LN_EPS) * g + b


def _sigmoid(z):
    return 1.0 / (1.0 + jnp.exp(-z))


def _split2(z):
    hi = z.astype(BF)
    lo = (z - hi.astype(F32)).astype(BF)
    return hi, lo


def _split3(z):
    hi = z.astype(BF)
    r1 = z - hi.astype(F32)
    mid = r1.astype(BF)
    lo = (r1 - mid.astype(F32)).astype(BF)
    return hi, mid, lo


def _full(shape):
    n = len(shape)
    return pl.BlockSpec(shape, lambda *_: (0,) * n)


def _bias_table_kernel(rb_ref, bucket_ref, valid_ref, o_ref):
    bucket = bucket_ref[...]
    valid = valid_ref[...] > 0
    for h in range(N_HEADS):
        acc = jnp.zeros(bucket.shape, F32)
        for b in range(N_BUCKETS):
            acc = jnp.where(bucket == b, rb_ref[b, h], acc)
        o_ref[h] = jnp.where(valid, acc, NEG)


def _bias_table(rel_bias, d):
    dc = np.maximum(d, 0)
    exact = N_BUCKETS // 2
    df = np.maximum(dc, 1).astype(np.float32)
    large = exact + (np.log(df / np.float32(exact)) / np.float32(math.log(MAX_DISTANCE / exact))
                     * np.float32(N_BUCKETS - exact)).astype(np.int32)
    bucket = np.where(dc < exact, dc, np.minimum(large, N_BUCKETS - 1)).astype(np.int32)
    valid = ((d >= 0) & (d < WINDOW)).astype(np.int32)
    return pl.pallas_call(
        _bias_table_kernel,
        out_shape=jax.ShapeDtypeStruct((N_HEADS,) + d.shape, F32),
        in_specs=[pl.BlockSpec(memory_space=pltpu.SMEM),
                  pl.BlockSpec(memory_space=pltpu.VMEM),
                  pl.BlockSpec(memory_space=pltpu.VMEM)],
        out_specs=pl.BlockSpec(memory_space=pltpu.VMEM),
        name="bias_table",
    )(rel_bias, jnp.asarray(bucket), jnp.asarray(valid))


def _qkv_kernel(x_ref, w_ref, b_ref, q_ref, k_ref, v_ref):
    acc = _dot(x_ref[...].astype(BF), w_ref[...]) + b_ref[...]
    nq = N_HEADS * HEAD_DIM
    nk = N_KV_HEADS * HEAD_DIM
    q_ref[...] = (acc[:, :nq] * HEAD_DIM ** -0.5).astype(q_ref.dtype)
    k_ref[...] = acc[:, nq:nq + nk]
    v_ref[...] = acc[:, nq + nk:]


def _qkv_proj(x2, w, b, q_dtype, tm=512):
    n = x2.shape[0]
    tm = min(tm, n)
    nq = N_HEADS * HEAD_DIM
    nk = N_KV_HEADS * HEAD_DIM
    return pl.pallas_call(
        _qkv_kernel,
        grid=(n // tm,),
        in_specs=[pl.BlockSpec((tm, D_MODEL), lambda i: (i, 0)),
                  _full(w.shape), _full(b.shape)],
        out_specs=[pl.BlockSpec((tm, nq), lambda i: (i, 0)),
                   pl.BlockSpec((tm, nk), lambda i: (i, 0)),
                   pl.BlockSpec((tm, nk), lambda i: (i, 0))],
        out_shape=[jax.ShapeDtypeStruct((n, nq), q_dtype),
                   jax.ShapeDtypeStruct((n, nk), F32),
                   jax.ShapeDtypeStruct((n, nk), F32)],
        compiler_params=_cparams("parallel"),
        name="qkv_proj",
    )(x2, w, b)


def _swa_prompt_kernel(sink_ref, q_ref, kc_ref, kp_ref, vc_ref, vp_ref, bias_ref, o_ref):
    blk = pl.program_id(1)
    kcat = jnp.concatenate([kp_ref[0], kc_ref[0]], axis=0).astype(BF)
    vcat = jnp.concatenate([vp_ref[0], vc_ref[0]], axis=0).astype(BF)
    col = lax.broadcasted_iota(jnp.int32, (WINDOW, 2 * WINDOW), 1)
    no_prev = jnp.logical_and(blk == 0, col < WINDOW)
    outs = []
    for h in range(N_HEADS):
        g = h // GROUP
        qh = q_ref[0, :, h * HEAD_DIM:(h + 1) * HEAD_DIM]
        kh = kcat[:, g * HEAD_DIM:(g + 1) * HEAD_DIM]
        vh = vcat[:, g * HEAD_DIM:(g + 1) * HEAD_DIM]
        s = lax.dot_general(qh, kh, NT_DIMS, preferred_element_type=F32) + bias_ref[h]
        s = jnp.where(no_prev, NEG, s)
        sink = sink_ref[h]
        m = jnp.maximum(jnp.max(s, axis=-1, keepdims=True), sink)
        p = jnp.exp(s - m)
        den = jnp.sum(p, axis=-1, keepdims=True) + jnp.exp(sink - m)
        outs.append(_dot(p.astype(BF), vh) / den)
    o_ref[0] = jnp.concatenate(outs, axis=-1).astype(o_ref.dtype)


def _swa_prompt(q, k, v, bias, sinks):
    B, T, _ = q.shape
    nb = T // WINDOW
    nk = N_KV_HEADS * HEAD_DIM
    cur = lambda b, i: (b, i, 0)
    prev = lambda b, i: (b, jnp.maximum(i - 1, 0), 0)
    return pl.pallas_call(
        _swa_prompt_kernel,
        grid=(B, nb),
        in_specs=[pl.BlockSpec(memory_space=pltpu.SMEM),
                  pl.BlockSpec((1, WINDOW, D_MODEL), cur),
                  pl.BlockSpec((1, WINDOW, nk), cur),
                  pl.BlockSpec((1, WINDOW, nk), prev),
                  pl.BlockSpec((1, WINDOW, nk), cur),
                  pl.BlockSpec((1, WINDOW, nk), prev),
                  _full(bias.shape)],
        out_specs=pl.BlockSpec((1, WINDOW, D_MODEL), cur),
        out_shape=jax.ShapeDtypeStruct((B, T, D_MODEL), BF),
        compiler_params=_cparams("parallel", "parallel"),
        name="swa_prompt",
    )(sinks, q, k, k, v, v, bias)


def _swa_sample_kernel(T, sink_ref, q_ref, kn_ref, vn_ref, ck_ref, cv_ref, bc_ref, bn_ref,
                       o_ref, nk_ref, nv_ref):
    L = ck_ref.shape[1]
    kn = kn_ref[0]
    vn = vn_ref[0]
    ck = ck_ref[0]
    cv = cv_ref[0]
    nk_ref[0, :L - T, :] = ck[T:, :]
    nk_ref[0, L - T:, :] = kn
    nv_ref[0, :L - T, :] = cv[T:, :]
    nv_ref[0, L - T:, :] = vn
    q = q_ref[0]
    groups = []
    for g in range(N_KV_HEADS):
        cs = slice(g * HEAD_DIM, (g + 1) * HEAD_DIM)
        qg = jnp.concatenate([q[:, (g * GROUP + j) * HEAD_DIM:(g * GROUP + j + 1) * HEAD_DIM]
                              for j in range(GROUP)], axis=0).astype(BF)
        sink = jnp.concatenate([jnp.full((T, 1), sink_ref[g * GROUP + j], F32)
                                for j in range(GROUP)], axis=0)
        sc = lax.dot_general(qg, ck[:, cs].astype(BF), NT_DIMS, preferred_element_type=F32) + bc_ref[g]
        sn = lax.dot_general(qg, kn[:, cs].astype(BF), NT_DIMS, preferred_element_type=F32) + bn_ref[g]
        m = jnp.maximum(jnp.maximum(jnp.max(sc, axis=-1, keepdims=True),
                                    jnp.max(sn, axis=-1, keepdims=True)), sink)
        pc = jnp.exp(sc - m)
        pn = jnp.exp(sn - m)
        den = (jnp.sum(pc, axis=-1, keepdims=True) + jnp.sum(pn, axis=-1, keepdims=True)
               + jnp.exp(sink - m))
        og = (_dot(pc.astype(BF), cv[:, cs].astype(BF)) + _dot(pn.astype(BF), vn[:, cs].astype(BF))) / den
        groups.extend(og[j * T:(j + 1) * T] for j in range(GROUP))
    o_ref[0] = jnp.concatenate(groups, axis=-1).astype(o_ref.dtype)


def _swa_sample(q, kn, vn, ck, cv, bias, sinks):
    B, T, _ = q.shape
    L = ck.shape[1]
    nk = N_KV_HEADS * HEAD_DIM
    bias = bias.reshape(N_KV_HEADS, GROUP * T, L + T)
    bc, bn = bias[:, :, :L], bias[:, :, L:]
    row = lambda b: (b, 0, 0)
    return pl.pallas_call(
        functools.partial(_swa_sample_kernel, T),
        grid=(B,),
        in_specs=[pl.BlockSpec(memory_space=pltpu.SMEM),
                  pl.BlockSpec((1, T, D_MODEL), row),
                  pl.BlockSpec((1, T, nk), row),
                  pl.BlockSpec((1, T, nk), row),
                  pl.BlockSpec((1, L, nk), row),
                  pl.BlockSpec((1, L, nk), row),
                  _full(bc.shape), _full(bn.shape)],
        out_specs=[pl.BlockSpec((1, T, D_MODEL), row),
                   pl.BlockSpec((1, L, nk), row),
                   pl.BlockSpec((1, L, nk), row)],
        out_shape=[jax.ShapeDtypeStruct((B, T, D_MODEL), BF),
                   jax.ShapeDtypeStruct((B, L, nk), F32),
                   jax.ShapeDtypeStruct((B, L, nk), F32)],
        compiler_params=_cparams("parallel"),
        name="swa_sample",
    )(sinks, q, kn, vn, ck, cv, bc, bn)


def _proj_ln_kernel(a_ref, w_ref, b_ref, x_ref, g_ref, beta_ref, o_ref):
    y = _dot(a_ref[...].astype(BF), w_ref[...]) + b_ref[...]
    o_ref[...] = _layer_norm(ALPHA * x_ref[...] + y, g_ref[...], beta_ref[...])


def _proj_ln(a, w, b, x2, g, beta, tm=512):
    n, kdim = a.shape
    tm = min(tm, n)
    rows = lambda i: (i, 0)
    return pl.pallas_call(
        _proj_ln_kernel,
        grid=(n // tm,),
        in_specs=[pl.BlockSpec((tm, kdim), rows), _full(w.shape), _full(b.shape),
                  pl.BlockSpec((tm, D_MODEL), rows), _full(g.shape), _full(beta.shape)],
        out_specs=pl.BlockSpec((tm, D_MODEL), rows),
        out_shape=jax.ShapeDtypeStruct((n, D_MODEL), F32),
        compiler_params=_cparams("parallel"),
        name="proj_ln",
    )(a, w, b, x2, g, beta)


def _ffn_kernel(x_ref, wg_ref, wu_ref, wd_ref, g_ref, beta_ref, o_ref, acc_ref, xb_ref):
    j = pl.program_id(1)

    @pl.when(j == 0)
    def _():
        acc_ref[...] = jnp.zeros_like(acc_ref)
        xb_ref[...] = x_ref[...].astype(BF)

    xb = xb_ref[...]
    gt = _dot(xb, wg_ref[...])
    ut = _dot(xb, wu_ref[...])
    h = (gt * _sigmoid(gt) * ut).astype(BF)
    acc_ref[...] += _dot(h, wd_ref[...])

    @pl.when(j == pl.num_programs(1) - 1)
    def _():
        o_ref[...] = _layer_norm(ALPHA * x_ref[...] + acc_ref[...], g_ref[...], beta_ref[...])


def _ffn(x2, w_gu, w_down, g, beta, tm=512, tf=1408):
    n = x2.shape[0]
    tm = min(tm, n)
    nf = D_FF // tf
    rows = lambda i, j: (i, 0)
    return pl.pallas_call(
        _ffn_kernel,
        grid=(n // tm, nf),
        in_specs=[pl.BlockSpec((tm, D_MODEL), rows),
                  pl.BlockSpec((D_MODEL, tf), lambda i, j: (0, j)),
                  pl.BlockSpec((D_MODEL, tf), lambda i, j: (0, j + nf)),
                  pl.BlockSpec((tf, D_MODEL), lambda i, j: (j, 0)),
                  pl.BlockSpec((1, D_MODEL), lambda i, j: (0, 0)),
                  pl.BlockSpec((1, D_MODEL), lambda i, j: (0, 0))],
        out_specs=pl.BlockSpec((tm, D_MODEL), rows),
        out_shape=jax.ShapeDtypeStruct((n, D_MODEL), F32),
        scratch_shapes=[pltpu.VMEM((tm, D_MODEL), F32), pltpu.VMEM((tm, D_MODEL), BF)],
        compiler_params=_cparams("parallel", "arbitrary"),
        name="ffn",
    )(x2, w_gu, w_gu, w_down, g, beta)


def _head_sum(z, m_ref):
    hi, lo = _split2(z)
    m = m_ref[...]
    return _dot(hi, m) + _dot(lo, m)


def _rwkv_proj_kernel(has_vres, *refs):
    if has_vres:
        (x_ref, xp_ref, mix_ref, vec_ref, wr_ref, wk_ref, wv_ref, w1_ref, w2_ref, a1_ref, a2_ref,
         g1_ref, g2_ref, m_ref, v1_ref, v2_ref, vf_ref,
         r_ref, w_ref, k_ref, v_ref, kk_ref, a_ref, g_ref, carry_ref) = refs
    else:
        (x_ref, xp_ref, mix_ref, vec_ref, wr_ref, wk_ref, wv_ref, w1_ref, w2_ref, a1_ref, a2_ref,
         g1_ref, g2_ref, m_ref,
         r_ref, w_ref, k_ref, v_ref, kk_ref, a_ref, g_ref, carry_ref) = refs
    bb, tt, D = x_ref.shape
    n = bb * tt
    t = pl.program_id(1)
    x3 = x_ref[...]
    prev = jnp.where(t == 0, xp_ref[...], carry_ref[...])
    carry_ref[...] = x3[:, tt - 1:tt, :]
    x = x3.reshape(n, D)
    prev_rows = jnp.broadcast_to(prev, (bb, tt, D)).reshape(n, D)
    row = lax.broadcasted_iota(jnp.int32, (n, D), 0)
    xs = jnp.where(row % tt == 0, prev_rows, pltpu.roll(x, 1, 0))
    xx = xs - x
    xr, xw, xk, xv, xa, xg = ((x + xx * mix_ref[c:c + 1, :]).astype(BF) for c in range(6))
    w0, a0, v0, k_k, k_a = (vec_ref[c:c + 1, :] for c in range(5))

    r = _dot(xr, wr_ref[...])
    k = _dot(xk, wk_ref[...])
    v = _dot(xv, wv_ref[...])
    z = w0 + _dot(jnp.tanh(_dot(xw, w1_ref[...])).astype(BF), w2_ref[...])
    w = -(jnp.maximum(-z, 0.0) + jnp.log1p(jnp.exp(-jnp.abs(z)))) - 0.5
    logd = -jnp.exp(w)
    if has_vres:
        gate = _sigmoid(v0 + _dot(_dot(xv, v1_ref[...]).astype(BF), v2_ref[...]))
        v = v + (vf_ref[...].reshape(n, D) - v) * gate
    a = _sigmoid(a0 + _dot(_dot(xa, a1_ref[...]).astype(BF), a2_ref[...]))
    g = _dot(_sigmoid(_dot(xg, g1_ref[...])).astype(BF), g2_ref[...])
    kk = k * k_k
    kk = kk / jnp.maximum(jnp.sqrt(_head_sum(kk * kk, m_ref)), 1e-12)
    k = k * (1.0 + (a - 1.0) * k_a)
    for ref, val in ((r_ref, r), (w_ref, logd), (k_ref, k), (v_ref, v), (kk_ref, kk), (a_ref, a),
                     (g_ref, g)):
        ref[...] = val.reshape(bb, tt, D)


def _rwkv_proj(x, x_prev, v_first, mix, vecs, mats, head_ones, bb, tt):
    B, T, D = x.shape
    has_vres = v_first is not None
    blk = pl.BlockSpec((bb, tt, D), lambda b, t: (b, t, 0))
    ins = [x, x_prev, mix, vecs] + list(mats[:9]) + [head_ones]
    in_specs = [blk, pl.BlockSpec((bb, 1, D), lambda b, t: (b, 0, 0))]
    in_specs += [_full(a.shape) for a in ins[2:]]
    if has_vres:
        ins += [mats[9], mats[10], v_first]
        in_specs += [_full(mats[9].shape), _full(mats[10].shape), blk]
    return pl.pallas_call(
        functools.partial(_rwkv_proj_kernel, has_vres),
        grid=(B // bb, T // tt),
        in_specs=in_specs,
        out_specs=[blk] * 7,
        out_shape=[jax.ShapeDtypeStruct((B, T, D), F32)] * 7,
        scratch_shapes=[pltpu.VMEM((bb, 1, D), F32)],
        compiler_params=_cparams("parallel", "arbitrary"),
        name="rwkv_proj",
    )(*ins)


def _wkv_kernel(has_s0, C, *refs):
    if has_s0:
        r_ref, w_ref, k_ref, v_ref, kk_ref, a_ref, s0_ref, o_ref, so_ref, s_scr = refs
    else:
        r_ref, w_ref, k_ref, v_ref, kk_ref, a_ref, o_ref, so_ref, s_scr = refs
    R = WKV_ROWS
    N = RWKV_HEAD
    NP = RWKV_HEADS // 2
    nseq = R // C
    t = pl.program_id(1)

    @pl.when(t == 0)
    def _():
        if has_s0:
            for s in range(nseq):
                for p in range(NP):
                    s_scr[s, p] = jnp.concatenate([s0_ref[s, 2 * p], s0_ref[s, 2 * p + 1]], axis=1)
        else:
            s_scr[...] = jnp.zeros_like(s_scr)

    logd = w_ref[0]
    ri = lax.broadcasted_iota(jnp.int32, (R, R), 0)
    ci = lax.broadcasted_iota(jnp.int32, (R, R), 1)
    tril = jnp.logical_and(ri // C == ci // C, ci <= ri).astype(BF)
    cum = sum(_dot(tril, part) for part in _split3(logd))
    g_in = jnp.exp(cum)
    g_ex = jnp.exp(cum - logd)
    g_inv = jnp.exp(-cum)
    kk = kk_ref[0]
    at = -(kk * g_ex)
    rt = r_ref[0] * g_in
    bt = kk * a_ref[0] * g_inv
    kt = k_ref[0] * g_inv
    v = v_ref[0]

    def lane_masks(width):
        rr = lax.broadcasted_iota(jnp.int32, (R, width), 0)
        cc = lax.broadcasted_iota(jnp.int32, (R, width), 1) % N
        same = rr // C == cc // C
        return jnp.logical_and(same, cc < rr), jnp.logical_and(same, cc <= rr)

    strict2, _ = lane_masks(2 * N)
    _, incl4 = lane_masks(4 * N)

    def bdiag(m):
        lo = lax.broadcasted_iota(jnp.int32, m.shape, 1) < N
        zero = jnp.zeros_like(m)
        return jnp.concatenate([jnp.where(lo, m, zero), jnp.where(lo, zero, m)], axis=0)

    pairs = range(NP)
    seqs = range(nseq)
    P = [slice(2 * N * p, 2 * N * (p + 1)) for p in pairs]
    rows = [slice(C * s, C * (s + 1)) for s in seqs]
    vp = [v[:, P[p]] for p in pairs]
    vpb = [x.astype(BF) for x in vp]
    btp = [bt[:, P[p]] for p in pairs]
    ktp = [kt[:, P[p]] for p in pairs]
    xa = [at[:, P[p]] for p in pairs]
    xr = [rt[:, P[p]] for p in pairs]
    S0 = [[s_scr[s, p] for p in pairs] for s in seqs]

    xp = [jnp.concatenate([xa[p], xr[p]], axis=0).astype(BF) for p in pairs]
    bmat = [jnp.concatenate([bdiag(btp[p].astype(BF)), bdiag(ktp[p].astype(BF))], axis=0)
            for p in pairs]
    G = [lax.dot_general(xp[p], bmat[p], NT_DIMS, preferred_element_type=F32) for p in pairs]
    Z = [[lax.dot_general(
        xp[p] if nseq == 1 else
        jnp.concatenate([xa[p][rows[s]], xr[p][rows[s]]], axis=0).astype(BF),
        bdiag(S0[s][p].astype(BF)), NT_DIMS, preferred_element_type=F32) for p in pairs] for s in seqs]
    za = [Z[0][p][:C] if nseq == 1 else jnp.concatenate([Z[s][p][:C] for s in seqs], axis=0)
          for p in pairs]
    zr = [Z[0][p][C:] if nseq == 1 else jnp.concatenate([Z[s][p][C:] for s in seqs], axis=0)
          for p in pairs]
    A = [jnp.where(strict2, G[p][:R, :2 * N], 0.0) for p in pairs]
    aak = [jnp.where(strict2, G[p][:R, 2 * N:], 0.0).astype(BF) for p in pairs]
    T2 = [jnp.where(incl4, G[p][R:], 0.0).astype(BF) for p in pairs]
    W = [za[p] + _dot(aak[p], bdiag(vpb[p])) for p in pairs]
    nsteps = max(1, int(math.log2(C)))
    for step in range(nsteps):
        Ab = [A[p].astype(BF) for p in pairs]
        if step + 1 < nsteps:
            res = [_dot(Ab[p], jnp.concatenate([bdiag(W[p].astype(BF)), bdiag(Ab[p])], axis=1))
                   for p in pairs]
            W = [W[p] + res[p][:, :2 * N] for p in pairs]
            A = [res[p][:, 2 * N:] for p in pairs]
        else:
            W = [W[p] + _dot(Ab[p], bdiag(W[p].astype(BF))) for p in pairs]
    wb = [W[p].astype(BF) for p in pairs]
    for p in pairs:
        uv = jnp.concatenate([bdiag(wb[p]), bdiag(vpb[p])], axis=0)
        o_ref[0, :, P[p]] = zr[p] + _dot(T2[p], uv)
    for s in seqs:
        for p in pairs:
            uvs = jnp.concatenate([W[p][rows[s]], vp[p][rows[s]]], axis=0).astype(BF)
            ys = jnp.concatenate([btp[p][rows[s]], ktp[p][rows[s]]], axis=0).astype(BF)
            full = lax.dot_general(uvs, ys, TN_DIMS, preferred_element_type=F32)
            lo = lax.broadcasted_iota(jnp.int32, (N, 2 * N), 1) < N
            upd = jnp.where(lo, full[:N], full[N:])
            g_end = g_in[C * (s + 1) - 1:C * (s + 1), P[p]]
            s_scr[s, p] = (S0[s][p] + upd) * g_end

    @pl.when(t == pl.num_programs(1) - 1)
    def _():
        for s in seqs:
            for p in pairs:
                so_ref[s, 2 * p] = s_scr[s, p][:, :N]
                so_ref[s, 2 * p + 1] = s_scr[s, p][:, N:]


def _wkv(r, logd, k, v, kk, a, s0, C):
    B, T, D = r.shape
    has_s0 = s0 is not None
    R = WKV_ROWS
    nseq = R // C
    nt = T // C
    assert (nseq == 1 or nt == 1) and B % nseq == 0 and T % C == 0
    nb = B // nseq
    blk = pl.BlockSpec((1, R, D), lambda b, t: (b * nt + t, 0, 0))
    sblk = pl.BlockSpec((nseq, RWKV_HEADS, RWKV_HEAD, RWKV_HEAD), lambda b, t: (b, 0, 0, 0))
    ins = [x.reshape(B * T // R, R, D) for x in (r, logd, k, v, kk, a)] + ([s0] if has_s0 else [])
    o, s_new = pl.pallas_call(
        functools.partial(_wkv_kernel, has_s0, C),
        grid=(nb, nt),
        in_specs=[blk] * 6 + ([sblk] if has_s0 else []),
        out_specs=[blk, sblk],
        out_shape=[jax.ShapeDtypeStruct((B * T // R, R, D), F32),
                   jax.ShapeDtypeStruct((B, RWKV_HEADS, RWKV_HEAD, RWKV_HEAD), F32)],
        scratch_shapes=[pltpu.VMEM((nseq, RWKV_HEADS // 2, RWKV_HEAD, 2 * RWKV_HEAD), F32)],
        compiler_params=_cparams("parallel", "arbitrary"),
        name="wkv",
    )(*ins)
    return o.reshape(B, T, D), s_new


def _rwkv_out_kernel(o_ref, r_ref, k_ref, v_ref, gate_ref, x_ref, vec_ref, m_ref, wo_ref,
                     g_ref, beta_ref, y_ref):
    r_k, ln_w, ln_b = (vec_ref[c:c + 1, :] for c in range(3))
    o = o_ref[...]
    inv_n = 1.0 / RWKV_HEAD
    mu = _head_sum(o, m_ref) * inv_n
    oc = o - mu
    var = _head_sum(oc * oc, m_ref) * inv_n
    on = oc * lax.rsqrt(var + GN_EPS) * ln_w + ln_b
    v = v_ref[...]
    bonus = _head_sum(r_ref[...] * k_ref[...] * r_k, m_ref) * v
    y = _dot(((on + bonus) * gate_ref[...]).astype(BF), wo_ref[...])
    y_ref[...] = _layer_norm(ALPHA * x_ref[...] + y, g_ref[...], beta_ref[...])


def _rwkv_out(o, r, k, v, gate, x2, vecs, head_ones, w_o, g, beta, tm=256):
    n = o.shape[0]
    tm = min(tm, n)
    rows = pl.BlockSpec((tm, D_MODEL), lambda i: (i, 0))
    return pl.pallas_call(
        _rwkv_out_kernel,
        grid=(n // tm,),
        in_specs=[rows] * 6 + [_full(vecs.shape), _full(head_ones.shape), _full(w_o.shape),
                               _full(g.shape), _full(beta.shape)],
        out_specs=rows,
        out_shape=jax.ShapeDtypeStruct((n, D_MODEL), F32),
        compiler_params=_cparams("parallel"),
        name="rwkv_out",
    )(o, r, k, v, gate, x2, vecs, head_ones, w_o, g, beta)


def _trunk(x, win_k, win_v, shift, wkv, P, bias):
    prompt = win_k is None
    B, T, D = x.shape
    n = B * T
    nkv = N_KV_HEADS * HEAD_DIM
    v_first = None
    nk, nv, ns, nw = [], [], [], []
    for i in range(DEPTH):
        j = i // 2
        ln_g, ln_b = P['ln_g'][i], P['ln_b'][i]
        x2 = x.reshape(n, D)
        if i % 2 == 0:
            q, k, v = _qkv_proj(x2, P['attn_w_qkv'][j], P['attn_b_qkv'][j:j + 1],
                                BF if prompt else F32)
            q, k, v = q.reshape(B, T, D), k.reshape(B, T, nkv), v.reshape(B, T, nkv)
            if prompt:
                o = _swa_prompt(q, k, v, bias, P['attn_sinks'][j])
                k_buf, v_buf = k[:, T - WINDOW:], v[:, T - WINDOW:]
            else:
                L = win_k.shape[2]
                o, k_buf, v_buf = _swa_sample(q, k, v, win_k[j].reshape(B, L, nkv),
                                              win_v[j].reshape(B, L, nkv), bias, P['attn_sinks'][j])
            nk.append(k_buf.reshape(B, -1, N_KV_HEADS, HEAD_DIM))
            nv.append(v_buf.reshape(B, -1, N_KV_HEADS, HEAD_DIM))
            x2 = _proj_ln(o.reshape(n, D), P['attn_w_o'][j], P['attn_b_o'][j:j + 1], x2,
                          ln_g[0:1], ln_b[0:1])
        else:
            if prompt:
                x_prev = jnp.zeros((B, 1, D), x.dtype)
                s0 = None
            else:
                x_prev, s0 = shift[j].reshape(B, 1, D), wkv[j]
            if j == 0:
                vecs = jnp.stack([P['rwkv_w0'][j], P['rwkv_a0'][j], P['rwkv_a0'][j],
                                  P['rwkv_k_k'][j], P['rwkv_k_a'][j]])
                vres = ()
            else:
                vecs = jnp.stack([P['rwkv_w0'][j], P['rwkv_a0'][j], P['rwkv_v0'][j - 1],
                                  P['rwkv_k_k'][j], P['rwkv_k_a'][j]])
                vres = (P['rwkv_v1'][j - 1], P['rwkv_v2'][j - 1])
            mats = (P['rwkv_w_rkv'][j, 0], P['rwkv_w_rkv'][j, 1], P['rwkv_w_rkv'][j, 2],
                    P['rwkv_w1'][j], P['rwkv_w2'][j], P['rwkv_a1'][j], P['rwkv_a2'][j],
                    P['rwkv_g1'][j], P['rwkv_g2'][j]) + vres
            bb, tt = (1, 256) if prompt else (min(32, B), T)
            r, logd, k, v, kk, a, gate = _rwkv_proj(
                x, x_prev, v_first if j > 0 else None, P['rwkv_mix'][j], vecs, mats,
                P['head_ones'], bb, tt)
            if j == 0:
                v_first = v
            o, s_new = _wkv(r, logd, k, v, kk, a, s0, 64 if prompt else T)
            ns.append(x[:, -1])
            nw.append(s_new)
            ovec = jnp.stack([P['rwkv_r_k'][j].reshape(D), P['rwkv_ln_w'][j], P['rwkv_ln_b'][j]])
            x2 = _rwkv_out(o.reshape(n, D), r.reshape(n, D), k.reshape(n, D), v.reshape(n, D),
                           gate.reshape(n, D), x2, ovec, P['head_ones'], P['rwkv_w_o'][j],
                           ln_g[0:1], ln_b[0:1])
        x2 = _ffn(x2, P['ffn_w_gu'][i], P['ffn_w_down'][i], ln_g[1:2], ln_b[1:2])
        x = x2.reshape(B, T, D)
    return x, jnp.stack(nk), jnp.stack(nv), jnp.stack(ns), jnp.stack(nw)


def kernel(x_prompt, x_sample, cache_win_k, cache_win_v, state_shift, state_wkv, rel_bias, ln_g, ln_b, attn_w_qkv, attn_b_qkv, attn_w_o, attn_b_o, attn_sinks, rwkv_mix, rwkv_w_rkv, rwkv_w0, rwkv_w1, rwkv_w2, rwkv_a0, rwkv_a1, rwkv_a2, rwkv_v0, rwkv_v1, rwkv_v2, rwkv_g1, rwkv_g2, rwkv_k_k, rwkv_k_a, rwkv_r_k, rwkv_ln_w, rwkv_ln_b, rwkv_w_o, ffn_w_gu, ffn_w_down):
    bf = lambda w: w.astype(BF)
    lane_head = np.arange(D_MODEL) // RWKV_HEAD
    head_ones = jnp.asarray(lane_head[:, None] == lane_head[None, :], dtype=BF)
    P = dict(ln_g=ln_g, ln_b=ln_b,
             attn_w_qkv=bf(attn_w_qkv), attn_b_qkv=attn_b_qkv, attn_w_o=bf(attn_w_o),
             attn_b_o=attn_b_o, attn_sinks=attn_sinks,
             rwkv_mix=rwkv_mix, rwkv_w_rkv=bf(rwkv_w_rkv), rwkv_w0=rwkv_w0, rwkv_w1=bf(rwkv_w1),
             rwkv_w2=bf(rwkv_w2), rwkv_a0=rwkv_a0, rwkv_a1=bf(rwkv_a1), rwkv_a2=bf(rwkv_a2),
             rwkv_v0=rwkv_v0, rwkv_v1=bf(rwkv_v1), rwkv_v2=bf(rwkv_v2), rwkv_g1=bf(rwkv_g1),
             rwkv_g2=bf(rwkv_g2), rwkv_k_k=rwkv_k_k, rwkv_k_a=rwkv_k_a, rwkv_r_k=rwkv_r_k,
             rwkv_ln_w=rwkv_ln_w, rwkv_ln_b=rwkv_ln_b, rwkv_w_o=bf(rwkv_w_o),
             ffn_w_gu=bf(ffn_w_gu), ffn_w_down=bf(ffn_w_down), head_ones=head_ones)

    T = x_sample.shape[1]
    L = cache_win_k.shape[2]
    d_prompt = np.arange(WINDOW)[:, None] + WINDOW - np.arange(2 * WINDOW)[None, :]
    d_sample = np.arange(T)[:, None] + L - np.arange(L + T)[None, :]
    bias_prompt = _bias_table(rel_bias, d_prompt)
    bias_sample = _bias_table(rel_bias, d_sample)

    y_prompt, pk, pv, ps, pw = _trunk(x_prompt, None, None, None, None, P, bias_prompt)
    y_sample, sk, sv, ss, sw = _trunk(x_sample, cache_win_k, cache_win_v, state_shift, state_wkv,
                                      P, bias_sample)
    return (y_prompt, y_sample, pk, pv, ps, pw, sk, sv, ss, sw)
```

```python
import functools
import math

import numpy as np
import jax
import jax.numpy as jnp
from jax import lax
from jax.experimental import pallas as pl
from jax.experimental.pallas import tpu as pltpu

D_MODEL = 1024
DEPTH = 4
HEAD_DIM = 64
N_HEADS = 16
N_KV_HEADS = 4
GROUP = 4
WINDOW = 128
N_BUCKETS = 32
MAX_DISTANCE = 128
RWKV_HEAD = 64
RWKV_HEADS = 16
GN_EPS = 64e-5
D_FF = 2816
ALPHA = (2 * DEPTH) ** 0.25
LN_EPS = 1e-5
NEG = -1e30
MXU_WIDTH = 256
WKV_ROWS = 64

BF = jnp.bfloat16
F32 = jnp.float32
VMEM_LIMIT = 56 * 1024 * 1024

NT_DIMS = (((1,), (1,)), ((), ()))
TN_DIMS = (((0,), (0,)), ((), ()))


def _cparams(*sem):
    return pltpu.CompilerParams(dimension_semantics=sem, vmem_limit_bytes=VMEM_LIMIT)


def _dot(a, b):
    return jnp.dot(a, b, preferred_element_type=F32)


def _layer_norm(z, g, b):
    mu = jnp.mean(z, axis=-1, keepdims=True)
    zc = z - mu
    var = jnp.mean(zc * zc, axis=-1, keepdims=True)
    return zc * lax.rsqrt(var + LN_EPS) * g + b


def _sigmoid(z):
    return 1.0 / (1.0 + jnp.exp(-z))


def _split2(z):
    hi = z.astype(BF)
    lo = (z - hi.astype(F32)).astype(BF)
    return hi, lo


def _split3(z):
    hi = z.astype(BF)
    r1 = z - hi.astype(F32)
    mid = r1.astype(BF)
    lo = (r1 - mid.astype(F32)).astype(BF)
    return hi, mid, lo


def _full(shape):
    n = len(shape)
    return pl.BlockSpec(shape, lambda *_: (0,) * n)


def _bias_table_kernel(rb_ref, bucket_ref, valid_ref, o_ref):
    bucket = bucket_ref[...]
    valid = valid_ref[...] > 0
    for h in range(N_HEADS):
        acc = jnp.zeros(bucket.shape, F32)
        for b in range(N_BUCKETS):
            acc = jnp.where(bucket == b, rb_ref[b, h], acc)
        o_ref[h] = jnp.where(valid, acc, NEG)


def _bias_table(rel_bias, d):
    dc = np.maximum(d, 0)
    exact = N_BUCKETS // 2
    df = np.maximum(dc, 1).astype(np.float32)
    large = exact + (np.log(df / np.float32(exact)) / np.float32(math.log(MAX_DISTANCE / exact))
                     * np.float32(N_BUCKETS - exact)).astype(np.int32)
    bucket = np.where(dc < exact, dc, np.minimum(large, N_BUCKETS - 1)).astype(np.int32)
    valid = ((d >= 0) & (d < WINDOW)).astype(np.int32)
    return pl.pallas_call(
        _bias_table_kernel,
        out_shape=jax.ShapeDtypeStruct((N_HEADS,) + d.shape, F32),
        in_specs=[pl.BlockSpec(memory_space=pltpu.SMEM),
                  pl.BlockSpec(memory_space=pltpu.VMEM),
                  pl.BlockSpec(memory_space=pltpu.VMEM)],
        out_specs=pl.BlockSpec(memory_space=pltpu.VMEM),
        name="bias_table",
    )(rel_bias, jnp.asarray(bucket), jnp.asarray(valid))


def _qkv_kernel(x_ref, w_ref, b_ref, q_ref, k_ref, v_ref):
    acc = _dot(x_ref[...].astype(BF), w_ref[...]) + b_ref[...]
    nq = N_HEADS * HEAD_DIM
    nk = N_KV_HEADS * HEAD_DIM
    q_ref[...] = (acc[:, :nq] * HEAD_DIM ** -0.5).astype(q_ref.dtype)
    k_ref[...] = acc[:, nq:nq + nk]
    v_ref[...] = acc[:, nq + nk:]


def _qkv_proj(x2, w, b, q_dtype, tm=512):
    n = x2.shape[0]
    tm = min(tm, n)
    nq = N_HEADS * HEAD_DIM
    nk = N_KV_HEADS * HEAD_DIM
    return pl.pallas_call(
        _qkv_kernel,
        grid=(n // tm,),
        in_specs=[pl.BlockSpec((tm, D_MODEL), lambda i: (i, 0)),
                  _full(w.shape), _full(b.shape)],
        out_specs=[pl.BlockSpec((tm, nq), lambda i: (i, 0)),
                   pl.BlockSpec((tm, nk), lambda i: (i, 0)),
                   pl.BlockSpec((tm, nk), lambda i: (i, 0))],
        out_shape=[jax.ShapeDtypeStruct((n, nq), q_dtype),
                   jax.ShapeDtypeStruct((n, nk), F32),
                   jax.ShapeDtypeStruct((n, nk), F32)],
        compiler_params=_cparams("parallel"),
        name="qkv_proj",
    )(x2, w, b)


def _swa_prompt_kernel(sink_ref, q_ref, kc_ref, kp_ref, vc_ref, vp_ref, bias_ref, o_ref):
    blk = pl.program_id(1)
    kcat = jnp.concatenate([kp_ref[0], kc_ref[0]], axis=0).astype(BF)
    vcat = jnp.concatenate([vp_ref[0], vc_ref[0]], axis=0).astype(BF)
    col = lax.broadcasted_iota(jnp.int32, (WINDOW, 2 * WINDOW), 1)
    no_prev = jnp.logical_and(blk == 0, col < WINDOW)
    outs = []
    for h in range(N_HEADS):
        g = h // GROUP
        qh = q_ref[0, :, h * HEAD_DIM:(h + 1) * HEAD_DIM]
        kh = kcat[:, g * HEAD_DIM:(g + 1) * HEAD_DIM]
        vh = vcat[:, g * HEAD_DIM:(g + 1) * HEAD_DIM]
        s = lax.dot_general(qh, kh, NT_DIMS, preferred_element_type=F32) + bias_ref[h]
        s = jnp.where(no_prev, NEG, s)
        sink = sink_ref[h]
        m = jnp.maximum(jnp.max(s, axis=-1, keepdims=True), sink)
        p = jnp.exp(s - m)
        den = jnp.sum(p, axis=-1, keepdims=True) + jnp.exp(sink - m)
        outs.append(_dot(p.astype(BF), vh) / den)
    o_ref[0] = jnp.concatenate(outs, axis=-1).astype(o_ref.dtype)


def _swa_prompt(q, k, v, bias, sinks):
    B, T, _ = q.shape
    nb = T // WINDOW
    nk = N_KV_HEADS * HEAD_DIM
    cur = lambda b, i: (b, i, 0)
    prev = lambda b, i: (b, jnp.maximum(i - 1, 0), 0)
    return pl.pallas_call(
        _swa_prompt_kernel,
        grid=(B, nb),
        in_specs=[pl.BlockSpec(memory_space=pltpu.SMEM),
                  pl.BlockSpec((1, WINDOW, D_MODEL), cur),
                  pl.BlockSpec((1, WINDOW, nk), cur),
                  pl.BlockSpec((1, WINDOW, nk), prev),
                  pl.BlockSpec((1, WINDOW, nk), cur),
                  pl.BlockSpec((1, WINDOW, nk), prev),
                  _full(bias.shape)],
        out_specs=pl.BlockSpec((1, WINDOW, D_MODEL), cur),
        out_shape=jax.ShapeDtypeStruct((B, T, D_MODEL), BF),
        compiler_params=_cparams("parallel", "parallel"),
        name="swa_prompt",
    )(sinks, q, k, k, v, v, bias)


def _qkv_t_kernel(x_ref, wq_ref, wkvt_ref, bq_ref, bkvt_ref, q_ref, kvt_ref):
    xb = x_ref[...].astype(BF)
    q_ref[...] = (_dot(xb, wq_ref[...]) + bq_ref[...]) * HEAD_DIM ** -0.5
    kvt_ref[...] = lax.dot_general(wkvt_ref[...], xb, NT_DIMS, preferred_element_type=F32) + bkvt_ref[...]


def _qkv_proj_t(x2, w, b, tm=256):
    n = x2.shape[0]
    tm = min(tm, n)
    nq = N_HEADS * HEAD_DIM
    nkv = 2 * N_KV_HEADS * HEAD_DIM
    wq, wkvt = w[:, :nq], w[:, nq:].T
    bq, bkvt = b[:, :nq], b[:, nq:].T
    return pl.pallas_call(
        _qkv_t_kernel,
        grid=(n // tm,),
        in_specs=[pl.BlockSpec((tm, D_MODEL), lambda i: (i, 0)),
                  _full(wq.shape), _full(wkvt.shape), _full(bq.shape), _full(bkvt.shape)],
        out_specs=[pl.BlockSpec((tm, nq), lambda i: (i, 0)),
                   pl.BlockSpec((nkv, tm), lambda i: (0, i))],
        out_shape=[jax.ShapeDtypeStruct((n, nq), F32),
                   jax.ShapeDtypeStruct((nkv, n), F32)],
        compiler_params=_cparams("parallel"),
        name="qkv_proj_t",
    )(x2, wq, wkvt, bq, bkvt)


def _swa_sample_kernel(T, sink_ref, q_ref, kvt_ref, ck_ref, cv_ref, bc_ref, bn_ref,
                       o_ref, nk_ref, nv_ref):
    bb = q_ref.shape[0]
    L = ck_ref.shape[-1]
    HD = HEAD_DIM
    units = [(b, g) for b in range(bb) for g in range(N_KV_HEADS)]
    lane = lax.broadcasted_iota(jnp.int32, (HD, L), 1)
    knew = [kvt_ref[g * HD:(g + 1) * HD, :] for g in range(N_KV_HEADS)]
    vnew = [kvt_ref[(N_KV_HEADS + g) * HD:(N_KV_HEADS + g + 1) * HD, :] for g in range(N_KV_HEADS)]
    sink = [jnp.concatenate([jnp.full((T, 1), sink_ref[g * GROUP + j], F32) for j in range(GROUP)], axis=0)
            for g in range(N_KV_HEADS)]

    def shift_in(old, new, b):
        moved = pltpu.roll(new, (L - T - b * T) % L, 1)
        return jnp.where(lane >= L - T, moved, pltpu.roll(old, L - T, 1))

    qs, sc, sn = {}, {}, {}
    for b in range(bb):
        q = q_ref[b]
        for g in range(N_KV_HEADS):
            qg = jnp.concatenate([q[:, (g * GROUP + j) * HD:(g * GROUP + j + 1) * HD]
                                  for j in range(GROUP)], axis=0).astype(BF)
            kc = ck_ref[0, b, g]
            nk_ref[b, g] = shift_in(kc, knew[g], b)
            sc[b, g] = _dot(qg, kc.astype(BF)) + bc_ref[g]
            sn[b, g] = _dot(qg, knew[g][:, b * T:(b + 1) * T].astype(BF)) + bn_ref[g]
    sc_all = jnp.concatenate([sc[u] for u in units], axis=0)
    sn_all = jnp.concatenate([sn[u] for u in units], axis=0)
    sink_all = jnp.concatenate([sink[g] for _, g in units], axis=0)
    m = jnp.maximum(jnp.maximum(jnp.max(sc_all, axis=-1, keepdims=True),
                                jnp.max(sn_all, axis=-1, keepdims=True)), sink_all)
    pc_all = jnp.exp(sc_all - m)
    pn_all = jnp.exp(sn_all - m)
    inv_all = 1.0 / (jnp.sum(pc_all, axis=-1, keepdims=True) + jnp.sum(pn_all, axis=-1, keepdims=True)
                     + jnp.exp(sink_all - m))
    pc_all = pc_all.astype(BF)
    pn_all = pn_all.astype(BF)
    gt = GROUP * T
    pc = {u: pc_all[i * gt:(i + 1) * gt] for i, u in enumerate(units)}
    pn = {u: pn_all[i * gt:(i + 1) * gt] for i, u in enumerate(units)}
    inv = {u: inv_all[i * gt:(i + 1) * gt] for i, u in enumerate(units)}
    for b in range(bb):
        pieces = []
        for g in range(N_KV_HEADS):
            vc = cv_ref[0, b, g]
            nv_ref[b, g] = shift_in(vc, vnew[g], b)
            og = (lax.dot_general(pc[b, g], vc.astype(BF), NT_DIMS, preferred_element_type=F32)
                  + lax.dot_general(pn[b, g], vnew[g][:, b * T:(b + 1) * T].astype(BF),
                                    NT_DIMS, preferred_element_type=F32)) * inv[b, g]
            pieces.extend(og[j * T:(j + 1) * T] for j in range(GROUP))
        o_ref[b] = jnp.concatenate(pieces, axis=-1).astype(o_ref.dtype)


def _swa_sample(q, kvt, ck, cv, layer, bias, sinks):
    B, T, _ = q.shape
    L = ck.shape[-1]
    bb = L // T
    assert B % bb == 0 and L % T == 0
    bias = bias.reshape(N_KV_HEADS, GROUP * T, L + T)
    bc, bn = bias[:, :, :L], bias[:, :, L:]
    qblk = pl.BlockSpec((bb, T, D_MODEL), lambda i: (i, 0, 0))
    cblk = pl.BlockSpec((1, bb, N_KV_HEADS, HEAD_DIM, L), lambda i: (layer, i, 0, 0, 0))
    nblk = pl.BlockSpec((bb, N_KV_HEADS, HEAD_DIM, L), lambda i: (i, 0, 0, 0))
    return pl.pallas_call(
        functools.partial(_swa_sample_kernel, T),
        grid=(B // bb,),
        in_specs=[pl.BlockSpec(memory_space=pltpu.SMEM), qblk,
                  pl.BlockSpec((kvt.shape[0], bb * T), lambda i: (0, i)),
                  cblk, cblk, _full(bc.shape), _full(bn.shape)],
        out_specs=[qblk, nblk, nblk],
        out_shape=[jax.ShapeDtypeStruct((B, T, D_MODEL), BF),
                   jax.ShapeDtypeStruct((B, N_KV_HEADS, HEAD_DIM, L), F32),
                   jax.ShapeDtypeStruct((B, N_KV_HEADS, HEAD_DIM, L), F32)],
        compiler_params=_cparams("parallel"),
        name="swa_sample",
    )(sinks, q, kvt, ck, cv, bc, bn)


def _proj_ln_kernel(a_ref, w_ref, b_ref, x_ref, g_ref, beta_ref, o_ref):
    y = _dot(a_ref[...].astype(BF), w_ref[...]) + b_ref[...]
    o_ref[...] = _layer_norm(ALPHA * x_ref[...] + y, g_ref[...], beta_ref[...])


def _proj_ln(a, w, b, x2, g, beta, tm=512):
    n, kdim = a.shape
    tm = min(tm, n)
    rows = lambda i: (i, 0)
    return pl.pallas_call(
        _proj_ln_kernel,
        grid=(n // tm,),
        in_specs=[pl.BlockSpec((tm, kdim), rows), _full(w.shape), _full(b.shape),
                  pl.BlockSpec((tm, D_MODEL), rows), _full(g.shape), _full(beta.shape)],
        out_specs=pl.BlockSpec((tm, D_MODEL), rows),
        out_shape=jax.ShapeDtypeStruct((n, D_MODEL), F32),
        compiler_params=_cparams("parallel"),
        name="proj_ln",
    )(a, w, b, x2, g, beta)


def _ffn_kernel(x_ref, wg_ref, wu_ref, wd_ref, g_ref, beta_ref, o_ref, acc_ref, xb_ref):
    j = pl.program_id(1)

    @pl.when(j == 0)
    def _():
        acc_ref[...] = jnp.zeros_like(acc_ref)
        xb_ref[...] = x_ref[...].astype(BF)

    xb = xb_ref[...]
    gt = _dot(xb, wg_ref[...])
    ut = _dot(xb, wu_ref[...])
    h = (gt * _sigmoid(gt) * ut).astype(BF)
    acc_ref[...] += _dot(h, wd_ref[...])

    @pl.when(j == pl.num_programs(1) - 1)
    def _():
        o_ref[...] = _layer_norm(ALPHA * x_ref[...] + acc_ref[...], g_ref[...], beta_ref[...])


def _ffn(x2, w_gu, w_down, g, beta, tm=512, tf=1408):
    n = x2.shape[0]
    tm = min(tm, n)
    nf = D_FF // tf
    rows = lambda i, j: (i, 0)
    return pl.pallas_call(
        _ffn_kernel,
        grid=(n // tm, nf),
        in_specs=[pl.BlockSpec((tm, D_MODEL), rows),
                  pl.BlockSpec((D_MODEL, tf), lambda i, j: (0, j)),
                  pl.BlockSpec((D_MODEL, tf), lambda i, j: (0, j + nf)),
                  pl.BlockSpec((tf, D_MODEL), lambda i, j: (j, 0)),
                  pl.BlockSpec((1, D_MODEL), lambda i, j: (0, 0)),
                  pl.BlockSpec((1, D_MODEL), lambda i, j: (0, 0))],
        out_specs=pl.BlockSpec((tm, D_MODEL), rows),
        out_shape=jax.ShapeDtypeStruct((n, D_MODEL), F32),
        scratch_shapes=[pltpu.VMEM((tm, D_MODEL), F32), pltpu.VMEM((tm, D_MODEL), BF)],
        compiler_params=_cparams("parallel", "arbitrary"),
        name="ffn",
    )(x2, w_gu, w_gu, w_down, g, beta)


def _head_sum(z, m_ref):
    hi, lo = _split2(z)
    m = m_ref[...]
    w = m.shape[0]
    return jnp.concatenate([_dot(hi[:, c:c + w], m) + _dot(lo[:, c:c + w], m)
                            for c in range(0, z.shape[1], w)], axis=1)


def _rwkv_proj_kernel(has_vres, *refs):
    if has_vres:
        (x_ref, xp_ref, mix_ref, vec_ref, wr_ref, wk_ref, wv_ref, w1_ref, w2_ref, a1_ref, a2_ref,
         g1_ref, g2_ref, m_ref, v1_ref, v2_ref, vf_ref,
         r_ref, w_ref, k_ref, v_ref, kk_ref, a_ref, g_ref, carry_ref) = refs
    else:
        (x_ref, xp_ref, mix_ref, vec_ref, wr_ref, wk_ref, wv_ref, w1_ref, w2_ref, a1_ref, a2_ref,
         g1_ref, g2_ref, m_ref,
         r_ref, w_ref, k_ref, v_ref, kk_ref, a_ref, g_ref, carry_ref) = refs
    bb, tt, D = x_ref.shape
    n = bb * tt
    t = pl.program_id(1)
    x3 = x_ref[...]
    prev = jnp.where(t == 0, xp_ref[...], carry_ref[...])
    carry_ref[...] = x3[:, tt - 1:tt, :]
    x = x3.reshape(n, D)
    prev_rows = jnp.broadcast_to(prev, (bb, tt, D)).reshape(n, D)
    row = lax.broadcasted_iota(jnp.int32, (n, D), 0)
    xs = jnp.where(row % tt == 0, prev_rows, pltpu.roll(x, 1, 0))
    xx = xs - x
    xr, xw, xk, xv, xa, xg = ((x + xx * mix_ref[c:c + 1, :]).astype(BF) for c in range(6))
    w0, a0, v0, k_k, k_a = (vec_ref[c:c + 1, :] for c in range(5))

    r = _dot(xr, wr_ref[...])
    k = _dot(xk, wk_ref[...])
    v = _dot(xv, wv_ref[...])
    z = w0 + _dot(jnp.tanh(_dot(xw, w1_ref[...])).astype(BF), w2_ref[...])
    w = -(jnp.maximum(-z, 0.0) + jnp.log1p(jnp.exp(-jnp.abs(z)))) - 0.5
    logd = -jnp.exp(w)
    if has_vres:
        gate = _sigmoid(v0 + _dot(_dot(xv, v1_ref[...]).astype(BF), v2_ref[...]))
        v = v + (vf_ref[...].reshape(n, D) - v) * gate
    a = _sigmoid(a0 + _dot(_dot(xa, a1_ref[...]).astype(BF), a2_ref[...]))
    g = _dot(_sigmoid(_dot(xg, g1_ref[...])).astype(BF), g2_ref[...])
    kk = k * k_k
    kk = kk / jnp.maximum(jnp.sqrt(_head_sum(kk * kk, m_ref)), 1e-12)
    k = k * (1.0 + (a - 1.0) * k_a)
    for ref, val in ((r_ref, r), (w_ref, logd), (k_ref, k), (v_ref, v), (kk_ref, kk), (a_ref, a),
                     (g_ref, g)):
        ref[...] = val.reshape(bb, tt, D)


def _rwkv_proj(x, x_prev, v_first, mix, vecs, mats, head_ones, bb, tt):
    B, T, D = x.shape
    has_vres = v_first is not None
    blk = pl.BlockSpec((bb, tt, D), lambda b, t: (b, t, 0))
    ins = [x, x_prev, mix, vecs] + list(mats[:9]) + [head_ones]
    in_specs = [blk, pl.BlockSpec((bb, 1, D), lambda b, t: (b, 0, 0))]
    in_specs += [_full(a.shape) for a in ins[2:]]
    if has_vres:
        ins += [mats[9], mats[10], v_first]
        in_specs += [_full(mats[9].shape), _full(mats[10].shape), blk]
    return pl.pallas_call(
        functools.partial(_rwkv_proj_kernel, has_vres),
        grid=(B // bb, T // tt),
        in_specs=in_specs,
        out_specs=[blk] * 7,
        out_shape=[jax.ShapeDtypeStruct((B, T, D), F32)] * 7,
        scratch_shapes=[pltpu.VMEM((bb, 1, D), F32)],
        compiler_params=_cparams("parallel", "arbitrary"),
        name="rwkv_proj",
    )(*ins)


def _wkv_kernel(has_s0, C, *refs):
    if has_s0:
        r_ref, w_ref, k_ref, v_ref, kk_ref, a_ref, s0_ref, o_ref, so_ref, s_scr = refs
    else:
        r_ref, w_ref, k_ref, v_ref, kk_ref, a_ref, o_ref, so_ref, s_scr = refs
    R = WKV_ROWS
    N = RWKV_HEAD
    NP = RWKV_HEADS // 2
    nseq = R // C
    t = pl.program_id(1)

    @pl.when(t == 0)
    def _():
        if has_s0:
            for s in range(nseq):
                for p in range(NP):
                    s_scr[s, p] = jnp.concatenate([s0_ref[s, 2 * p], s0_ref[s, 2 * p + 1]], axis=1)
        else:
            s_scr[...] = jnp.zeros_like(s_scr)

    logd = w_ref[0]
    ri = lax.broadcasted_iota(jnp.int32, (R, R), 0)
    ci = lax.broadcasted_iota(jnp.int32, (R, R), 1)
    tril = jnp.logical_and(ri // C == ci // C, ci <= ri).astype(BF)
    cum = sum(_dot(tril, part) for part in _split3(logd))
    g_in = jnp.exp(cum)
    g_ex = jnp.exp(cum - logd)
    g_inv = jnp.exp(-cum)
    kk = kk_ref[0]
    at = -(kk * g_ex)
    rt = r_ref[0] * g_in
    bt = kk * a_ref[0] * g_inv
    kt = k_ref[0] * g_inv
    v = v_ref[0]

    def lane_masks(width):
        rr = lax.broadcasted_iota(jnp.int32, (R, width), 0)
        cc = lax.broadcasted_iota(jnp.int32, (R, width), 1) % N
        same = rr // C == cc // C
        return jnp.logical_and(same, cc < rr), jnp.logical_and(same, cc <= rr)

    strict2, _ = lane_masks(2 * N)
    _, incl4 = lane_masks(4 * N)

    def bdiag(m):
        lo = lax.broadcasted_iota(jnp.int32, m.shape, 1) < N
        zero = jnp.zeros_like(m)
        return jnp.concatenate([jnp.where(lo, m, zero), jnp.where(lo, zero, m)], axis=0)

    pairs = range(NP)
    seqs = range(nseq)
    P = [slice(2 * N * p, 2 * N * (p + 1)) for p in pairs]
    rows = [slice(C * s, C * (s + 1)) for s in seqs]
    vp = [v[:, P[p]] for p in pairs]
    vpb = [x.astype(BF) for x in vp]
    btp = [bt[:, P[p]] for p in pairs]
    ktp = [kt[:, P[p]] for p in pairs]
    xa = [at[:, P[p]] for p in pairs]
    xr = [rt[:, P[p]] for p in pairs]
    S0 = [[s_scr[s, p] for p in pairs] for s in seqs]

    xp = [jnp.concatenate([xa[p], xr[p]], axis=0).astype(BF) for p in pairs]
    bmat = [jnp.concatenate([bdiag(btp[p].astype(BF)), bdiag(ktp[p].astype(BF))], axis=0)
            for p in pairs]
    G = [lax.dot_general(xp[p], bmat[p], NT_DIMS, preferred_element_type=F32) for p in pairs]
    Z = [[lax.dot_general(
        xp[p] if nseq == 1 else
        jnp.concatenate([xa[p][rows[s]], xr[p][rows[s]]], axis=0).astype(BF),
        bdiag(S0[s][p].astype(BF)), NT_DIMS, preferred_element_type=F32) for p in pairs] for s in seqs]
    za = [Z[0][p][:C] if nseq == 1 else jnp.concatenate([Z[s][p][:C] for s in seqs], axis=0)
          for p in pairs]
    zr = [Z[0][p][C:] if nseq == 1 else jnp.concatenate([Z[s][p][C:] for s in seqs], axis=0)
          for p in pairs]
    A = [jnp.where(strict2, G[p][:R, :2 * N], 0.0) for p in pairs]
    aak = [jnp.where(strict2, G[p][:R, 2 * N:], 0.0).astype(BF) for p in pairs]
    T2 = [jnp.where(incl4, G[p][R:], 0.0).astype(BF) for p in pairs]
    W = [za[p] + _dot(aak[p], bdiag(vpb[p])) for p in pairs]
    nsteps = max(1, int(math.log2(C)))
    for step in range(nsteps):
        Ab = [A[p].astype(BF) for p in pairs]
        if step + 1 < nsteps:
            res = [_dot(Ab[p], jnp.concatenate([bdiag(W[p].astype(BF)), bdiag(Ab[p])], axis=1))
                   for p in pairs]
            W = [W[p] + res[p][:, :2 * N] for p in pairs]
            A = [res[p][:, 2 * N:] for p in pairs]
        else:
            W = [W[p] + _dot(Ab[p], bdiag(W[p].astype(BF))) for p in pairs]
    wb = [W[p].astype(BF) for p in pairs]
    for p in pairs:
        uv = jnp.concatenate([bdiag(wb[p]), bdiag(vpb[p])], axis=0)
        o_ref[0, :, P[p]] = zr[p] + _dot(T2[p], uv)
    for s in seqs:
        for p in pairs:
            uvs = jnp.concatenate([W[p][rows[s]], vp[p][rows[s]]], axis=0).astype(BF)
            ys = jnp.concatenate([btp[p][rows[s]], ktp[p][rows[s]]], axis=0).astype(BF)
            full = lax.dot_general(uvs, ys, TN_DIMS, preferred_element_type=F32)
            lo = lax.broadcasted_iota(jnp.int32, (N, 2 * N), 1) < N
            upd = jnp.where(lo, full[:N], full[N:])
            g_end = g_in[C * (s + 1) - 1:C * (s + 1), P[p]]
            s_scr[s, p] = (S0[s][p] + upd) * g_end

    @pl.when(t == pl.num_programs(1) - 1)
    def _():
        for s in seqs:
            for p in pairs:
                so_ref[s, 2 * p] = s_scr[s, p][:, :N]
                so_ref[s, 2 * p + 1] = s_scr[s, p][:, N:]


def _wkv(r, logd, k, v, kk, a, s0, C):
    B, T, D = r.shape
    has_s0 = s0 is not None
    R = WKV_ROWS
    nseq = R // C
    nt = T // C
    assert (nseq == 1 or nt == 1) and B % nseq == 0 and T % C == 0
    nb = B // nseq
    blk = pl.BlockSpec((1, R, D), lambda b, t: (b * nt + t, 0, 0))
    sblk = pl.BlockSpec((nseq, RWKV_HEADS, RWKV_HEAD, RWKV_HEAD), lambda b, t: (b, 0, 0, 0))
    ins = [x.reshape(B * T // R, R, D) for x in (r, logd, k, v, kk, a)] + ([s0] if has_s0 else [])
    o, s_new = pl.pallas_call(
        functools.partial(_wkv_kernel, has_s0, C),
        grid=(nb, nt),
        in_specs=[blk] * 6 + ([sblk] if has_s0 else []),
        out_specs=[blk, sblk],
        out_shape=[jax.ShapeDtypeStruct((B * T // R, R, D), F32),
                   jax.ShapeDtypeStruct((B, RWKV_HEADS, RWKV_HEAD, RWKV_HEAD), F32)],
        scratch_shapes=[pltpu.VMEM((nseq, RWKV_HEADS // 2, RWKV_HEAD, 2 * RWKV_HEAD), F32)],
        compiler_params=_cparams("parallel", "arbitrary"),
        name="wkv",
    )(*ins)
    return o.reshape(B, T, D), s_new


def _rwkv_out_kernel(o_ref, r_ref, k_ref, v_ref, gate_ref, x_ref, vec_ref, m_ref, wo_ref,
                     g_ref, beta_ref, y_ref):
    r_k, ln_w, ln_b = (vec_ref[c:c + 1, :] for c in range(3))
    o = o_ref[...]
    inv_n = 1.0 / RWKV_HEAD
    mu = _head_sum(o, m_ref) * inv_n
    oc = o - mu
    var = _head_sum(oc * oc, m_ref) * inv_n
    on = oc * lax.rsqrt(var + GN_EPS) * ln_w + ln_b
    v = v_ref[...]
    bonus = _head_sum(r_ref[...] * k_ref[...] * r_k, m_ref) * v
    y = _dot(((on + bonus) * gate_ref[...]).astype(BF), wo_ref[...])
    y_ref[...] = _layer_norm(ALPHA * x_ref[...] + y, g_ref[...], beta_ref[...])


def _rwkv_out(o, r, k, v, gate, x2, vecs, head_ones, w_o, g, beta, tm=256):
    n = o.shape[0]
    tm = min(tm, n)
    rows = pl.BlockSpec((tm, D_MODEL), lambda i: (i, 0))
    return pl.pallas_call(
        _rwkv_out_kernel,
        grid=(n // tm,),
        in_specs=[rows] * 6 + [_full(vecs.shape), _full(head_ones.shape), _full(w_o.shape),
                               _full(g.shape), _full(beta.shape)],
        out_specs=rows,
        out_shape=jax.ShapeDtypeStruct((n, D_MODEL), F32),
        compiler_params=_cparams("parallel"),
        name="rwkv_out",
    )(o, r, k, v, gate, x2, vecs, head_ones, w_o, g, beta)


def _trunk(x, win_k, win_v, shift, wkv, P, bias):
    prompt = win_k is None
    B, T, D = x.shape
    n = B * T
    nkv = N_KV_HEADS * HEAD_DIM
    v_first = None
    nk, nv, ns, nw = [], [], [], []
    for i in range(DEPTH):
        j = i // 2
        ln_g, ln_b = P['ln_g'][i], P['ln_b'][i]
        x2 = x.reshape(n, D)
        if i % 2 == 0:
            if prompt:
                q, k, v = _qkv_proj(x2, P['attn_w_qkv'][j], P['attn_b_qkv'][j:j + 1], BF)
                q, k, v = q.reshape(B, T, D), k.reshape(B, T, nkv), v.reshape(B, T, nkv)
                o = _swa_prompt(q, k, v, bias, P['attn_sinks'][j])
                k_buf = k[:, T - WINDOW:].reshape(B, WINDOW, N_KV_HEADS, HEAD_DIM)
                v_buf = v[:, T - WINDOW:].reshape(B, WINDOW, N_KV_HEADS, HEAD_DIM)
            else:
                q, kvt = _qkv_proj_t(x2, P['attn_w_qkv'][j], P['attn_b_qkv'][j:j + 1])
                o, k_buf, v_buf = _swa_sample(q.reshape(B, T, D), kvt, win_k, win_v, j, bias,
                                              P['attn_sinks'][j])
                k_buf = jnp.transpose(k_buf, (0, 3, 1, 2))
                v_buf = jnp.transpose(v_buf, (0, 3, 1, 2))
            nk.append(k_buf)
            nv.append(v_buf)
            x2 = _proj_ln(o.reshape(n, D), P['attn_w_o'][j], P['attn_b_o'][j:j + 1], x2,
                          ln_g[0:1], ln_b[0:1])
        else:
            if prompt:
                x_prev = jnp.zeros((B, 1, D), x.dtype)
                s0 = None
            else:
                x_prev, s0 = shift[j].reshape(B, 1, D), wkv[j]
            if j == 0:
                vecs = jnp.stack([P['rwkv_w0'][j], P['rwkv_a0'][j], P['rwkv_a0'][j],
                                  P['rwkv_k_k'][j], P['rwkv_k_a'][j]])
                vres = ()
            else:
                vecs = jnp.stack([P['rwkv_w0'][j], P['rwkv_a0'][j], P['rwkv_v0'][j - 1],
                                  P['rwkv_k_k'][j], P['rwkv_k_a'][j]])
                vres = (P['rwkv_v1'][j - 1], P['rwkv_v2'][j - 1])
            mats = (P['rwkv_w_rkv'][j, 0], P['rwkv_w_rkv'][j, 1], P['rwkv_w_rkv'][j, 2],
                    P['rwkv_w1'][j], P['rwkv_w2'][j], P['rwkv_a1'][j], P['rwkv_a2'][j],
                    P['rwkv_g1'][j], P['rwkv_g2'][j]) + vres
            bb, tt = (1, 256) if prompt else (min(32, B), T)
            r, logd, k, v, kk, a, gate = _rwkv_proj(
                x, x_prev, v_first if j > 0 else None, P['rwkv_mix'][j], vecs, mats,
                P['head_ones'], bb, tt)
            if j == 0:
                v_first = v
            o, s_new = _wkv(r, logd, k, v, kk, a, s0, 64 if prompt else T)
            ns.append(x[:, -1])
            nw.append(s_new)
            ovec = jnp.stack([P['rwkv_r_k'][j].reshape(D), P['rwkv_ln_w'][j], P['rwkv_ln_b'][j]])
            x2 = _rwkv_out(o.reshape(n, D), r.reshape(n, D), k.reshape(n, D), v.reshape(n, D),
                           gate.reshape(n, D), x2, ovec, P['head_ones'], P['rwkv_w_o'][j],
                           ln_g[0:1], ln_b[0:1])
        x2 = _ffn(x2, P['ffn_w_gu'][i], P['ffn_w_down'][i], ln_g[1:2], ln_b[1:2])
        x = x2.reshape(B, T, D)
    return x, jnp.stack(nk), jnp.stack(nv), jnp.stack(ns), jnp.stack(nw)


def kernel(x_prompt, x_sample, cache_win_k, cache_win_v, state_shift, state_wkv, rel_bias, ln_g, ln_b, attn_w_qkv, attn_b_qkv, attn_w_o, attn_b_o, attn_sinks, rwkv_mix, rwkv_w_rkv, rwkv_w0, rwkv_w1, rwkv_w2, rwkv_a0, rwkv_a1, rwkv_a2, rwkv_v0, rwkv_v1, rwkv_v2, rwkv_g1, rwkv_g2, rwkv_k_k, rwkv_k_a, rwkv_r_k, rwkv_ln_w, rwkv_ln_b, rwkv_w_o, ffn_w_gu, ffn_w_down):
    bf = lambda w: w.astype(BF)
    lane_head = np.arange(MXU_WIDTH) // RWKV_HEAD
    head_ones = jnp.asarray(lane_head[:, None] == lane_head[None, :], dtype=BF)
    P = dict(ln_g=ln_g, ln_b=ln_b,
             attn_w_qkv=bf(attn_w_qkv), attn_b_qkv=attn_b_qkv, attn_w_o=bf(attn_w_o),
             attn_b_o=attn_b_o, attn_sinks=attn_sinks,
             rwkv_mix=rwkv_mix, rwkv_w_rkv=bf(rwkv_w_rkv), rwkv_w0=rwkv_w0, rwkv_w1=bf(rwkv_w1),
             rwkv_w2=bf(rwkv_w2), rwkv_a0=rwkv_a0, rwkv_a1=bf(rwkv_a1), rwkv_a2=bf(rwkv_a2),
             rwkv_v0=rwkv_v0, rwkv_v1=bf(rwkv_v1), rwkv_v2=bf(rwkv_v2), rwkv_g1=bf(rwkv_g1),
             rwkv_g2=bf(rwkv_g2), rwkv_k_k=rwkv_k_k, rwkv_k_a=rwkv_k_a, rwkv_r_k=rwkv_r_k,
             rwkv_ln_w=rwkv_ln_w, rwkv_ln_b=rwkv_ln_b, rwkv_w_o=bf(rwkv_w_o),
             ffn_w_gu=bf(ffn_w_gu), ffn_w_down=bf(ffn_w_down), head_ones=head_ones)

    T = x_sample.shape[1]
    L = cache_win_k.shape[2]
    d_prompt = np.arange(WINDOW)[:, None] + WINDOW - np.arange(2 * WINDOW)[None, :]
    d_sample = np.arange(T)[:, None] + L - np.arange(L + T)[None, :]
    bias_prompt = _bias_table(rel_bias, d_prompt)
    bias_sample = _bias_table(rel_bias, d_sample)

    y_prompt, pk, pv, ps, pw = _trunk(x_prompt, None, None, None, None, P, bias_prompt)
    win_k = jnp.transpose(cache_win_k, (0, 1, 3, 4, 2))
    win_v = jnp.transpose(cache_win_v, (0, 1, 3, 4, 2))
    y_sample, sk, sv, ss, sw = _trunk(x_sample, win_k, win_v, state_shift, state_wkv,
                                      P, bias_sample)
    return (y_prompt, y_sample, pk, pv, ps, pw, sk, sv, ss, sw)
```

```python
import functools
import math

import numpy as np
import jax
import jax.numpy as jnp
from jax import lax
from jax.experimental import pallas as pl
from jax.experimental.pallas import tpu as pltpu

D_MODEL = 1024
DEPTH = 4
HEAD_DIM = 64
N_HEADS = 16
N_KV_HEADS = 4
GROUP = 4
WINDOW = 128
N_BUCKETS = 32
MAX_DISTANCE = 128
RWKV_HEAD = 64
RWKV_HEADS = 16
GN_EPS = 64e-5
D_FF = 2816
ALPHA = (2 * DEPTH) ** 0.25
LN_EPS = 1e-5
NEG = -1e30
MXU_WIDTH = 256
WKV_ROWS = 64

BF = jnp.bfloat16
F32 = jnp.float32
VMEM_LIMIT = 56 * 1024 * 1024

NT_DIMS = (((1,), (1,)), ((), ()))
TN_DIMS = (((0,), (0,)), ((), ()))

V_W0, V_A0, V_V0, V_KK, V_KA, V_RK, V_LNW, V_LNB = range(8)


def _cparams(*sem):
    return pltpu.CompilerParams(dimension_semantics=sem, vmem_limit_bytes=VMEM_LIMIT)


def _dot(a, b):
    return jnp.dot(a, b, preferred_element_type=F32)


def _layer_norm(z, g, b):
    mu = jnp.mean(z, axis=-1, keepdims=True)
    zc = z - mu
    var = jnp.mean(zc * zc, axis=-1, keepdims=True)
    return zc * lax.rsqrt(var + LN_EPS) * g + b


def _sigmoid(z):
    return 1.0 / (1.0 + jnp.exp(-z))


def _split2(z):
    hi = z.astype(BF)
    lo = (z - hi.astype(F32)).astype(BF)
    return hi, lo


def _split3(z):
    hi = z.astype(BF)
    r1 = z - hi.astype(F32)
    mid = r1.astype(BF)
    lo = (r1 - mid.astype(F32)).astype(BF)
    return hi, mid, lo


def _full(shape):
    n = len(shape)
    return pl.BlockSpec(shape, lambda *_: (0,) * n)


def _layer(shape, *lead):
    k = len(lead)
    rest = len(shape) - k
    return pl.BlockSpec((None,) * k + tuple(shape[k:]), lambda *_: tuple(lead) + (0,) * rest)


def _bias_table_kernel(rb_ref, bucket_ref, valid_ref, o_ref):
    bucket = bucket_ref[...]
    valid = valid_ref[...] > 0
    for h in range(N_HEADS):
        acc = jnp.zeros(bucket.shape, F32)
        for b in range(N_BUCKETS):
            acc = jnp.where(bucket == b, rb_ref[b, h], acc)
        o_ref[h] = jnp.where(valid, acc, NEG)


def _bias_table(rel_bias, d):
    dc = np.maximum(d, 0)
    exact = N_BUCKETS // 2
    df = np.maximum(dc, 1).astype(np.float32)
    large = exact + (np.log(df / np.float32(exact)) / np.float32(math.log(MAX_DISTANCE / exact))
                     * np.float32(N_BUCKETS - exact)).astype(np.int32)
    bucket = np.where(dc < exact, dc, np.minimum(large, N_BUCKETS - 1)).astype(np.int32)
    valid = ((d >= 0) & (d < WINDOW)).astype(np.int32)
    return pl.pallas_call(
        _bias_table_kernel,
        out_shape=jax.ShapeDtypeStruct((N_HEADS,) + d.shape, F32),
        in_specs=[pl.BlockSpec(memory_space=pltpu.SMEM),
                  pl.BlockSpec(memory_space=pltpu.VMEM),
                  pl.BlockSpec(memory_space=pltpu.VMEM)],
        out_specs=pl.BlockSpec(memory_space=pltpu.VMEM),
        name="bias_table",
    )(rel_bias, jnp.asarray(bucket), jnp.asarray(valid))


def _qkv_kernel(x_ref, w_ref, b_ref, q_ref, k_ref, v_ref):
    acc = _dot(x_ref[...].astype(BF), w_ref[...]) + b_ref[...]
    nq = N_HEADS * HEAD_DIM
    nk = N_KV_HEADS * HEAD_DIM
    q_ref[...] = (acc[:, :nq] * HEAD_DIM ** -0.5).astype(q_ref.dtype)
    k_ref[...] = acc[:, nq:nq + nk]
    v_ref[...] = acc[:, nq + nk:]


def _qkv_proj(x2, w, b, layer, tm=512):
    n = x2.shape[0]
    tm = min(tm, n)
    nq = N_HEADS * HEAD_DIM
    nk = N_KV_HEADS * HEAD_DIM
    return pl.pallas_call(
        _qkv_kernel,
        grid=(n // tm,),
        in_specs=[pl.BlockSpec((tm, D_MODEL), lambda i: (i, 0)),
                  _layer(w.shape, layer), _layer(b.shape, layer)],
        out_specs=[pl.BlockSpec((tm, nq), lambda i: (i, 0)),
                   pl.BlockSpec((tm, nk), lambda i: (i, 0)),
                   pl.BlockSpec((tm, nk), lambda i: (i, 0))],
        out_shape=[jax.ShapeDtypeStruct((n, nq), BF),
                   jax.ShapeDtypeStruct((n, nk), F32),
                   jax.ShapeDtypeStruct((n, nk), F32)],
        compiler_params=_cparams("parallel"),
        name="qkv_proj",
    )(x2, w, b)


def _swa_prompt_kernel(sink_ref, q_ref, kc_ref, kp_ref, vc_ref, vp_ref, bias_ref, o_ref):
    blk = pl.program_id(1)
    kcat = jnp.concatenate([kp_ref[0], kc_ref[0]], axis=0).astype(BF)
    vcat = jnp.concatenate([vp_ref[0], vc_ref[0]], axis=0).astype(BF)
    col = lax.broadcasted_iota(jnp.int32, (WINDOW, 2 * WINDOW), 1)
    no_prev = jnp.logical_and(blk == 0, col < WINDOW)
    outs = []
    for h in range(N_HEADS):
        g = h // GROUP
        qh = q_ref[0, :, h * HEAD_DIM:(h + 1) * HEAD_DIM]
        kh = kcat[:, g * HEAD_DIM:(g + 1) * HEAD_DIM]
        vh = vcat[:, g * HEAD_DIM:(g + 1) * HEAD_DIM]
        s = lax.dot_general(qh, kh, NT_DIMS, preferred_element_type=F32) + bias_ref[h]
        s = jnp.where(no_prev, NEG, s)
        sink = sink_ref[h]
        m = jnp.maximum(jnp.max(s, axis=-1, keepdims=True), sink)
        p = jnp.exp(s - m)
        den = jnp.sum(p, axis=-1, keepdims=True) + jnp.exp(sink - m)
        outs.append(_dot(p.astype(BF), vh) / den)
    o_ref[0] = jnp.concatenate(outs, axis=-1).astype(o_ref.dtype)


def _swa_prompt(q, k, v, bias, sinks):
    B, T, _ = q.shape
    nb = T // WINDOW
    nk = N_KV_HEADS * HEAD_DIM
    cur = lambda b, i: (b, i, 0)
    prev = lambda b, i: (b, jnp.maximum(i - 1, 0), 0)
    return pl.pallas_call(
        _swa_prompt_kernel,
        grid=(B, nb),
        in_specs=[pl.BlockSpec(memory_space=pltpu.SMEM),
                  pl.BlockSpec((1, WINDOW, D_MODEL), cur),
                  pl.BlockSpec((1, WINDOW, nk), cur),
                  pl.BlockSpec((1, WINDOW, nk), prev),
                  pl.BlockSpec((1, WINDOW, nk), cur),
                  pl.BlockSpec((1, WINDOW, nk), prev),
                  _full(bias.shape)],
        out_specs=pl.BlockSpec((1, WINDOW, D_MODEL), cur),
        out_shape=jax.ShapeDtypeStruct((B, T, D_MODEL), BF),
        compiler_params=_cparams("parallel", "parallel"),
        name="swa_prompt",
    )(sinks, q, k, k, v, v, bias)


def _qkv_t_kernel(x_ref, wq_ref, wkvt_ref, bq_ref, bkvt_ref, q_ref, kvt_ref):
    xb = x_ref[...].astype(BF)
    q_ref[...] = (_dot(xb, wq_ref[...]) + bq_ref[...]) * HEAD_DIM ** -0.5
    kvt_ref[...] = lax.dot_general(wkvt_ref[...], xb, NT_DIMS, preferred_element_type=F32) + bkvt_ref[...]


def _qkv_proj_t(x2, wq, wkvt, bq, bkvt, layer, tm=256):
    n = x2.shape[0]
    tm = min(tm, n)
    nq = N_HEADS * HEAD_DIM
    nkv = 2 * N_KV_HEADS * HEAD_DIM
    return pl.pallas_call(
        _qkv_t_kernel,
        grid=(n // tm,),
        in_specs=[pl.BlockSpec((tm, D_MODEL), lambda i: (i, 0)),
                  _layer(wq.shape, layer), _layer(wkvt.shape, layer),
                  _layer(bq.shape, layer), _layer(bkvt.shape, layer)],
        out_specs=[pl.BlockSpec((tm, nq), lambda i: (i, 0)),
                   pl.BlockSpec((nkv, tm), lambda i: (0, i))],
        out_shape=[jax.ShapeDtypeStruct((n, nq), F32),
                   jax.ShapeDtypeStruct((nkv, n), F32)],
        compiler_params=_cparams("parallel"),
        name="qkv_proj_t",
    )(x2, wq, wkvt, bq, bkvt)


def _swa_sample_kernel(T, first, sink_ref, q_ref, kvt_ref, ck_ref, cv_ref, bc_ref, bn_ref, *rest):
    o_ref, nk_ref, nv_ref = rest[-3:]
    bb = q_ref.shape[0]
    L = ck_ref.shape[-1]
    HD = HEAD_DIM
    if first:
        for lyr in range(1, nk_ref.shape[0]):
            nk_ref[lyr] = jnp.zeros(nk_ref.shape[1:], F32)
            nv_ref[lyr] = jnp.zeros(nv_ref.shape[1:], F32)
        nk_out, nv_out = nk_ref.at[0], nv_ref.at[0]
    else:
        nk_out, nv_out = nk_ref, nv_ref
    units = [(b, g) for b in range(bb) for g in range(N_KV_HEADS)]
    lane = lax.broadcasted_iota(jnp.int32, (HD, L), 1)
    knew = [kvt_ref[g * HD:(g + 1) * HD, :] for g in range(N_KV_HEADS)]
    vnew = [kvt_ref[(N_KV_HEADS + g) * HD:(N_KV_HEADS + g + 1) * HD, :] for g in range(N_KV_HEADS)]
    sink = [jnp.concatenate([jnp.full((T, 1), sink_ref[g * GROUP + j], F32) for j in range(GROUP)], axis=0)
            for g in range(N_KV_HEADS)]

    def shift_in(old, new, b):
        moved = pltpu.roll(new, (L - T - b * T) % L, 1)
        return jnp.where(lane >= L - T, moved, pltpu.roll(old, L - T, 1))

    sc, sn = {}, {}
    for b in range(bb):
        q = q_ref[b]
        for g in range(N_KV_HEADS):
            qg = jnp.concatenate([q[:, (g * GROUP + j) * HD:(g * GROUP + j + 1) * HD]
                                  for j in range(GROUP)], axis=0).astype(BF)
            kc = ck_ref[b, g]
            nk_out[b, g] = shift_in(kc, knew[g], b)
            sc[b, g] = _dot(qg, kc.astype(BF)) + bc_ref[g]
            sn[b, g] = _dot(qg, knew[g][:, b * T:(b + 1) * T].astype(BF)) + bn_ref[g]
    sc_all = jnp.concatenate([sc[u] for u in units], axis=0)
    sn_all = jnp.concatenate([sn[u] for u in units], axis=0)
    sink_all = jnp.concatenate([sink[g] for _, g in units], axis=0)
    m = jnp.maximum(jnp.maximum(jnp.max(sc_all, axis=-1, keepdims=True),
                                jnp.max(sn_all, axis=-1, keepdims=True)), sink_all)
    pc_all = jnp.exp(sc_all - m)
    pn_all = jnp.exp(sn_all - m)
    inv_all = 1.0 / (jnp.sum(pc_all, axis=-1, keepdims=True) + jnp.sum(pn_all, axis=-1, keepdims=True)
                     + jnp.exp(sink_all - m))
    pc_all = pc_all.astype(BF)
    pn_all = pn_all.astype(BF)
    gt = GROUP * T
    pc = {u: pc_all[i * gt:(i + 1) * gt] for i, u in enumerate(units)}
    pn = {u: pn_all[i * gt:(i + 1) * gt] for i, u in enumerate(units)}
    inv = {u: inv_all[i * gt:(i + 1) * gt] for i, u in enumerate(units)}
    for b in range(bb):
        pieces = []
        for g in range(N_KV_HEADS):
            vc = cv_ref[b, g]
            nv_out[b, g] = shift_in(vc, vnew[g], b)
            og = (lax.dot_general(pc[b, g], vc.astype(BF), NT_DIMS, preferred_element_type=F32)
                  + lax.dot_general(pn[b, g], vnew[g][:, b * T:(b + 1) * T].astype(BF),
                                    NT_DIMS, preferred_element_type=F32)) * inv[b, g]
            pieces.extend(og[j * T:(j + 1) * T] for j in range(GROUP))
        o_ref[b] = jnp.concatenate(pieces, axis=-1).astype(o_ref.dtype)


def _swa_sample(q, kvt, ck, cv, layer, bias, sinks, nk_prev, nv_prev):
    B, T, _ = q.shape
    nl, L = ck.shape[0], ck.shape[-1]
    first = nk_prev is None
    bb = L // T
    assert B % bb == 0 and L % T == 0
    bias = bias.reshape(N_KV_HEADS, GROUP * T, L + T)
    bc, bn = bias[:, :, :L], bias[:, :, L:]
    qblk = pl.BlockSpec((bb, T, D_MODEL), lambda i: (i, 0, 0))
    cblk = pl.BlockSpec((None, bb, N_KV_HEADS, HEAD_DIM, L), lambda i: (layer, i, 0, 0, 0))
    nblk = (pl.BlockSpec((nl, bb, N_KV_HEADS, HEAD_DIM, L), lambda i: (0, i, 0, 0, 0)) if first else cblk)
    ins = [sinks, q, kvt, ck, cv, bc, bn]
    in_specs = [pl.BlockSpec(memory_space=pltpu.SMEM), qblk,
                pl.BlockSpec((kvt.shape[0], bb * T), lambda i: (0, i)),
                cblk, cblk, _full(bc.shape), _full(bn.shape)]
    aliases = {}
    if not first:
        aliases = {len(ins): 1, len(ins) + 1: 2}
        ins += [nk_prev, nv_prev]
        in_specs += [pl.BlockSpec(memory_space=pl.ANY)] * 2
    return pl.pallas_call(
        functools.partial(_swa_sample_kernel, T, first),
        grid=(B // bb,),
        in_specs=in_specs,
        out_specs=[qblk, nblk, nblk],
        out_shape=[jax.ShapeDtypeStruct((B, T, D_MODEL), BF),
                   jax.ShapeDtypeStruct((nl, B, N_KV_HEADS, HEAD_DIM, L), F32),
                   jax.ShapeDtypeStruct((nl, B, N_KV_HEADS, HEAD_DIM, L), F32)],
        input_output_aliases=aliases,
        compiler_params=_cparams("parallel"),
        name="swa_sample",
    )(*ins)


def _proj_ln_kernel(which, a_ref, w_ref, b_ref, x_ref, g_ref, beta_ref, o_ref):
    y = _dot(a_ref[...].astype(BF), w_ref[...]) + b_ref[...]
    o_ref[...] = _layer_norm(ALPHA * x_ref[...] + y, g_ref[which:which + 1, :], beta_ref[which:which + 1, :])


def _proj_ln(a, w, b, x2, ln_g, ln_b, layer, depth, tm=512):
    n, kdim = a.shape
    tm = min(tm, n)
    rows = lambda i: (i, 0)
    return pl.pallas_call(
        functools.partial(_proj_ln_kernel, 0),
        grid=(n // tm,),
        in_specs=[pl.BlockSpec((tm, kdim), rows), _layer(w.shape, layer), _layer(b.shape, layer),
                  pl.BlockSpec((tm, D_MODEL), rows), _layer(ln_g.shape, depth), _layer(ln_b.shape, depth)],
        out_specs=pl.BlockSpec((tm, D_MODEL), rows),
        out_shape=jax.ShapeDtypeStruct((n, D_MODEL), F32),
        compiler_params=_cparams("parallel"),
        name="proj_ln",
    )(a, w, b, x2, ln_g, ln_b)


def _ffn_kernel(x_ref, wg_ref, wu_ref, wd_ref, g_ref, beta_ref, o_ref, acc_ref, xb_ref):
    j = pl.program_id(1)

    @pl.when(j == 0)
    def _():
        acc_ref[...] = jnp.zeros_like(acc_ref)
        xb_ref[...] = x_ref[...].astype(BF)

    xb = xb_ref[...]
    gt = _dot(xb, wg_ref[...])
    ut = _dot(xb, wu_ref[...])
    h = (gt * _sigmoid(gt) * ut).astype(BF)
    acc_ref[...] += _dot(h, wd_ref[...])

    @pl.when(j == pl.num_programs(1) - 1)
    def _():
        o_ref[...] = _layer_norm(ALPHA * x_ref[...] + acc_ref[...], g_ref[1:2, :], beta_ref[1:2, :])


def _ffn(x2, w_gu, w_down, ln_g, ln_b, depth, tm=512, tf=1408):
    n = x2.shape[0]
    tm = min(tm, n)
    nf = D_FF // tf
    rows = lambda i, j: (i, 0)
    return pl.pallas_call(
        _ffn_kernel,
        grid=(n // tm, nf),
        in_specs=[pl.BlockSpec((tm, D_MODEL), rows),
                  pl.BlockSpec((None, D_MODEL, tf), lambda i, j: (depth, 0, j)),
                  pl.BlockSpec((None, D_MODEL, tf), lambda i, j: (depth, 0, j + nf)),
                  pl.BlockSpec((None, tf, D_MODEL), lambda i, j: (depth, j, 0)),
                  _layer(ln_g.shape, depth), _layer(ln_b.shape, depth)],
        out_specs=pl.BlockSpec((tm, D_MODEL), rows),
        out_shape=jax.ShapeDtypeStruct((n, D_MODEL), F32),
        scratch_shapes=[pltpu.VMEM((tm, D_MODEL), F32), pltpu.VMEM((tm, D_MODEL), BF)],
        compiler_params=_cparams("parallel", "arbitrary"),
        name="ffn",
    )(x2, w_gu, w_gu, w_down, ln_g, ln_b)


def _head_sum(z, m_ref):
    hi, lo = _split2(z)
    m = m_ref[...]
    w = m.shape[0]
    return jnp.concatenate([_dot(hi[:, c:c + w], m) + _dot(lo[:, c:c + w], m)
                            for c in range(0, z.shape[1], w)], axis=1)


def _rwkv_proj_kernel(has_vres, *refs):
    if has_vres:
        (x_ref, xp_ref, mix_ref, vec_ref, wr_ref, wk_ref, wv_ref, w1_ref, w2_ref, a1_ref, a2_ref,
         g1_ref, g2_ref, m_ref, v1_ref, v2_ref, vf_ref,
         r_ref, w_ref, k_ref, v_ref, kk_ref, a_ref, g_ref, carry_ref) = refs
    else:
        (x_ref, xp_ref, mix_ref, vec_ref, wr_ref, wk_ref, wv_ref, w1_ref, w2_ref, a1_ref, a2_ref,
         g1_ref, g2_ref, m_ref,
         r_ref, w_ref, k_ref, v_ref, kk_ref, a_ref, g_ref, carry_ref) = refs
    bb, tt, D = x_ref.shape
    n = bb * tt
    t = pl.program_id(1)
    x3 = x_ref[...]
    prev = jnp.where(t == 0, xp_ref[...], carry_ref[...])
    carry_ref[...] = x3[:, tt - 1:tt, :]
    x = x3.reshape(n, D)
    prev_rows = jnp.broadcast_to(prev, (bb, tt, D)).reshape(n, D)
    row = lax.broadcasted_iota(jnp.int32, (n, D), 0)
    xs = jnp.where(row % tt == 0, prev_rows, pltpu.roll(x, 1, 0))
    xx = xs - x
    xr, xw, xk, xv, xa, xg = ((x + xx * mix_ref[c:c + 1, :]).astype(BF) for c in range(6))
    w0, a0, v0, k_k, k_a = (vec_ref[c:c + 1, :] for c in (V_W0, V_A0, V_V0, V_KK, V_KA))

    r = _dot(xr, wr_ref[...])
    k = _dot(xk, wk_ref[...])
    v = _dot(xv, wv_ref[...])
    z = w0 + _dot(jnp.tanh(_dot(xw, w1_ref[...])).astype(BF), w2_ref[...])
    w = -(jnp.maximum(-z, 0.0) + jnp.log1p(jnp.exp(-jnp.abs(z)))) - 0.5
    logd = -jnp.exp(w)
    if has_vres:
        gate = _sigmoid(v0 + _dot(_dot(xv, v1_ref[...]).astype(BF), v2_ref[...]))
        v = v + (vf_ref[...].reshape(n, D) - v) * gate
    a = _sigmoid(a0 + _dot(_dot(xa, a1_ref[...]).astype(BF), a2_ref[...]))
    g = _dot(_sigmoid(_dot(xg, g1_ref[...])).astype(BF), g2_ref[...])
    kk = k * k_k
    kk = kk / jnp.maximum(jnp.sqrt(_head_sum(kk * kk, m_ref)), 1e-12)
    k = k * (1.0 + (a - 1.0) * k_a)
    for ref, val in ((r_ref, r), (w_ref, logd), (k_ref, k), (v_ref, v), (kk_ref, kk), (a_ref, a),
                     (g_ref, g)):
        ref[...] = val.reshape(bb, tt, D)


def _rwkv_proj(x, x_prev, v_first, P, j, bb, tt):
    B, T, D = x.shape
    has_vres = j > 0
    blk = pl.BlockSpec((bb, tt, D), lambda b, t: (b, t, 0))
    w_rkv = P['rwkv_w_rkv']
    ins = [x, x_prev, P['rwkv_mix'], P['rwkv_vecs'], w_rkv, w_rkv, w_rkv,
           P['rwkv_w1'], P['rwkv_w2'], P['rwkv_a1'], P['rwkv_a2'], P['rwkv_g1'], P['rwkv_g2'],
           P['head_ones']]
    in_specs = [blk, pl.BlockSpec((bb, 1, D), lambda b, t: (b, 0, 0)),
                _layer(ins[2].shape, j), _layer(ins[3].shape, j),
                _layer(w_rkv.shape, j, 0), _layer(w_rkv.shape, j, 1), _layer(w_rkv.shape, j, 2)]
    in_specs += [_layer(a.shape, j) for a in ins[7:13]] + [_full(P['head_ones'].shape)]
    if has_vres:
        ins += [P['rwkv_v1'], P['rwkv_v2'], v_first]
        in_specs += [_layer(P['rwkv_v1'].shape, j - 1), _layer(P['rwkv_v2'].shape, j - 1), blk]
    return pl.pallas_call(
        functools.partial(_rwkv_proj_kernel, has_vres),
        grid=(B // bb, T // tt),
        in_specs=in_specs,
        out_specs=[blk] * 7,
        out_shape=[jax.ShapeDtypeStruct((B, T, D), F32)] * 7,
        scratch_shapes=[pltpu.VMEM((bb, 1, D), F32)],
        compiler_params=_cparams("parallel", "arbitrary"),
        name="rwkv_proj",
    )(*ins)


def _wkv_kernel(has_s0, first, C, *refs):
    r_ref, w_ref, k_ref, v_ref, kk_ref, a_ref = refs[:6]
    s0_ref = refs[6] if has_s0 else None
    o_ref, so_ref, s_scr = refs[-3:]
    groups = r_ref.shape[0]
    R = WKV_ROWS
    N = RWKV_HEAD
    NP = RWKV_HEADS // 2
    nseq = R // C
    t = pl.program_id(1)
    so_cur = so_ref.at[0] if first else so_ref

    @pl.when(t == 0)
    def _():
        if has_s0:
            for s in range(groups * nseq):
                for p in range(NP):
                    s_scr[s, p] = jnp.concatenate([s0_ref[s, 2 * p], s0_ref[s, 2 * p + 1]], axis=1)
        else:
            s_scr[...] = jnp.zeros_like(s_scr)

    D = r_ref.shape[-1]
    GR = groups * R
    flat = lambda ref: ref[...].reshape(GR, D)
    logd = flat(w_ref)
    ri = lax.broadcasted_iota(jnp.int32, (GR, GR), 0)
    ci = lax.broadcasted_iota(jnp.int32, (GR, GR), 1)
    tril = jnp.logical_and(ri // C == ci // C, ci <= ri).astype(BF)
    cum = sum(_dot(tril, part) for part in _split3(logd))
    g_in = jnp.exp(cum)
    g_ex = jnp.exp(cum - logd)
    g_inv = jnp.exp(-cum)
    kk = flat(kk_ref)
    at = -(kk * g_ex)
    rt = flat(r_ref) * g_in
    bt = kk * flat(a_ref) * g_inv
    kt = flat(k_ref) * g_inv
    v = flat(v_ref)

    def lane_masks(width):
        rr = lax.broadcasted_iota(jnp.int32, (R, width), 0)
        cc = lax.broadcasted_iota(jnp.int32, (R, width), 1) % N
        same = rr // C == cc // C
        return jnp.logical_and(same, cc < rr), jnp.logical_and(same, cc <= rr)

    strict2, _ = lane_masks(2 * N)
    _, incl4 = lane_masks(4 * N)

    def bdiag(m):
        lo = lax.broadcasted_iota(jnp.int32, m.shape, 1) < N
        zero = jnp.zeros_like(m)
        return jnp.concatenate([jnp.where(lo, m, zero), jnp.where(lo, zero, m)], axis=0)

    units = [(gi, p) for gi in range(groups) for p in range(NP)]
    seqs = range(nseq)
    tile = lambda z, u: z[u[0] * R:(u[0] + 1) * R, 2 * N * u[1]:2 * N * (u[1] + 1)]
    rows = [slice(C * s, C * (s + 1)) for s in seqs]
    vp = {u: tile(v, u) for u in units}
    vpb = {u: vp[u].astype(BF) for u in units}
    btp = {u: tile(bt, u) for u in units}
    ktp = {u: tile(kt, u) for u in units}
    xa = {u: tile(at, u) for u in units}
    xr = {u: tile(rt, u) for u in units}
    S0 = {(u, s): s_scr[u[0] * nseq + s, u[1]] for u in units for s in seqs}

    xp = {u: jnp.concatenate([xa[u], xr[u]], axis=0).astype(BF) for u in units}
    bmat = {u: jnp.concatenate([bdiag(btp[u].astype(BF)), bdiag(ktp[u].astype(BF))], axis=0)
            for u in units}
    G = {u: lax.dot_general(xp[u], bmat[u], NT_DIMS, preferred_element_type=F32) for u in units}
    Z = {(u, s): lax.dot_general(
        xp[u] if nseq == 1 else
        jnp.concatenate([xa[u][rows[s]], xr[u][rows[s]]], axis=0).astype(BF),
        bdiag(S0[u, s].astype(BF)), NT_DIMS, preferred_element_type=F32) for u in units for s in seqs}
    za = {u: Z[u, 0][:C] if nseq == 1 else jnp.concatenate([Z[u, s][:C] for s in seqs], axis=0)
          for u in units}
    zr = {u: Z[u, 0][C:] if nseq == 1 else jnp.concatenate([Z[u, s][C:] for s in seqs], axis=0)
          for u in units}
    A = {u: jnp.where(strict2, G[u][:R, :2 * N], 0.0) for u in units}
    aak = {u: jnp.where(strict2, G[u][:R, 2 * N:], 0.0).astype(BF) for u in units}
    T2 = {u: jnp.where(incl4, G[u][R:], 0.0).astype(BF) for u in units}
    W = {u: za[u] + _dot(aak[u], bdiag(vpb[u])) for u in units}
    nsteps = max(1, int(math.log2(C)))
    for step in range(nsteps):
        Ab = {u: A[u].astype(BF) for u in units}
        if step + 1 < nsteps:
            res = {u: _dot(Ab[u], jnp.concatenate([bdiag(W[u].astype(BF)), bdiag(Ab[u])], axis=1))
                   for u in units}
            W = {u: W[u] + res[u][:, :2 * N] for u in units}
            A = {u: res[u][:, 2 * N:] for u in units}
        else:
            W = {u: W[u] + _dot(Ab[u], bdiag(W[u].astype(BF))) for u in units}
    wb = {u: W[u].astype(BF) for u in units}
    for u in units:
        uv = jnp.concatenate([bdiag(wb[u]), bdiag(vpb[u])], axis=0)
        o_ref[u[0], :, 2 * N * u[1]:2 * N * (u[1] + 1)] = zr[u] + _dot(T2[u], uv)
    lo = lax.broadcasted_iota(jnp.int32, (N, 2 * N), 1) < N
    for u in units:
        for s in seqs:
            uvs = jnp.concatenate([W[u][rows[s]], vp[u][rows[s]]], axis=0).astype(BF)
            ys = jnp.concatenate([btp[u][rows[s]], ktp[u][rows[s]]], axis=0).astype(BF)
            full = lax.dot_general(uvs, ys, TN_DIMS, preferred_element_type=F32)
            upd = jnp.where(lo, full[:N], full[N:])
            last = u[0] * R + C * (s + 1) - 1
            g_end = g_in[last:last + 1, 2 * N * u[1]:2 * N * (u[1] + 1)]
            s_scr[u[0] * nseq + s, u[1]] = (S0[u, s] + upd) * g_end

    @pl.when(t == pl.num_programs(1) - 1)
    def _():
        for s in range(groups * nseq):
            for p in range(NP):
                so_cur[s, 2 * p] = s_scr[s, p][:, :N]
                so_cur[s, 2 * p + 1] = s_scr[s, p][:, N:]
        if first:
            for lyr in range(1, so_ref.shape[0]):
                so_ref[lyr] = jnp.zeros(so_ref.shape[1:], F32)


def _wkv(r, logd, k, v, kk, a, s0, layer, nlayers, s_prev, C, groups):
    B, T, D = r.shape
    has_s0 = s0 is not None
    first = s_prev is None
    R = WKV_ROWS
    nseq = R // C
    nt = T // C
    assert (nseq == 1 or nt == 1) and T % C == 0
    nbk = B * T // (R * nt)
    assert nbk % groups == 0
    gs = groups * nseq
    blk = pl.BlockSpec((groups, None, R, D), lambda b, t: (b, t, 0, 0))
    sdims = (RWKV_HEADS, RWKV_HEAD, RWKV_HEAD)
    sblk = pl.BlockSpec((None, gs) + sdims, lambda b, t: (layer, b, 0, 0, 0))
    oblk = pl.BlockSpec((nlayers, gs) + sdims, lambda b, t: (0, b, 0, 0, 0)) if first else sblk
    ins = [x.reshape(nbk, nt, R, D) for x in (r, logd, k, v, kk, a)]
    in_specs = [blk] * 6
    if has_s0:
        ins.append(s0)
        in_specs.append(sblk)
    aliases = {}
    if not first:
        aliases = {len(ins): 1}
        ins.append(s_prev)
        in_specs.append(pl.BlockSpec(memory_space=pl.ANY))
    o, s_new = pl.pallas_call(
        functools.partial(_wkv_kernel, has_s0, first, C),
        grid=(nbk // groups, nt),
        in_specs=in_specs,
        out_specs=[blk, oblk],
        out_shape=[jax.ShapeDtypeStruct((nbk, nt, R, D), F32),
                   jax.ShapeDtypeStruct((nlayers, B) + sdims, F32)],
        scratch_shapes=[pltpu.VMEM((gs, RWKV_HEADS // 2, RWKV_HEAD, 2 * RWKV_HEAD), F32)],
        input_output_aliases=aliases,
        compiler_params=_cparams("parallel", "arbitrary"),
        name="wkv",
    )(*ins)
    return o.reshape(B, T, D), s_new


def _rwkv_out_kernel(o_ref, r_ref, k_ref, v_ref, gate_ref, x_ref, vec_ref, m_ref, wo_ref,
                     g_ref, beta_ref, y_ref):
    r_k, ln_w, ln_b = (vec_ref[c:c + 1, :] for c in (V_RK, V_LNW, V_LNB))
    o = o_ref[...]
    inv_n = 1.0 / RWKV_HEAD
    mu = _head_sum(o, m_ref) * inv_n
    oc = o - mu
    var = _head_sum(oc * oc, m_ref) * inv_n
    on = oc * lax.rsqrt(var + GN_EPS) * ln_w + ln_b
    v = v_ref[...]
    bonus = _head_sum(r_ref[...] * k_ref[...] * r_k, m_ref) * v
    y = _dot(((on + bonus) * gate_ref[...]).astype(BF), wo_ref[...])
    y_ref[...] = _layer_norm(ALPHA * x_ref[...] + y, g_ref[0:1, :], beta_ref[0:1, :])


def _rwkv_out(o, r, k, v, gate, x2, P, j, depth, tm=256):
    n = o.shape[0]
    tm = min(tm, n)
    rows = pl.BlockSpec((tm, D_MODEL), lambda i: (i, 0))
    vecs, w_o, ln_g, ln_b = P['rwkv_vecs'], P['rwkv_w_o'], P['ln_g'], P['ln_b']
    return pl.pallas_call(
        _rwkv_out_kernel,
        grid=(n // tm,),
        in_specs=[rows] * 6 + [_layer(vecs.shape, j), _full(P['head_ones'].shape), _layer(w_o.shape, j),
                               _layer(ln_g.shape, depth), _layer(ln_b.shape, depth)],
        out_specs=rows,
        out_shape=jax.ShapeDtypeStruct((n, D_MODEL), F32),
        compiler_params=_cparams("parallel"),
        name="rwkv_out",
    )(o, r, k, v, gate, x2, vecs, P['head_ones'], w_o, ln_g, ln_b)


def _trunk(x, win_k, win_v, shift, wkv, P, bias):
    prompt = win_k is None
    B, T, D = x.shape
    n = B * T
    nkv = N_KV_HEADS * HEAD_DIM
    n_rwkv = DEPTH // 2
    v_first = None
    nk, nv, ns = [], [], []
    nk_all = nv_all = s_all = None
    for i in range(DEPTH):
        j = i // 2
        x2 = x.reshape(n, D)
        if i % 2 == 0:
            if prompt:
                q, k, v = _qkv_proj(x2, P['attn_w_qkv'], P['attn_b_qkv'], j)
                q, k, v = q.reshape(B, T, D), k.reshape(B, T, nkv), v.reshape(B, T, nkv)
                o = _swa_prompt(q, k, v, bias, P['attn_sinks'][j])
                nk.append(k[:, T - WINDOW:].reshape(B, WINDOW, N_KV_HEADS, HEAD_DIM))
                nv.append(v[:, T - WINDOW:].reshape(B, WINDOW, N_KV_HEADS, HEAD_DIM))
            else:
                q, kvt = _qkv_proj_t(x2, P['attn_wq'], P['attn_wkvt'], P['attn_bq'], P['attn_bkvt'], j)
                o, nk_all, nv_all = _swa_sample(q.reshape(B, T, D), kvt, win_k, win_v, j, bias,
                                                P['attn_sinks'][j], nk_all, nv_all)
            x2 = _proj_ln(o.reshape(n, D), P['attn_w_o'], P['attn_b_o'], x2, P['ln_g'], P['ln_b'], j, i)
        else:
            x_prev = jnp.zeros((B, 1, D), x.dtype) if prompt else shift[j].reshape(B, 1, D)
            bb, tt = (1, 256) if prompt else (min(32, B), T)
            r, logd, k, v, kk, a, gate = _rwkv_proj(x, x_prev, v_first, P, j, bb, tt)
            if j == 0:
                v_first = v
            C, groups = (WKV_ROWS, 2 if B % 2 == 0 else 1) if prompt else (T, 1)
            o, s_all = _wkv(r, logd, k, v, kk, a, wkv, j, n_rwkv, s_all, C, groups)
            ns.append(x[:, -1])
            x2 = _rwkv_out(o.reshape(n, D), r.reshape(n, D), k.reshape(n, D), v.reshape(n, D),
                           gate.reshape(n, D), x2, P, j, i)
        x2 = _ffn(x2, P['ffn_w_gu'], P['ffn_w_down'], P['ln_g'], P['ln_b'], i)
        x = x2.reshape(B, T, D)
    if prompt:
        nk_all, nv_all = jnp.stack(nk), jnp.stack(nv)
    else:
        nk_all = jnp.transpose(nk_all, (0, 1, 4, 2, 3))
        nv_all = jnp.transpose(nv_all, (0, 1, 4, 2, 3))
    return x, nk_all, nv_all, jnp.stack(ns), s_all


def kernel(x_prompt, x_sample, cache_win_k, cache_win_v, state_shift, state_wkv, rel_bias, ln_g, ln_b, attn_w_qkv, attn_b_qkv, attn_w_o, attn_b_o, attn_sinks, rwkv_mix, rwkv_w_rkv, rwkv_w0, rwkv_w1, rwkv_w2, rwkv_a0, rwkv_a1, rwkv_a2, rwkv_v0, rwkv_v1, rwkv_v2, rwkv_g1, rwkv_g2, rwkv_k_k, rwkv_k_a, rwkv_r_k, rwkv_ln_w, rwkv_ln_b, rwkv_w_o, ffn_w_gu, ffn_w_down):
    bf = lambda w: w.astype(BF)
    lane_head = np.arange(MXU_WIDTH) // RWKV_HEAD
    head_ones = jnp.asarray(lane_head[:, None] == lane_head[None, :], dtype=BF)
    n_rwkv = rwkv_w0.shape[0]
    nq = N_HEADS * HEAD_DIM
    v0_rows = jnp.concatenate([jnp.zeros((1, D_MODEL), F32), rwkv_v0], axis=0)
    rwkv_vecs = jnp.stack([rwkv_w0, rwkv_a0, v0_rows, rwkv_k_k, rwkv_k_a,
                           rwkv_r_k.reshape(n_rwkv, D_MODEL), rwkv_ln_w, rwkv_ln_b], axis=1)
    w_qkv = bf(attn_w_qkv)
    b_qkv = attn_b_qkv[:, None, :]
    P = dict(ln_g=ln_g, ln_b=ln_b,
             attn_w_qkv=w_qkv, attn_b_qkv=b_qkv,
             attn_wq=w_qkv[:, :, :nq], attn_wkvt=jnp.transpose(w_qkv[:, :, nq:], (0, 2, 1)),
             attn_bq=b_qkv[:, :, :nq], attn_bkvt=jnp.transpose(b_qkv[:, :, nq:], (0, 2, 1)),
             attn_w_o=bf(attn_w_o), attn_b_o=attn_b_o[:, None, :], attn_sinks=attn_sinks,
             rwkv_mix=rwkv_mix, rwkv_vecs=rwkv_vecs, rwkv_w_rkv=bf(rwkv_w_rkv),
             rwkv_w1=bf(rwkv_w1), rwkv_w2=bf(rwkv_w2), rwkv_a1=bf(rwkv_a1), rwkv_a2=bf(rwkv_a2),
             rwkv_v1=bf(rwkv_v1), rwkv_v2=bf(rwkv_v2), rwkv_g1=bf(rwkv_g1), rwkv_g2=bf(rwkv_g2),
             rwkv_w_o=bf(rwkv_w_o), ffn_w_gu=bf(ffn_w_gu), ffn_w_down=bf(ffn_w_down),
             head_ones=head_ones)

    T = x_sample.shape[1]
    L = cache_win_k.shape[2]
    d_prompt = np.arange(WINDOW)[:, None] + WINDOW - np.arange(2 * WINDOW)[None, :]
    d_sample = np.arange(T)[:, None] + L - np.arange(L + T)[None, :]
    bias_prompt = _bias_table(rel_bias, d_prompt)
    bias_sample = _bias_table(rel_bias, d_sample)

    y_prompt, pk, pv, ps, pw = _trunk(x_prompt, None, None, None, None, P, bias_prompt)
    win_k = jnp.transpose(cache_win_k, (0, 1, 3, 4, 2))
    win_v = jnp.transpose(cache_win_v, (0, 1, 3, 4, 2))
    y_sample, sk, sv, ss, sw = _trunk(x_sample, win_k, win_v, state_shift, state_wkv,
                                      P, bias_sample)
    return (y_prompt, y_sample, pk, pv, ps, pw, sk, sv, ss, sw)
```

```python
import functools
import math

import numpy as np
import jax
import jax.numpy as jnp
from jax import lax
from jax.experimental import pallas as pl
from jax.experimental.pallas import tpu as pltpu

D_MODEL = 1024
DEPTH = 4
HEAD_DIM = 64
N_HEADS = 16
N_KV_HEADS = 4
GROUP = 4
WINDOW = 128
N_BUCKETS = 32
MAX_DISTANCE = 128
RWKV_HEAD = 64
RWKV_HEADS = 16
GN_EPS = 64e-5
D_FF = 2816
ALPHA = (2 * DEPTH) ** 0.25
LN_EPS = 1e-5
NEG = -1e30
MXU_WIDTH = 256
WKV_ROWS = 64

BF = jnp.bfloat16
F32 = jnp.float32
VMEM_LIMIT = 56 * 1024 * 1024

NT_DIMS = (((1,), (1,)), ((), ()))
TN_DIMS = (((0,), (0,)), ((), ()))

V_W0, V_A0, V_V0, V_KK, V_KA, V_RK, V_LNW, V_LNB = range(8)


def _cparams(*sem):
    return pltpu.CompilerParams(dimension_semantics=sem, vmem_limit_bytes=VMEM_LIMIT)


def _dot(a, b):
    return jnp.dot(a, b, preferred_element_type=F32)


def _layer_norm(z, g, b):
    mu = jnp.mean(z, axis=-1, keepdims=True)
    zc = z - mu
    var = jnp.mean(zc * zc, axis=-1, keepdims=True)
    return zc * lax.rsqrt(var + LN_EPS) * g + b


def _sigmoid(z):
    return 1.0 / (1.0 + jnp.exp(-z))


def _split2(z):
    hi = z.astype(BF)
    lo = (z - hi.astype(F32)).astype(BF)
    return hi, lo


def _split3(z):
    hi = z.astype(BF)
    r1 = z - hi.astype(F32)
    mid = r1.astype(BF)
    lo = (r1 - mid.astype(F32)).astype(BF)
    return hi, mid, lo


def _full(shape):
    n = len(shape)
    return pl.BlockSpec(shape, lambda *_: (0,) * n)


def _layer(shape, *lead):
    k = len(lead)
    rest = len(shape) - k
    return pl.BlockSpec((None,) * k + tuple(shape[k:]), lambda *_: tuple(lead) + (0,) * rest,
                        pipeline_mode=pl.Buffered(1))


def _bias_table_kernel(rb_ref, bucket_ref, valid_ref, o_ref):
    bucket = bucket_ref[...]
    valid = valid_ref[...] > 0
    for h in range(N_HEADS):
        acc = jnp.zeros(bucket.shape, F32)
        for b in range(N_BUCKETS):
            acc = jnp.where(bucket == b, rb_ref[b, h], acc)
        o_ref[h] = jnp.where(valid, acc, NEG)


def _bias_table(rel_bias, d):
    dc = np.maximum(d, 0)
    exact = N_BUCKETS // 2
    df = np.maximum(dc, 1).astype(np.float32)
    large = exact + (np.log(df / np.float32(exact)) / np.float32(math.log(MAX_DISTANCE / exact))
                     * np.float32(N_BUCKETS - exact)).astype(np.int32)
    bucket = np.where(dc < exact, dc, np.minimum(large, N_BUCKETS - 1)).astype(np.int32)
    valid = ((d >= 0) & (d < WINDOW)).astype(np.int32)
    return pl.pallas_call(
        _bias_table_kernel,
        out_shape=jax.ShapeDtypeStruct((N_HEADS,) + d.shape, F32),
        in_specs=[pl.BlockSpec(memory_space=pltpu.SMEM),
                  pl.BlockSpec(memory_space=pltpu.VMEM),
                  pl.BlockSpec(memory_space=pltpu.VMEM)],
        out_specs=pl.BlockSpec(memory_space=pltpu.VMEM),
        name="bias_table",
    )(rel_bias, jnp.asarray(bucket), jnp.asarray(valid))


def _qkv_kernel(x_ref, w_ref, b_ref, q_ref, k_ref, v_ref):
    acc = _dot(x_ref[...].astype(BF), w_ref[...]) + b_ref[...]
    nq = N_HEADS * HEAD_DIM
    nk = N_KV_HEADS * HEAD_DIM
    q_ref[...] = (acc[:, :nq] * HEAD_DIM ** -0.5).astype(q_ref.dtype)
    k_ref[...] = acc[:, nq:nq + nk]
    v_ref[...] = acc[:, nq + nk:]


def _qkv_proj(x2, w, b, layer, tm=512):
    n = x2.shape[0]
    tm = min(tm, n)
    nq = N_HEADS * HEAD_DIM
    nk = N_KV_HEADS * HEAD_DIM
    return pl.pallas_call(
        _qkv_kernel,
        grid=(n // tm,),
        in_specs=[pl.BlockSpec((tm, D_MODEL), lambda i: (i, 0)),
                  _layer(w.shape, layer), _layer(b.shape, layer)],
        out_specs=[pl.BlockSpec((tm, nq), lambda i: (i, 0)),
                   pl.BlockSpec((tm, nk), lambda i: (i, 0)),
                   pl.BlockSpec((tm, nk), lambda i: (i, 0))],
        out_shape=[jax.ShapeDtypeStruct((n, nq), BF),
                   jax.ShapeDtypeStruct((n, nk), F32),
                   jax.ShapeDtypeStruct((n, nk), F32)],
        compiler_params=_cparams("parallel"),
        name="qkv_proj",
    )(x2, w, b)


def _swa_prompt_kernel(sink_ref, q_ref, kc_ref, kp_ref, vc_ref, vp_ref, bias_ref, o_ref):
    blk = pl.program_id(1)
    kcat = jnp.concatenate([kp_ref[0], kc_ref[0]], axis=0).astype(BF)
    vcat = jnp.concatenate([vp_ref[0], vc_ref[0]], axis=0).astype(BF)
    row = lax.broadcasted_iota(jnp.int32, (WINDOW, WINDOW), 0)
    col = lax.broadcasted_iota(jnp.int32, (WINDOW, WINDOW), 1)
    own = col <= row
    zero = jnp.zeros((WINDOW, WINDOW), BF)
    scores = []
    for h in range(N_HEADS):
        g = h // GROUP
        qh = q_ref[0, :, h * HEAD_DIM:(h + 1) * HEAD_DIM]
        kh = kcat[:, g * HEAD_DIM:(g + 1) * HEAD_DIM]
        s2 = lax.dot_general(qh, kh, NT_DIMS, preferred_element_type=F32)
        s_prev = jnp.where(blk == 0, NEG, s2[:, :WINDOW])
        scores.append(jnp.where(own, s2[:, WINDOW:], s_prev) + bias_ref[h])
    s = jnp.concatenate(scores, axis=0)
    m = jnp.max(s, axis=-1, keepdims=True)
    p = jnp.exp(s - m)
    tot = jnp.sum(p, axis=-1, keepdims=True)
    pb = p.astype(BF)
    outs = []
    for h in range(N_HEADS):
        g = h // GROUP
        hs = slice(h * WINDOW, (h + 1) * WINDOW)
        vh = vcat[:, g * HEAD_DIM:(g + 1) * HEAD_DIM]
        p2 = jnp.concatenate([jnp.where(own, zero, pb[hs]), jnp.where(own, pb[hs], zero)], axis=1)
        inv = 1.0 / (tot[hs] + jnp.exp(sink_ref[h] - m[hs]))
        outs.append(_dot(p2, vh) * inv)
    o_ref[0] = jnp.concatenate(outs, axis=-1).astype(o_ref.dtype)


def _swa_prompt(q, k, v, bias, sinks):
    B, T, _ = q.shape
    nb = T // WINDOW
    nk = N_KV_HEADS * HEAD_DIM
    cur = lambda b, i: (b, i, 0)
    prev = lambda b, i: (b, jnp.maximum(i - 1, 0), 0)
    return pl.pallas_call(
        _swa_prompt_kernel,
        grid=(B, nb),
        in_specs=[pl.BlockSpec(memory_space=pltpu.SMEM),
                  pl.BlockSpec((1, WINDOW, D_MODEL), cur),
                  pl.BlockSpec((1, WINDOW, nk), cur),
                  pl.BlockSpec((1, WINDOW, nk), prev),
                  pl.BlockSpec((1, WINDOW, nk), cur),
                  pl.BlockSpec((1, WINDOW, nk), prev),
                  _full(bias.shape)],
        out_specs=pl.BlockSpec((1, WINDOW, D_MODEL), cur),
        out_shape=jax.ShapeDtypeStruct((B, T, D_MODEL), BF),
        compiler_params=_cparams("parallel", "parallel"),
        name="swa_prompt",
    )(sinks, q, k, k, v, v, bias)


def _qkv_t_kernel(x_ref, wq_ref, wkvt_ref, bq_ref, bkvt_ref, q_ref, kvt_ref):
    xb = x_ref[...].astype(BF)
    q_ref[...] = (_dot(xb, wq_ref[...]) + bq_ref[...]) * HEAD_DIM ** -0.5
    kvt_ref[...] = lax.dot_general(wkvt_ref[...], xb, NT_DIMS, preferred_element_type=F32) + bkvt_ref[...]


def _qkv_proj_t(x2, wq, wkvt, bq, bkvt, layer, tm=256):
    n = x2.shape[0]
    tm = min(tm, n)
    nq = N_HEADS * HEAD_DIM
    nkv = 2 * N_KV_HEADS * HEAD_DIM
    return pl.pallas_call(
        _qkv_t_kernel,
        grid=(n // tm,),
        in_specs=[pl.BlockSpec((tm, D_MODEL), lambda i: (i, 0)),
                  _layer(wq.shape, layer), _layer(wkvt.shape, layer),
                  _layer(bq.shape, layer), _layer(bkvt.shape, layer)],
        out_specs=[pl.BlockSpec((tm, nq), lambda i: (i, 0)),
                   pl.BlockSpec((nkv, tm), lambda i: (0, i))],
        out_shape=[jax.ShapeDtypeStruct((n, nq), F32),
                   jax.ShapeDtypeStruct((nkv, n), F32)],
        compiler_params=_cparams("parallel"),
        name="qkv_proj_t",
    )(x2, wq, wkvt, bq, bkvt)


def _swa_sample_kernel(T, first, sink_ref, q_ref, kvt_ref, ck_ref, cv_ref, bc_ref, bn_ref, *rest):
    o_ref, nk_ref, nv_ref = rest[-3:]
    bb = q_ref.shape[0]
    L = ck_ref.shape[-1]
    HD = HEAD_DIM
    if first:
        for lyr in range(1, nk_ref.shape[0]):
            nk_ref[lyr] = jnp.zeros(nk_ref.shape[1:], F32)
            nv_ref[lyr] = jnp.zeros(nv_ref.shape[1:], F32)
        nk_out, nv_out = nk_ref.at[0], nv_ref.at[0]
    else:
        nk_out, nv_out = nk_ref, nv_ref
    units = [(b, g) for b in range(bb) for g in range(N_KV_HEADS)]
    lane = lax.broadcasted_iota(jnp.int32, (HD, L), 1)
    knew = [kvt_ref[g * HD:(g + 1) * HD, :] for g in range(N_KV_HEADS)]
    vnew = [kvt_ref[(N_KV_HEADS + g) * HD:(N_KV_HEADS + g + 1) * HD, :] for g in range(N_KV_HEADS)]

    def shift_in(old, new, b):
        moved = pltpu.roll(new, (L - T - b * T) % L, 1)
        return jnp.where(lane >= L - T, moved, pltpu.roll(old, L - T, 1))

    sc, sn = {}, {}
    for b in range(bb):
        q = q_ref[b]
        for g in range(N_KV_HEADS):
            qg = jnp.concatenate([q[:, (g * GROUP + j) * HD:(g * GROUP + j + 1) * HD]
                                  for j in range(GROUP)], axis=0).astype(BF)
            kc = ck_ref[b, g]
            nk_out[b, g] = shift_in(kc, knew[g], b)
            sc[b, g] = _dot(qg, kc.astype(BF)) + bc_ref[g]
            sn[b, g] = _dot(qg, knew[g][:, b * T:(b + 1) * T].astype(BF)) + bn_ref[g]
    sc_all = jnp.concatenate([sc[u] for u in units], axis=0)
    sn_all = jnp.concatenate([sn[u] for u in units], axis=0)
    m = jnp.maximum(jnp.max(sc_all, axis=-1, keepdims=True), jnp.max(sn_all, axis=-1, keepdims=True))
    pc_all = jnp.exp(sc_all - m)
    pn_all = jnp.exp(sn_all - m)
    tot = jnp.sum(pc_all, axis=-1, keepdims=True) + jnp.sum(pn_all, axis=-1, keepdims=True)
    pc_all = pc_all.astype(BF)
    pn_all = pn_all.astype(BF)
    gt = GROUP * T
    for b in range(bb):
        pieces = []
        for g in range(N_KV_HEADS):
            r0 = (b * N_KV_HEADS + g) * gt
            vc = cv_ref[b, g]
            nv_out[b, g] = shift_in(vc, vnew[g], b)
            og = (lax.dot_general(pc_all[r0:r0 + gt], vc.astype(BF), NT_DIMS, preferred_element_type=F32)
                  + lax.dot_general(pn_all[r0:r0 + gt], vnew[g][:, b * T:(b + 1) * T].astype(BF),
                                    NT_DIMS, preferred_element_type=F32))
            for j in range(GROUP):
                rj = slice(r0 + j * T, r0 + (j + 1) * T)
                inv = 1.0 / (tot[rj] + jnp.exp(sink_ref[g * GROUP + j] - m[rj]))
                pieces.append(og[j * T:(j + 1) * T] * inv)
        o_ref[b] = jnp.concatenate(pieces, axis=-1).astype(o_ref.dtype)


def _swa_sample(q, kvt, ck, cv, layer, bias, sinks, nk_prev, nv_prev):
    B, T, _ = q.shape
    nl, L = ck.shape[0], ck.shape[-1]
    first = nk_prev is None
    bb = L // T
    assert B % bb == 0 and L % T == 0
    bias = bias.reshape(N_KV_HEADS, GROUP * T, L + T)
    bc, bn = bias[:, :, :L], bias[:, :, L:]
    qblk = pl.BlockSpec((bb, T, D_MODEL), lambda i: (i, 0, 0))
    cblk = pl.BlockSpec((None, bb, N_KV_HEADS, HEAD_DIM, L), lambda i: (layer, i, 0, 0, 0))
    nblk = (pl.BlockSpec((nl, bb, N_KV_HEADS, HEAD_DIM, L), lambda i: (0, i, 0, 0, 0)) if first else cblk)
    ins = [sinks, q, kvt, ck, cv, bc, bn]
    in_specs = [pl.BlockSpec(memory_space=pltpu.SMEM), qblk,
                pl.BlockSpec((kvt.shape[0], bb * T), lambda i: (0, i)),
                cblk, cblk, _full(bc.shape), _full(bn.shape)]
    aliases = {}
    if not first:
        aliases = {len(ins): 1, len(ins) + 1: 2}
        ins += [nk_prev, nv_prev]
        in_specs += [pl.BlockSpec(memory_space=pl.ANY)] * 2
    return pl.pallas_call(
        functools.partial(_swa_sample_kernel, T, first),
        grid=(B // bb,),
        in_specs=in_specs,
        out_specs=[qblk, nblk, nblk],
        out_shape=[jax.ShapeDtypeStruct((B, T, D_MODEL), BF),
                   jax.ShapeDtypeStruct((nl, B, N_KV_HEADS, HEAD_DIM, L), F32),
                   jax.ShapeDtypeStruct((nl, B, N_KV_HEADS, HEAD_DIM, L), F32)],
        input_output_aliases=aliases,
        compiler_params=_cparams("parallel"),
        name="swa_sample",
    )(*ins)


def _proj_ln_kernel(which, a_ref, w_ref, b_ref, x_ref, g_ref, beta_ref, o_ref):
    y = _dot(a_ref[...].astype(BF), w_ref[...]) + b_ref[...]
    o_ref[...] = _layer_norm(ALPHA * x_ref[...] + y, g_ref[which:which + 1, :], beta_ref[which:which + 1, :])


def _proj_ln(a, w, b, x2, ln_g, ln_b, layer, depth, tm=512):
    n, kdim = a.shape
    tm = min(tm, n)
    rows = lambda i: (i, 0)
    return pl.pallas_call(
        functools.partial(_proj_ln_kernel, 0),
        grid=(n // tm,),
        in_specs=[pl.BlockSpec((tm, kdim), rows), _layer(w.shape, layer), _layer(b.shape, layer),
                  pl.BlockSpec((tm, D_MODEL), rows), _layer(ln_g.shape, depth), _layer(ln_b.shape, depth)],
        out_specs=pl.BlockSpec((tm, D_MODEL), rows),
        out_shape=jax.ShapeDtypeStruct((n, D_MODEL), F32),
        compiler_params=_cparams("parallel"),
        name="proj_ln",
    )(a, w, b, x2, ln_g, ln_b)


def _ffn_kernel(tf, x_ref, wgu_ref, wd_ref, g_ref, beta_ref, o_ref):
    x = x_ref[...]
    xb = x.astype(BF)
    acc = None
    for c in range(0, D_FF, tf):
        gt = _dot(xb, wgu_ref[:, c:c + tf])
        ut = _dot(xb, wgu_ref[:, D_FF + c:D_FF + c + tf])
        h = (gt * _sigmoid(gt) * ut).astype(BF)
        part = _dot(h, wd_ref[c:c + tf, :])
        acc = part if acc is None else acc + part
    o_ref[...] = _layer_norm(ALPHA * x + acc, g_ref[1:2, :], beta_ref[1:2, :])


def _ffn(x2, w_gu, w_down, ln_g, ln_b, depth, tm=512, tf=1408):
    n = x2.shape[0]
    tm = min(tm, n)
    assert D_FF % tf == 0 and tf % 128 == 0
    rows = lambda i: (i, 0)
    return pl.pallas_call(
        functools.partial(_ffn_kernel, tf),
        grid=(n // tm,),
        in_specs=[pl.BlockSpec((tm, D_MODEL), rows),
                  _layer(w_gu.shape, depth), _layer(w_down.shape, depth),
                  _layer(ln_g.shape, depth), _layer(ln_b.shape, depth)],
        out_specs=pl.BlockSpec((tm, D_MODEL), rows),
        out_shape=jax.ShapeDtypeStruct((n, D_MODEL), F32),
        compiler_params=_cparams("parallel"),
        name="ffn",
    )(x2, w_gu, w_down, ln_g, ln_b)


def _head_sum(z, m_ref):
    hi, lo = _split2(z)
    m = m_ref[...]
    w = m.shape[0]
    return jnp.concatenate([_dot(hi[:, c:c + w], m) + _dot(lo[:, c:c + w], m)
                            for c in range(0, z.shape[1], w)], axis=1)


def _rwkv_proj_kernel(has_vres, *refs):
    if has_vres:
        (x_ref, xp_ref, mix_ref, vec_ref, wr_ref, wk_ref, wv_ref, w1_ref, w2_ref, a1_ref, a2_ref,
         g1_ref, g2_ref, m_ref, v1_ref, v2_ref, vf_ref,
         r_ref, w_ref, k_ref, v_ref, kk_ref, a_ref, g_ref, carry_ref) = refs
    else:
        (x_ref, xp_ref, mix_ref, vec_ref, wr_ref, wk_ref, wv_ref, w1_ref, w2_ref, a1_ref, a2_ref,
         g1_ref, g2_ref, m_ref,
         r_ref, w_ref, k_ref, v_ref, kk_ref, a_ref, g_ref, carry_ref) = refs
    bb, tt, D = x_ref.shape
    n = bb * tt
    t = pl.program_id(1)
    x3 = x_ref[...]
    prev = jnp.where(t == 0, xp_ref[...], carry_ref[...])
    carry_ref[...] = x3[:, tt - 1:tt, :]
    x = x3.reshape(n, D)
    prev_rows = jnp.broadcast_to(prev, (bb, tt, D)).reshape(n, D)
    row = lax.broadcasted_iota(jnp.int32, (n, D), 0)
    xs = jnp.where(row % tt == 0, prev_rows, pltpu.roll(x, 1, 0))
    xx = xs - x
    xr, xw, xk, xv, xa, xg = ((x + xx * mix_ref[c:c + 1, :]).astype(BF) for c in range(6))
    w0, a0, v0, k_k, k_a = (vec_ref[c:c + 1, :] for c in (V_W0, V_A0, V_V0, V_KK, V_KA))

    r = _dot(xr, wr_ref[...])
    k = _dot(xk, wk_ref[...])
    v = _dot(xv, wv_ref[...])
    z = w0 + _dot(jnp.tanh(_dot(xw, w1_ref[...])).astype(BF), w2_ref[...])
    logd = -math.exp(-0.5) * _sigmoid(z)
    if has_vres:
        gate = _sigmoid(v0 + _dot(_dot(xv, v1_ref[...]).astype(BF), v2_ref[...]))
        v = v + (vf_ref[...].reshape(n, D) - v) * gate
    a = _sigmoid(a0 + _dot(_dot(xa, a1_ref[...]).astype(BF), a2_ref[...]))
    g = _dot(_sigmoid(_dot(xg, g1_ref[...])).astype(BF), g2_ref[...])
    kk = k * k_k
    kk = kk / jnp.maximum(jnp.sqrt(_head_sum(kk * kk, m_ref)), 1e-12)
    k = k * (1.0 + (a - 1.0) * k_a)
    for ref, val in ((r_ref, r), (w_ref, logd), (k_ref, k), (v_ref, v), (kk_ref, kk), (a_ref, a),
                     (g_ref, g)):
        ref[...] = val.reshape(bb, tt, D)


def _rwkv_proj(x, x_prev, v_first, P, j, bb, tt):
    B, T, D = x.shape
    has_vres = j > 0
    blk = pl.BlockSpec((bb, tt, D), lambda b, t: (b, t, 0))
    w_rkv = P['rwkv_w_rkv']
    ins = [x, x_prev, P['rwkv_mix'], P['rwkv_vecs'], w_rkv, w_rkv, w_rkv,
           P['rwkv_w1'], P['rwkv_w2'], P['rwkv_a1'], P['rwkv_a2'], P['rwkv_g1'], P['rwkv_g2'],
           P['head_ones']]
    in_specs = [blk, pl.BlockSpec((bb, 1, D), lambda b, t: (b, 0, 0)),
                _layer(ins[2].shape, j), _layer(ins[3].shape, j),
                _layer(w_rkv.shape, j, 0), _layer(w_rkv.shape, j, 1), _layer(w_rkv.shape, j, 2)]
    in_specs += [_layer(a.shape, j) for a in ins[7:13]] + [_full(P['head_ones'].shape)]
    if has_vres:
        ins += [P['rwkv_v1'], P['rwkv_v2'], v_first]
        in_specs += [_layer(P['rwkv_v1'].shape, j - 1), _layer(P['rwkv_v2'].shape, j - 1), blk]
    return pl.pallas_call(
        functools.partial(_rwkv_proj_kernel, has_vres),
        grid=(B // bb, T // tt),
        in_specs=in_specs,
        out_specs=[blk] * 7,
        out_shape=[jax.ShapeDtypeStruct((B, T, D), F32)] * 7,
        scratch_shapes=[pltpu.VMEM((bb, 1, D), F32)],
        compiler_params=_cparams("parallel", "arbitrary"),
        name="rwkv_proj",
    )(*ins)


def _wkv_kernel(has_s0, first, C, *refs):
    r_ref, w_ref, k_ref, v_ref, kk_ref, a_ref = refs[:6]
    s0_ref = refs[6] if has_s0 else None
    o_ref, so_ref, s_scr = refs[-3:]
    groups = r_ref.shape[0]
    R = WKV_ROWS
    N = RWKV_HEAD
    NP = RWKV_HEADS // 2
    nseq = R // C
    t = pl.program_id(1)
    so_cur = so_ref.at[0] if first else so_ref

    @pl.when(t == 0)
    def _():
        if has_s0:
            for s in range(groups * nseq):
                for p in range(NP):
                    s_scr[s, p] = jnp.concatenate([s0_ref[s, 2 * p], s0_ref[s, 2 * p + 1]], axis=1)
        else:
            s_scr[...] = jnp.zeros_like(s_scr)

    D = r_ref.shape[-1]
    GR = groups * R
    flat = lambda ref: ref[...].reshape(GR, D)
    logd = flat(w_ref)
    ri = lax.broadcasted_iota(jnp.int32, (GR, GR), 0)
    ci = lax.broadcasted_iota(jnp.int32, (GR, GR), 1)
    tril = jnp.logical_and(ri // C == ci // C, ci <= ri).astype(BF)
    cum = sum(_dot(tril, part) for part in _split3(logd))
    g_in = jnp.exp(cum)
    g_ex = jnp.exp(cum - logd)
    g_inv = jnp.exp(-cum)
    kk = flat(kk_ref)
    at = -(kk * g_ex)
    rt = flat(r_ref) * g_in
    bt = kk * flat(a_ref) * g_inv
    kt = flat(k_ref) * g_inv
    v = flat(v_ref)

    def lane_masks(width):
        rr = lax.broadcasted_iota(jnp.int32, (R, width), 0)
        cc = lax.broadcasted_iota(jnp.int32, (R, width), 1) % N
        same = rr // C == cc // C
        return jnp.logical_and(same, cc < rr), jnp.logical_and(same, cc <= rr)

    strict2, _ = lane_masks(2 * N)
    _, incl4 = lane_masks(4 * N)

    def bdiag(m):
        lo = lax.broadcasted_iota(jnp.int32, m.shape, 1) < N
        zero = jnp.zeros_like(m)
        return jnp.concatenate([jnp.where(lo, m, zero), jnp.where(lo, zero, m)], axis=0)

    units = [(gi, p) for gi in range(groups) for p in range(NP)]
    seqs = range(nseq)
    tile = lambda z, u: z[u[0] * R:(u[0] + 1) * R, 2 * N * u[1]:2 * N * (u[1] + 1)]
    rows = [slice(C * s, C * (s + 1)) for s in seqs]
    vp = {u: tile(v, u) for u in units}
    vpb = {u: vp[u].astype(BF) for u in units}
    btp = {u: tile(bt, u) for u in units}
    ktp = {u: tile(kt, u) for u in units}
    xa = {u: tile(at, u) for u in units}
    xr = {u: tile(rt, u) for u in units}
    S0 = {(u, s): s_scr[u[0] * nseq + s, u[1]] for u in units for s in seqs}

    xp = {u: jnp.concatenate([xa[u], xr[u]], axis=0).astype(BF) for u in units}
    bmat = {u: jnp.concatenate([bdiag(btp[u].astype(BF)), bdiag(ktp[u].astype(BF))], axis=0)
            for u in units}
    G = {u: lax.dot_general(xp[u], bmat[u], NT_DIMS, preferred_element_type=F32) for u in units}
    Z = {(u, s): lax.dot_general(
        xp[u] if nseq == 1 else
        jnp.concatenate([xa[u][rows[s]], xr[u][rows[s]]], axis=0).astype(BF),
        bdiag(S0[u, s].astype(BF)), NT_DIMS, preferred_element_type=F32) for u in units for s in seqs}
    za = {u: Z[u, 0][:C] if nseq == 1 else jnp.concatenate([Z[u, s][:C] for s in seqs], axis=0)
          for u in units}
    zr = {u: Z[u, 0][C:] if nseq == 1 else jnp.concatenate([Z[u, s][C:] for s in seqs], axis=0)
          for u in units}
    A = {u: jnp.where(strict2, G[u][:R, :2 * N], 0.0) for u in units}
    aak = {u: jnp.where(strict2, G[u][:R, 2 * N:], 0.0).astype(BF) for u in units}
    T2 = {u: jnp.where(incl4, G[u][R:], 0.0).astype(BF) for u in units}
    W = {u: za[u] + _dot(aak[u], bdiag(vpb[u])) for u in units}
    nsteps = max(1, int(math.log2(C)))
    for step in range(nsteps):
        Ab = {u: A[u].astype(BF) for u in units}
        if step + 1 < nsteps:
            res = {u: _dot(Ab[u], jnp.concatenate([bdiag(W[u].astype(BF)), bdiag(Ab[u])], axis=1))
                   for u in units}
            W = {u: W[u] + res[u][:, :2 * N] for u in units}
            A = {u: res[u][:, 2 * N:] for u in units}
        else:
            W = {u: W[u] + _dot(Ab[u], bdiag(W[u].astype(BF))) for u in units}
    wb = {u: W[u].astype(BF) for u in units}
    for u in units:
        uv = jnp.concatenate([bdiag(wb[u]), bdiag(vpb[u])], axis=0)
        o_ref[u[0], :, 2 * N * u[1]:2 * N * (u[1] + 1)] = zr[u] + _dot(T2[u], uv)
    lo = lax.broadcasted_iota(jnp.int32, (N, 2 * N), 1) < N
    for u in units:
        for s in seqs:
            uvs = jnp.concatenate([W[u][rows[s]], vp[u][rows[s]]], axis=0).astype(BF)
            ys = jnp.concatenate([btp[u][rows[s]], ktp[u][rows[s]]], axis=0).astype(BF)
            full = lax.dot_general(uvs, ys, TN_DIMS, preferred_element_type=F32)
            upd = jnp.where(lo, full[:N], full[N:])
            last = u[0] * R + C * (s + 1) - 1
            g_end = g_in[last:last + 1, 2 * N * u[1]:2 * N * (u[1] + 1)]
            s_scr[u[0] * nseq + s, u[1]] = (S0[u, s] + upd) * g_end

    @pl.when(t == pl.num_programs(1) - 1)
    def _():
        for s in range(groups * nseq):
            for p in range(NP):
                so_cur[s, 2 * p] = s_scr[s, p][:, :N]
                so_cur[s, 2 * p + 1] = s_scr[s, p][:, N:]
        if first:
            for lyr in range(1, so_ref.shape[0]):
                so_ref[lyr] = jnp.zeros(so_ref.shape[1:], F32)


def _wkv(r, logd, k, v, kk, a, s0, layer, nlayers, s_prev, C, groups):
    B, T, D = r.shape
    has_s0 = s0 is not None
    first = s_prev is None
    R = WKV_ROWS
    nseq = R // C
    nt = T // C
    assert (nseq == 1 or nt == 1) and T % C == 0
    nbk = B * T // (R * nt)
    assert nbk % groups == 0
    gs = groups * nseq
    blk = pl.BlockSpec((groups, None, R, D), lambda b, t: (b, t, 0, 0))
    sdims = (RWKV_HEADS, RWKV_HEAD, RWKV_HEAD)
    sblk = pl.BlockSpec((None, gs) + sdims, lambda b, t: (layer, b, 0, 0, 0))
    oblk = pl.BlockSpec((nlayers, gs) + sdims, lambda b, t: (0, b, 0, 0, 0)) if first else sblk
    ins = [x.reshape(nbk, nt, R, D) for x in (r, logd, k, v, kk, a)]
    in_specs = [blk] * 6
    if has_s0:
        ins.append(s0[layer])
        in_specs.append(pl.BlockSpec((gs,) + sdims, lambda b, t: (b, 0, 0, 0)))
    aliases = {}
    if not first:
        aliases = {len(ins): 1}
        ins.append(s_prev)
        in_specs.append(pl.BlockSpec(memory_space=pl.ANY))
    o, s_new = pl.pallas_call(
        functools.partial(_wkv_kernel, has_s0, first, C),
        grid=(nbk // groups, nt),
        in_specs=in_specs,
        out_specs=[blk, oblk],
        out_shape=[jax.ShapeDtypeStruct((nbk, nt, R, D), F32),
                   jax.ShapeDtypeStruct((nlayers, B) + sdims, F32)],
        scratch_shapes=[pltpu.VMEM((gs, RWKV_HEADS // 2, RWKV_HEAD, 2 * RWKV_HEAD), F32)],
        input_output_aliases=aliases,
        compiler_params=_cparams("parallel", "arbitrary"),
        name="wkv",
    )(*ins)
    return o.reshape(B, T, D), s_new


def _rwkv_out_kernel(o_ref, r_ref, k_ref, v_ref, gate_ref, x_ref, vec_ref, m_ref, wo_ref,
                     g_ref, beta_ref, y_ref):
    r_k, ln_w, ln_b = (vec_ref[c:c + 1, :] for c in (V_RK, V_LNW, V_LNB))
    o = o_ref[...]
    inv_n = 1.0 / RWKV_HEAD
    mu = _head_sum(o, m_ref) * inv_n
    oc = o - mu
    var = _head_sum(oc * oc, m_ref) * inv_n
    on = oc * lax.rsqrt(var + GN_EPS) * ln_w + ln_b
    v = v_ref[...]
    bonus = _head_sum(r_ref[...] * k_ref[...] * r_k, m_ref) * v
    y = _dot(((on + bonus) * gate_ref[...]).astype(BF), wo_ref[...])
    y_ref[...] = _layer_norm(ALPHA * x_ref[...] + y, g_ref[0:1, :], beta_ref[0:1, :])


def _rwkv_out(o, r, k, v, gate, x2, P, j, depth, tm=256):
    n = o.shape[0]
    tm = min(tm, n)
    rows = pl.BlockSpec((tm, D_MODEL), lambda i: (i, 0))
    vecs, w_o, ln_g, ln_b = P['rwkv_vecs'], P['rwkv_w_o'], P['ln_g'], P['ln_b']
    return pl.pallas_call(
        _rwkv_out_kernel,
        grid=(n // tm,),
        in_specs=[rows] * 6 + [_layer(vecs.shape, j), _full(P['head_ones'].shape), _layer(w_o.shape, j),
                               _layer(ln_g.shape, depth), _layer(ln_b.shape, depth)],
        out_specs=rows,
        out_shape=jax.ShapeDtypeStruct((n, D_MODEL), F32),
        compiler_params=_cparams("parallel"),
        name="rwkv_out",
    )(o, r, k, v, gate, x2, vecs, P['head_ones'], w_o, ln_g, ln_b)


def _trunk(x, win_k, win_v, shift, wkv, P, bias):
    prompt = win_k is None
    B, T, D = x.shape
    n = B * T
    nkv = N_KV_HEADS * HEAD_DIM
    n_rwkv = DEPTH // 2
    v_first = None
    nk, nv, ns = [], [], []
    nk_all = nv_all = s_all = None
    for i in range(DEPTH):
        j = i // 2
        x2 = x.reshape(n, D)
        if i % 2 == 0:
            if prompt:
                q, k, v = _qkv_proj(x2, P['attn_w_qkv'], P['attn_b_qkv'], j)
                q, k, v = q.reshape(B, T, D), k.reshape(B, T, nkv), v.reshape(B, T, nkv)
                o = _swa_prompt(q, k, v, bias, P['attn_sinks'][j])
                nk.append(k[:, T - WINDOW:].reshape(B, WINDOW, N_KV_HEADS, HEAD_DIM))
                nv.append(v[:, T - WINDOW:].reshape(B, WINDOW, N_KV_HEADS, HEAD_DIM))
            else:
                q, kvt = _qkv_proj_t(x2, P['attn_wq'], P['attn_wkvt'], P['attn_bq'], P['attn_bkvt'], j)
                o, nk_all, nv_all = _swa_sample(q.reshape(B, T, D), kvt, win_k, win_v, j, bias,
                                                P['attn_sinks'][j], nk_all, nv_all)
            x2 = _proj_ln(o.reshape(n, D), P['attn_w_o'], P['attn_b_o'], x2, P['ln_g'], P['ln_b'], j, i)
        else:
            x_prev = jnp.zeros((B, 1, D), x.dtype) if prompt else shift[j].reshape(B, 1, D)
            bb, tt = (1, 256) if prompt else (min(32, B), T)
            r, logd, k, v, kk, a, gate = _rwkv_proj(x, x_prev, v_first, P, j, bb, tt)
            if j == 0:
                v_first = v
            C, groups = (WKV_ROWS, 2 if B % 2 == 0 else 1) if prompt else (T, 1)
            o, s_all = _wkv(r, logd, k, v, kk, a, wkv, j, n_rwkv, s_all, C, groups)
            ns.append(x[:, -1])
            x2 = _rwkv_out(o.reshape(n, D), r.reshape(n, D), k.reshape(n, D), v.reshape(n, D),
                           gate.reshape(n, D), x2, P, j, i)
        x2 = _ffn(x2, P['ffn_w_gu'], P['ffn_w_down'], P['ln_g'], P['ln_b'], i)
        x = x2.reshape(B, T, D)
    if prompt:
        nk_all, nv_all = jnp.stack(nk), jnp.stack(nv)
    else:
        nk_all = jnp.transpose(nk_all, (0, 1, 4, 2, 3))
        nv_all = jnp.transpose(nv_all, (0, 1, 4, 2, 3))
    return x, nk_all, nv_all, jnp.stack(ns), s_all


def kernel(x_prompt, x_sample, cache_win_k, cache_win_v, state_shift, state_wkv, rel_bias, ln_g, ln_b, attn_w_qkv, attn_b_qkv, attn_w_o, attn_b_o, attn_sinks, rwkv_mix, rwkv_w_rkv, rwkv_w0, rwkv_w1, rwkv_w2, rwkv_a0, rwkv_a1, rwkv_a2, rwkv_v0, rwkv_v1, rwkv_v2, rwkv_g1, rwkv_g2, rwkv_k_k, rwkv_k_a, rwkv_r_k, rwkv_ln_w, rwkv_ln_b, rwkv_w_o, ffn_w_gu, ffn_w_down):
    bf = lambda w: w.astype(BF)
    lane_head = np.arange(MXU_WIDTH) // RWKV_HEAD
    head_ones = jnp.asarray(lane_head[:, None] == lane_head[None, :], dtype=BF)
    n_rwkv = rwkv_w0.shape[0]
    nq = N_HEADS * HEAD_DIM
    v0_rows = jnp.concatenate([jnp.zeros((1, D_MODEL), F32), rwkv_v0], axis=0)
    rwkv_vecs = jnp.stack([rwkv_w0, rwkv_a0, v0_rows, rwkv_k_k, rwkv_k_a,
                           rwkv_r_k.reshape(n_rwkv, D_MODEL), rwkv_ln_w, rwkv_ln_b], axis=1)
    w_qkv = bf(attn_w_qkv)
    b_qkv = attn_b_qkv[:, None, :]
    P = dict(ln_g=ln_g, ln_b=ln_b,
             attn_w_qkv=w_qkv, attn_b_qkv=b_qkv,
             attn_wq=w_qkv[:, :, :nq], attn_wkvt=jnp.transpose(w_qkv[:, :, nq:], (0, 2, 1)),
             attn_bq=b_qkv[:, :, :nq], attn_bkvt=jnp.transpose(b_qkv[:, :, nq:], (0, 2, 1)),
             attn_w_o=bf(attn_w_o), attn_b_o=attn_b_o[:, None, :], attn_sinks=attn_sinks,
             rwkv_mix=rwkv_mix, rwkv_vecs=rwkv_vecs, rwkv_w_rkv=bf(rwkv_w_rkv),
             rwkv_w1=bf(rwkv_w1), rwkv_w2=bf(rwkv_w2), rwkv_a1=bf(rwkv_a1), rwkv_a2=bf(rwkv_a2),
             rwkv_v1=bf(rwkv_v1), rwkv_v2=bf(rwkv_v2), rwkv_g1=bf(rwkv_g1), rwkv_g2=bf(rwkv_g2),
             rwkv_w_o=bf(rwkv_w_o), ffn_w_gu=bf(ffn_w_gu), ffn_w_down=bf(ffn_w_down),
             head_ones=head_ones)

    T = x_sample.shape[1]
    L = cache_win_k.shape[2]
    qi, kc = np.arange(WINDOW)[:, None], np.arange(WINDOW)[None, :]
    d_prompt = np.where(kc <= qi, qi - kc, qi + WINDOW - kc)
    d_sample = np.arange(T)[:, None] + L - np.arange(L + T)[None, :]
    bias_prompt = _bias_table(rel_bias, d_prompt)
    bias_sample = _bias_table(rel_bias, d_sample)

    y_prompt, pk, pv, ps, pw = _trunk(x_prompt, None, None, None, None, P, bias_prompt)
    win_k = jnp.transpose(cache_win_k, (0, 1, 3, 4, 2))
    win_v = jnp.transpose(cache_win_v, (0, 1, 3, 4, 2))
    y_sample, sk, sv, ss, sw = _trunk(x_sample, win_k, win_v, state_shift, state_wkv,
                                      P, bias_sample)
    return (y_prompt, y_sample, pk, pv, ps, pw, sk, sv, ss, sw)
```

```python
import functools
import math

import numpy as np
import jax
import jax.numpy as jnp
from jax import lax
from jax.experimental import pallas as pl
from jax.experimental.pallas import tpu as pltpu

D_MODEL = 1024
DEPTH = 4
HEAD_DIM = 64
N_HEADS = 16
N_KV_HEADS = 4
GROUP = 4
WINDOW = 128
N_BUCKETS = 32
MAX_DISTANCE = 128
RWKV_HEAD = 64
RWKV_HEADS = 16
GN_EPS = 64e-5
D_FF = 2816
ALPHA = (2 * DEPTH) ** 0.25
LN_EPS = 1e-5
NEG = -1e30
MXU_WIDTH = 256
WKV_ROWS = 64

BF = jnp.bfloat16
F32 = jnp.float32
VMEM_LIMIT = 56 * 1024 * 1024

NT_DIMS = (((1,), (1,)), ((), ()))
TN_DIMS = (((0,), (0,)), ((), ()))

V_W0, V_A0, V_V0, V_KK, V_KA, V_RK, V_LNW, V_LNB = range(8)


def _cparams(*sem):
    return pltpu.CompilerParams(dimension_semantics=sem, vmem_limit_bytes=VMEM_LIMIT)


def _dot(a, b):
    return jnp.dot(a, b, preferred_element_type=F32)


def _layer_norm(z, g, b):
    mu = jnp.mean(z, axis=-1, keepdims=True)
    zc = z - mu
    var = jnp.mean(zc * zc, axis=-1, keepdims=True)
    return zc * lax.rsqrt(var + LN_EPS) * g + b


def _sigmoid(z):
    return 1.0 / (1.0 + jnp.exp(-z))


def _split2(z):
    hi = z.astype(BF)
    lo = (z - hi.astype(F32)).astype(BF)
    return hi, lo


def _split3(z):
    hi = z.astype(BF)
    r1 = z - hi.astype(F32)
    mid = r1.astype(BF)
    lo = (r1 - mid.astype(F32)).astype(BF)
    return hi, mid, lo


def _full(shape):
    n = len(shape)
    return pl.BlockSpec(shape, lambda *_: (0,) * n)


def _layer(shape, *lead):
    k = len(lead)
    rest = len(shape) - k
    return pl.BlockSpec((None,) * k + tuple(shape[k:]), lambda *_: tuple(lead) + (0,) * rest,
                        pipeline_mode=pl.Buffered(1))


def _bias_table_kernel(rb_ref, bucket_ref, valid_ref, o_ref):
    bucket = bucket_ref[...]
    valid = valid_ref[...] > 0
    for h in range(N_HEADS):
        acc = jnp.zeros(bucket.shape, F32)
        for b in range(N_BUCKETS):
            acc = jnp.where(bucket == b, rb_ref[b, h], acc)
        o_ref[h] = jnp.where(valid, acc, NEG)


def _bias_table(rel_bias, d):
    dc = np.maximum(d, 0)
    exact = N_BUCKETS // 2
    df = np.maximum(dc, 1).astype(np.float32)
    large = exact + (np.log(df / np.float32(exact)) / np.float32(math.log(MAX_DISTANCE / exact))
                     * np.float32(N_BUCKETS - exact)).astype(np.int32)
    bucket = np.where(dc < exact, dc, np.minimum(large, N_BUCKETS - 1)).astype(np.int32)
    valid = ((d >= 0) & (d < WINDOW)).astype(np.int32)
    return pl.pallas_call(
        _bias_table_kernel,
        out_shape=jax.ShapeDtypeStruct((N_HEADS,) + d.shape, F32),
        in_specs=[pl.BlockSpec(memory_space=pltpu.SMEM),
                  pl.BlockSpec(memory_space=pltpu.VMEM),
                  pl.BlockSpec(memory_space=pltpu.VMEM)],
        out_specs=pl.BlockSpec(memory_space=pltpu.VMEM),
        name="bias_table",
    )(rel_bias, jnp.asarray(bucket), jnp.asarray(valid))


def _qkv_kernel(x_ref, w_ref, b_ref, q_ref, k_ref, v_ref):
    acc = _dot(x_ref[...].astype(BF), w_ref[...]) + b_ref[...]
    nq = N_HEADS * HEAD_DIM
    nk = N_KV_HEADS * HEAD_DIM
    q_ref[...] = (acc[:, :nq] * HEAD_DIM ** -0.5).astype(q_ref.dtype)
    k_ref[...] = acc[:, nq:nq + nk]
    v_ref[...] = acc[:, nq + nk:]


def _qkv_proj(x2, w, b, layer, tm=1024):
    n = x2.shape[0]
    tm = min(tm, n)
    nq = N_HEADS * HEAD_DIM
    nk = N_KV_HEADS * HEAD_DIM
    return pl.pallas_call(
        _qkv_kernel,
        grid=(n // tm,),
        in_specs=[pl.BlockSpec((tm, D_MODEL), lambda i: (i, 0)),
                  _layer(w.shape, layer), _layer(b.shape, layer)],
        out_specs=[pl.BlockSpec((tm, nq), lambda i: (i, 0)),
                   pl.BlockSpec((tm, nk), lambda i: (i, 0)),
                   pl.BlockSpec((tm, nk), lambda i: (i, 0))],
        out_shape=[jax.ShapeDtypeStruct((n, nq), BF),
                   jax.ShapeDtypeStruct((n, nk), F32),
                   jax.ShapeDtypeStruct((n, nk), F32)],
        compiler_params=_cparams("parallel"),
        name="qkv_proj",
    )(x2, w, b)


def _swa_prompt_kernel(sink_ref, q_ref, kc_ref, kp_ref, vc_ref, vp_ref, bias_ref, o_ref):
    blk = pl.program_id(1)
    kcat = jnp.concatenate([kp_ref[0], kc_ref[0]], axis=0).astype(BF)
    vcat = jnp.concatenate([vp_ref[0], vc_ref[0]], axis=0).astype(BF)
    row = lax.broadcasted_iota(jnp.int32, (WINDOW, WINDOW), 0)
    col = lax.broadcasted_iota(jnp.int32, (WINDOW, WINDOW), 1)
    own = col <= row
    zero = jnp.zeros((WINDOW, WINDOW), BF)
    scores = []
    for h in range(N_HEADS):
        g = h // GROUP
        qh = q_ref[0, :, h * HEAD_DIM:(h + 1) * HEAD_DIM]
        kh = kcat[:, g * HEAD_DIM:(g + 1) * HEAD_DIM]
        s2 = lax.dot_general(qh, kh, NT_DIMS, preferred_element_type=F32)
        s_prev = jnp.where(blk == 0, NEG, s2[:, :WINDOW])
        scores.append(jnp.where(own, s2[:, WINDOW:], s_prev) + bias_ref[h])
    s = jnp.concatenate(scores, axis=0)
    m = jnp.max(s, axis=-1, keepdims=True)
    p = jnp.exp(s - m)
    tot = jnp.sum(p, axis=-1, keepdims=True)
    pb = p.astype(BF)
    outs = []
    for h in range(N_HEADS):
        g = h // GROUP
        hs = slice(h * WINDOW, (h + 1) * WINDOW)
        vh = vcat[:, g * HEAD_DIM:(g + 1) * HEAD_DIM]
        p2 = jnp.concatenate([jnp.where(own, zero, pb[hs]), jnp.where(own, pb[hs], zero)], axis=1)
        inv = 1.0 / (tot[hs] + jnp.exp(sink_ref[h] - m[hs]))
        outs.append(_dot(p2, vh) * inv)
    o_ref[0] = jnp.concatenate(outs, axis=-1).astype(o_ref.dtype)


def _swa_prompt(q, k, v, bias, sinks):
    B, T, _ = q.shape
    nb = T // WINDOW
    nk = N_KV_HEADS * HEAD_DIM
    cur = lambda b, i: (b, i, 0)
    prev = lambda b, i: (b, jnp.maximum(i - 1, 0), 0)
    return pl.pallas_call(
        _swa_prompt_kernel,
        grid=(B, nb),
        in_specs=[pl.BlockSpec(memory_space=pltpu.SMEM),
                  pl.BlockSpec((1, WINDOW, D_MODEL), cur),
                  pl.BlockSpec((1, WINDOW, nk), cur),
                  pl.BlockSpec((1, WINDOW, nk), prev),
                  pl.BlockSpec((1, WINDOW, nk), cur),
                  pl.BlockSpec((1, WINDOW, nk), prev),
                  _full(bias.shape)],
        out_specs=pl.BlockSpec((1, WINDOW, D_MODEL), cur),
        out_shape=jax.ShapeDtypeStruct((B, T, D_MODEL), BF),
        compiler_params=_cparams("parallel", "parallel"),
        name="swa_prompt",
    )(sinks, q, k, k, v, v, bias)


def _qkv_t_kernel(x_ref, wq_ref, wkvt_ref, bq_ref, bkvt_ref, q_ref, kvt_ref):
    xb = x_ref[...].astype(BF)
    q_ref[...] = (_dot(xb, wq_ref[...]) + bq_ref[...]) * HEAD_DIM ** -0.5
    kvt_ref[...] = lax.dot_general(wkvt_ref[...], xb, NT_DIMS, preferred_element_type=F32) + bkvt_ref[...]


def _qkv_proj_t(x2, wq, wkvt, bq, bkvt, layer, tm=256):
    n = x2.shape[0]
    tm = min(tm, n)
    nq = N_HEADS * HEAD_DIM
    nkv = 2 * N_KV_HEADS * HEAD_DIM
    return pl.pallas_call(
        _qkv_t_kernel,
        grid=(n // tm,),
        in_specs=[pl.BlockSpec((tm, D_MODEL), lambda i: (i, 0)),
                  _layer(wq.shape, layer), _layer(wkvt.shape, layer),
                  _layer(bq.shape, layer), _layer(bkvt.shape, layer)],
        out_specs=[pl.BlockSpec((tm, nq), lambda i: (i, 0)),
                   pl.BlockSpec((nkv, tm), lambda i: (0, i))],
        out_shape=[jax.ShapeDtypeStruct((n, nq), F32),
                   jax.ShapeDtypeStruct((nkv, n), F32)],
        compiler_params=_cparams("parallel"),
        name="qkv_proj_t",
    )(x2, wq, wkvt, bq, bkvt)


def _swa_sample_kernel(T, first, sink_ref, q_ref, kvt_ref, ck_ref, cv_ref, bc_ref, bn_ref, *rest):
    o_ref, nk_ref, nv_ref = rest[-3:]
    bb = q_ref.shape[0]
    L = ck_ref.shape[-1]
    HD = HEAD_DIM
    if first:
        for lyr in range(1, nk_ref.shape[0]):
            nk_ref[lyr] = jnp.zeros(nk_ref.shape[1:], F32)
            nv_ref[lyr] = jnp.zeros(nv_ref.shape[1:], F32)
        nk_out, nv_out = nk_ref.at[0], nv_ref.at[0]
    else:
        nk_out, nv_out = nk_ref, nv_ref
    units = [(b, g) for b in range(bb) for g in range(N_KV_HEADS)]
    lane = lax.broadcasted_iota(jnp.int32, (HD, L), 1)
    knew = [kvt_ref[g * HD:(g + 1) * HD, :] for g in range(N_KV_HEADS)]
    vnew = [kvt_ref[(N_KV_HEADS + g) * HD:(N_KV_HEADS + g + 1) * HD, :] for g in range(N_KV_HEADS)]

    def shift_in(old, new, b):
        moved = pltpu.roll(new, (L - T - b * T) % L, 1)
        return jnp.where(lane >= L - T, moved, pltpu.roll(old, L - T, 1))

    sc, sn = {}, {}
    for b in range(bb):
        q = q_ref[b]
        for g in range(N_KV_HEADS):
            qg = jnp.concatenate([q[:, (g * GROUP + j) * HD:(g * GROUP + j + 1) * HD]
                                  for j in range(GROUP)], axis=0).astype(BF)
            kc = ck_ref[b, g]
            nk_out[b, g] = shift_in(kc, knew[g], b)
            sc[b, g] = _dot(qg, kc.astype(BF)) + bc_ref[g]
            sn[b, g] = _dot(qg, knew[g][:, b * T:(b + 1) * T].astype(BF)) + bn_ref[g]
    sc_all = jnp.concatenate([sc[u] for u in units], axis=0)
    sn_all = jnp.concatenate([sn[u] for u in units], axis=0)
    m = jnp.maximum(jnp.max(sc_all, axis=-1, keepdims=True), jnp.max(sn_all, axis=-1, keepdims=True))
    pc_all = jnp.exp(sc_all - m)
    pn_all = jnp.exp(sn_all - m)
    tot = jnp.sum(pc_all, axis=-1, keepdims=True) + jnp.sum(pn_all, axis=-1, keepdims=True)
    pc_all = pc_all.astype(BF)
    pn_all = pn_all.astype(BF)
    gt = GROUP * T
    for b in range(bb):
        pieces = []
        for g in range(N_KV_HEADS):
            r0 = (b * N_KV_HEADS + g) * gt
            vc = cv_ref[b, g]
            nv_out[b, g] = shift_in(vc, vnew[g], b)
            og = (lax.dot_general(pc_all[r0:r0 + gt], vc.astype(BF), NT_DIMS, preferred_element_type=F32)
                  + lax.dot_general(pn_all[r0:r0 + gt], vnew[g][:, b * T:(b + 1) * T].astype(BF),
                                    NT_DIMS, preferred_element_type=F32))
            for j in range(GROUP):
                rj = slice(r0 + j * T, r0 + (j + 1) * T)
                inv = 1.0 / (tot[rj] + jnp.exp(sink_ref[g * GROUP + j] - m[rj]))
                pieces.append(og[j * T:(j + 1) * T] * inv)
        o_ref[b] = jnp.concatenate(pieces, axis=-1).astype(o_ref.dtype)


def _swa_sample(q, kvt, ck, cv, layer, bias, sinks, nk_prev, nv_prev):
    B, T, _ = q.shape
    nl, L = ck.shape[0], ck.shape[-1]
    first = nk_prev is None
    bb = L // T
    assert B % bb == 0 and L % T == 0
    bias = bias.reshape(N_KV_HEADS, GROUP * T, L + T)
    bc, bn = bias[:, :, :L], bias[:, :, L:]
    qblk = pl.BlockSpec((bb, T, D_MODEL), lambda i: (i, 0, 0))
    cblk = pl.BlockSpec((None, bb, N_KV_HEADS, HEAD_DIM, L), lambda i: (layer, i, 0, 0, 0))
    nblk = (pl.BlockSpec((nl, bb, N_KV_HEADS, HEAD_DIM, L), lambda i: (0, i, 0, 0, 0)) if first else cblk)
    ins = [sinks, q, kvt, ck, cv, bc, bn]
    in_specs = [pl.BlockSpec(memory_space=pltpu.SMEM), qblk,
                pl.BlockSpec((kvt.shape[0], bb * T), lambda i: (0, i)),
                cblk, cblk, _full(bc.shape), _full(bn.shape)]
    aliases = {}
    if not first:
        aliases = {len(ins): 1, len(ins) + 1: 2}
        ins += [nk_prev, nv_prev]
        in_specs += [pl.BlockSpec(memory_space=pl.ANY)] * 2
    return pl.pallas_call(
        functools.partial(_swa_sample_kernel, T, first),
        grid=(B // bb,),
        in_specs=in_specs,
        out_specs=[qblk, nblk, nblk],
        out_shape=[jax.ShapeDtypeStruct((B, T, D_MODEL), BF),
                   jax.ShapeDtypeStruct((nl, B, N_KV_HEADS, HEAD_DIM, L), F32),
                   jax.ShapeDtypeStruct((nl, B, N_KV_HEADS, HEAD_DIM, L), F32)],
        input_output_aliases=aliases,
        compiler_params=_cparams("parallel"),
        name="swa_sample",
    )(*ins)


def _proj_ln_kernel(which, a_ref, w_ref, b_ref, x_ref, g_ref, beta_ref, o_ref):
    y = _dot(a_ref[...].astype(BF), w_ref[...]) + b_ref[...]
    o_ref[...] = _layer_norm(ALPHA * x_ref[...] + y, g_ref[which:which + 1, :], beta_ref[which:which + 1, :])


def _proj_ln(a, w, b, x2, ln_g, ln_b, layer, depth, tm=1024):
    n, kdim = a.shape
    tm = min(tm, n)
    rows = lambda i: (i, 0)
    return pl.pallas_call(
        functools.partial(_proj_ln_kernel, 0),
        grid=(n // tm,),
        in_specs=[pl.BlockSpec((tm, kdim), rows), _layer(w.shape, layer), _layer(b.shape, layer),
                  pl.BlockSpec((tm, D_MODEL), rows), _layer(ln_g.shape, depth), _layer(ln_b.shape, depth)],
        out_specs=pl.BlockSpec((tm, D_MODEL), rows),
        out_shape=jax.ShapeDtypeStruct((n, D_MODEL), F32),
        compiler_params=_cparams("parallel"),
        name="proj_ln",
    )(a, w, b, x2, ln_g, ln_b)


def _ffn_kernel(tf, x_ref, wgu_ref, wd_ref, g_ref, beta_ref, o_ref):
    x = x_ref[...]
    xb = x.astype(BF)
    acc = None
    for c in range(0, D_FF, tf):
        gt = _dot(xb, wgu_ref[:, c:c + tf])
        ut = _dot(xb, wgu_ref[:, D_FF + c:D_FF + c + tf])
        h = (gt * _sigmoid(gt) * ut).astype(BF)
        part = _dot(h, wd_ref[c:c + tf, :])
        acc = part if acc is None else acc + part
    o_ref[...] = _layer_norm(ALPHA * x + acc, g_ref[1:2, :], beta_ref[1:2, :])


def _ffn(x2, w_gu, w_down, ln_g, ln_b, depth, tm=512, tf=1408):
    n = x2.shape[0]
    tm = min(tm, n)
    assert D_FF % tf == 0 and tf % 128 == 0
    rows = lambda i: (i, 0)
    return pl.pallas_call(
        functools.partial(_ffn_kernel, tf),
        grid=(n // tm,),
        in_specs=[pl.BlockSpec((tm, D_MODEL), rows),
                  _layer(w_gu.shape, depth), _layer(w_down.shape, depth),
                  _layer(ln_g.shape, depth), _layer(ln_b.shape, depth)],
        out_specs=pl.BlockSpec((tm, D_MODEL), rows),
        out_shape=jax.ShapeDtypeStruct((n, D_MODEL), F32),
        compiler_params=_cparams("parallel"),
        name="ffn",
    )(x2, w_gu, w_down, ln_g, ln_b)


def _head_sum(z, m_ref):
    hi, lo = _split2(z)
    m = m_ref[...]
    w = m.shape[0]
    return jnp.concatenate([_dot(hi[:, c:c + w], m) + _dot(lo[:, c:c + w], m)
                            for c in range(0, z.shape[1], w)], axis=1)


def _rwkv_proj_kernel(has_vres, *refs):
    if has_vres:
        (x_ref, xp_ref, mix_ref, vec_ref, wr_ref, wk_ref, wv_ref, w1_ref, w2_ref, a1_ref, a2_ref,
         g1_ref, g2_ref, m_ref, v1_ref, v2_ref, vf_ref,
         r_ref, w_ref, k_ref, v_ref, kk_ref, a_ref, g_ref, bonus_ref, carry_ref) = refs
    else:
        (x_ref, xp_ref, mix_ref, vec_ref, wr_ref, wk_ref, wv_ref, w1_ref, w2_ref, a1_ref, a2_ref,
         g1_ref, g2_ref, m_ref,
         r_ref, w_ref, k_ref, v_ref, kk_ref, a_ref, g_ref, bonus_ref, carry_ref) = refs
    bb, tt, D = x_ref.shape
    n = bb * tt
    t = pl.program_id(1)
    x3 = x_ref[...]
    prev = jnp.where(t == 0, xp_ref[...], carry_ref[...])
    carry_ref[...] = x3[:, tt - 1:tt, :]
    x = x3.reshape(n, D)
    prev_rows = jnp.broadcast_to(prev, (bb, tt, D)).reshape(n, D)
    row = lax.broadcasted_iota(jnp.int32, (n, D), 0)
    xs = jnp.where(row % tt == 0, prev_rows, pltpu.roll(x, 1, 0))
    xx = xs - x
    xr, xw, xk, xv, xa, xg = ((x + xx * mix_ref[c:c + 1, :]).astype(BF) for c in range(6))
    w0, a0, v0, k_k, k_a = (vec_ref[c:c + 1, :] for c in (V_W0, V_A0, V_V0, V_KK, V_KA))

    r = _dot(xr, wr_ref[...])
    k = _dot(xk, wk_ref[...])
    v = _dot(xv, wv_ref[...])
    z = w0 + _dot(jnp.tanh(_dot(xw, w1_ref[...])).astype(BF), w2_ref[...])
    logd = -math.exp(-0.5) * _sigmoid(z)
    if has_vres:
        gate = _sigmoid(v0 + _dot(_dot(xv, v1_ref[...]).astype(BF), v2_ref[...]))
        v = v + (vf_ref[...].reshape(n, D) - v) * gate
    a = _sigmoid(a0 + _dot(_dot(xa, a1_ref[...]).astype(BF), a2_ref[...]))
    g = _dot(_sigmoid(_dot(xg, g1_ref[...])).astype(BF), g2_ref[...])
    kk = k * k_k
    kk = kk / jnp.maximum(jnp.sqrt(_head_sum(kk * kk, m_ref)), 1e-12)
    k = k * (1.0 + (a - 1.0) * k_a)
    bonus = _head_sum(r * k * vec_ref[V_RK:V_RK + 1, :], m_ref) * v
    for ref, val in ((r_ref, r), (w_ref, logd), (k_ref, k), (v_ref, v), (kk_ref, kk), (a_ref, a),
                     (g_ref, g), (bonus_ref, bonus)):
        ref[...] = val.reshape(bb, tt, D)


def _rwkv_proj(x, x_prev, v_first, P, j, bb, tt):
    B, T, D = x.shape
    has_vres = j > 0
    blk = pl.BlockSpec((bb, tt, D), lambda b, t: (b, t, 0))
    w_rkv = P['rwkv_w_rkv']
    ins = [x, x_prev, P['rwkv_mix'], P['rwkv_vecs'], w_rkv, w_rkv, w_rkv,
           P['rwkv_w1'], P['rwkv_w2'], P['rwkv_a1'], P['rwkv_a2'], P['rwkv_g1'], P['rwkv_g2'],
           P['head_ones']]
    in_specs = [blk, pl.BlockSpec((bb, 1, D), lambda b, t: (b, 0, 0)),
                _layer(ins[2].shape, j), _layer(ins[3].shape, j),
                _layer(w_rkv.shape, j, 0), _layer(w_rkv.shape, j, 1), _layer(w_rkv.shape, j, 2)]
    in_specs += [_layer(a.shape, j) for a in ins[7:13]] + [_full(P['head_ones'].shape)]
    if has_vres:
        ins += [P['rwkv_v1'], P['rwkv_v2'], v_first]
        in_specs += [_layer(P['rwkv_v1'].shape, j - 1), _layer(P['rwkv_v2'].shape, j - 1), blk]
    return pl.pallas_call(
        functools.partial(_rwkv_proj_kernel, has_vres),
        grid=(B // bb, T // tt),
        in_specs=in_specs,
        out_specs=[blk] * 8,
        out_shape=[jax.ShapeDtypeStruct((B, T, D), F32)] * 8,
        scratch_shapes=[pltpu.VMEM((bb, 1, D), F32)],
        compiler_params=_cparams("parallel", "arbitrary"),
        name="rwkv_proj",
    )(*ins)


def _wkv_kernel(has_s0, first, C, *refs):
    r_ref, w_ref, k_ref, v_ref, kk_ref, a_ref = refs[:6]
    s0_ref = refs[6] if has_s0 else None
    o_ref, so_ref, s_scr = refs[-3:]
    groups = r_ref.shape[0]
    R = WKV_ROWS
    N = RWKV_HEAD
    NP = RWKV_HEADS // 2
    nseq = R // C
    t = pl.program_id(1)
    so_cur = so_ref.at[0] if first else so_ref

    @pl.when(t == 0)
    def _():
        if has_s0:
            for s in range(groups * nseq):
                for p in range(NP):
                    s_scr[s, p] = jnp.concatenate([s0_ref[s, 2 * p], s0_ref[s, 2 * p + 1]], axis=1)
        else:
            s_scr[...] = jnp.zeros_like(s_scr)

    D = r_ref.shape[-1]
    GR = groups * R
    flat = lambda ref: ref[...].reshape(GR, D)
    logd = flat(w_ref)
    ri = lax.broadcasted_iota(jnp.int32, (GR, GR), 0)
    ci = lax.broadcasted_iota(jnp.int32, (GR, GR), 1)
    tril = jnp.logical_and(ri // C == ci // C, ci <= ri).astype(BF)
    cum = sum(_dot(tril, part) for part in _split3(logd))
    g_in = jnp.exp(cum)
    g_ex = jnp.exp(cum - logd)
    g_inv = jnp.exp(-cum)
    kk = flat(kk_ref)
    at = -(kk * g_ex)
    rt = flat(r_ref) * g_in
    bt = kk * flat(a_ref) * g_inv
    kt = flat(k_ref) * g_inv
    v = flat(v_ref)

    def lane_masks(width):
        rr = lax.broadcasted_iota(jnp.int32, (R, width), 0)
        cc = lax.broadcasted_iota(jnp.int32, (R, width), 1) % N
        same = rr // C == cc // C
        return jnp.logical_and(same, cc < rr), jnp.logical_and(same, cc <= rr)

    strict2, _ = lane_masks(2 * N)
    _, incl4 = lane_masks(4 * N)

    def bdiag(m):
        lo = lax.broadcasted_iota(jnp.int32, m.shape, 1) < N
        zero = jnp.zeros_like(m)
        return jnp.concatenate([jnp.where(lo, m, zero), jnp.where(lo, zero, m)], axis=0)

    units = [(gi, p) for gi in range(groups) for p in range(NP)]
    seqs = range(nseq)
    tile = lambda z, u: z[u[0] * R:(u[0] + 1) * R, 2 * N * u[1]:2 * N * (u[1] + 1)]
    rows = [slice(C * s, C * (s + 1)) for s in seqs]
    vp = {u: tile(v, u) for u in units}
    vpb = {u: vp[u].astype(BF) for u in units}
    btp = {u: tile(bt, u) for u in units}
    ktp = {u: tile(kt, u) for u in units}
    xa = {u: tile(at, u) for u in units}
    xr = {u: tile(rt, u) for u in units}
    S0 = {(u, s): s_scr[u[0] * nseq + s, u[1]] for u in units for s in seqs}

    xp = {u: jnp.concatenate([xa[u], xr[u]], axis=0).astype(BF) for u in units}
    bmat = {u: jnp.concatenate([bdiag(btp[u].astype(BF)), bdiag(ktp[u].astype(BF))], axis=0)
            for u in units}
    G = {u: lax.dot_general(xp[u], bmat[u], NT_DIMS, preferred_element_type=F32) for u in units}
    Z = {(u, s): lax.dot_general(
        xp[u] if nseq == 1 else
        jnp.concatenate([xa[u][rows[s]], xr[u][rows[s]]], axis=0).astype(BF),
        bdiag(S0[u, s].astype(BF)), NT_DIMS, preferred_element_type=F32) for u in units for s in seqs}
    za = {u: Z[u, 0][:C] if nseq == 1 else jnp.concatenate([Z[u, s][:C] for s in seqs], axis=0)
          for u in units}
    zr = {u: Z[u, 0][C:] if nseq == 1 else jnp.concatenate([Z[u, s][C:] for s in seqs], axis=0)
          for u in units}
    A = {u: jnp.where(strict2, G[u][:R, :2 * N], 0.0) for u in units}
    aak = {u: jnp.where(strict2, G[u][:R, 2 * N:], 0.0).astype(BF) for u in units}
    T2 = {u: jnp.where(incl4, G[u][R:], 0.0).astype(BF) for u in units}
    W = {u: za[u] + _dot(aak[u], bdiag(vpb[u])) for u in units}
    nsteps = max(1, int(math.log2(C)))
    for step in range(nsteps):
        Ab = {u: A[u].astype(BF) for u in units}
        if step + 1 < nsteps:
            res = {u: _dot(Ab[u], jnp.concatenate([bdiag(W[u].astype(BF)), bdiag(Ab[u])], axis=1))
                   for u in units}
            W = {u: W[u] + res[u][:, :2 * N] for u in units}
            A = {u: res[u][:, 2 * N:] for u in units}
        else:
            W = {u: W[u] + _dot(Ab[u], bdiag(W[u].astype(BF))) for u in units}
    wb = {u: W[u].astype(BF) for u in units}
    for u in units:
        uv = jnp.concatenate([bdiag(wb[u]), bdiag(vpb[u])], axis=0)
        o_ref[u[0], :, 2 * N * u[1]:2 * N * (u[1] + 1)] = zr[u] + _dot(T2[u], uv)
    lo = lax.broadcasted_iota(jnp.int32, (N, 2 * N), 1) < N
    for u in units:
        for s in seqs:
            uvs = jnp.concatenate([W[u][rows[s]], vp[u][rows[s]]], axis=0).astype(BF)
            ys = jnp.concatenate([btp[u][rows[s]], ktp[u][rows[s]]], axis=0).astype(BF)
            full = lax.dot_general(uvs, ys, TN_DIMS, preferred_element_type=F32)
            upd = jnp.where(lo, full[:N], full[N:])
            last = u[0] * R + C * (s + 1) - 1
            g_end = g_in[last:last + 1, 2 * N * u[1]:2 * N * (u[1] + 1)]
            s_scr[u[0] * nseq + s, u[1]] = (S0[u, s] + upd) * g_end

    @pl.when(t == pl.num_programs(1) - 1)
    def _():
        for s in range(groups * nseq):
            for p in range(NP):
                so_cur[s, 2 * p] = s_scr[s, p][:, :N]
                so_cur[s, 2 * p + 1] = s_scr[s, p][:, N:]
        if first:
            for lyr in range(1, so_ref.shape[0]):
                so_ref[lyr] = jnp.zeros(so_ref.shape[1:], F32)


def _wkv(r, logd, k, v, kk, a, s0, layer, nlayers, s_prev, C, groups):
    B, T, D = r.shape
    has_s0 = s0 is not None
    first = s_prev is None
    R = WKV_ROWS
    nseq = R // C
    nt = T // C
    assert (nseq == 1 or nt == 1) and T % C == 0
    nbk = B * T // (R * nt)
    assert nbk % groups == 0
    gs = groups * nseq
    blk = pl.BlockSpec((groups, None, R, D), lambda b, t: (b, t, 0, 0))
    sdims = (RWKV_HEADS, RWKV_HEAD, RWKV_HEAD)
    sblk = pl.BlockSpec((None, gs) + sdims, lambda b, t: (layer, b, 0, 0, 0))
    oblk = pl.BlockSpec((nlayers, gs) + sdims, lambda b, t: (0, b, 0, 0, 0)) if first else sblk
    ins = [x.reshape(nbk, nt, R, D) for x in (r, logd, k, v, kk, a)]
    in_specs = [blk] * 6
    if has_s0:
        ins.append(s0[layer])
        in_specs.append(pl.BlockSpec((gs,) + sdims, lambda b, t: (b, 0, 0, 0)))
    aliases = {}
    if not first:
        aliases = {len(ins): 1}
        ins.append(s_prev)
        in_specs.append(pl.BlockSpec(memory_space=pl.ANY))
    o, s_new = pl.pallas_call(
        functools.partial(_wkv_kernel, has_s0, first, C),
        grid=(nbk // groups, nt),
        in_specs=in_specs,
        out_specs=[blk, oblk],
        out_shape=[jax.ShapeDtypeStruct((nbk, nt, R, D), F32),
                   jax.ShapeDtypeStruct((nlayers, B) + sdims, F32)],
        scratch_shapes=[pltpu.VMEM((gs, RWKV_HEADS // 2, RWKV_HEAD, 2 * RWKV_HEAD), F32)],
        input_output_aliases=aliases,
        compiler_params=_cparams("parallel", "arbitrary"),
        name="wkv",
    )(*ins)
    return o.reshape(B, T, D), s_new


def _rwkv_out_kernel(o_ref, bonus_ref, gate_ref, x_ref, vec_ref, m_ref, wo_ref, g_ref, beta_ref, y_ref):
    ln_w, ln_b = (vec_ref[c:c + 1, :] for c in (V_LNW, V_LNB))
    o = o_ref[...]
    inv_n = 1.0 / RWKV_HEAD
    mu = _head_sum(o, m_ref) * inv_n
    oc = o - mu
    var = _head_sum(oc * oc, m_ref) * inv_n
    on = oc * lax.rsqrt(var + GN_EPS) * ln_w + ln_b
    y = _dot(((on + bonus_ref[...]) * gate_ref[...]).astype(BF), wo_ref[...])
    y_ref[...] = _layer_norm(ALPHA * x_ref[...] + y, g_ref[0:1, :], beta_ref[0:1, :])


def _rwkv_out(o, bonus, gate, x2, P, j, depth, tm=256):
    n = o.shape[0]
    tm = min(tm, n)
    rows = pl.BlockSpec((tm, D_MODEL), lambda i: (i, 0))
    vecs, w_o, ln_g, ln_b = P['rwkv_vecs'], P['rwkv_w_o'], P['ln_g'], P['ln_b']
    return pl.pallas_call(
        _rwkv_out_kernel,
        grid=(n // tm,),
        in_specs=[rows] * 4 + [_layer(vecs.shape, j), _full(P['head_ones'].shape), _layer(w_o.shape, j),
                               _layer(ln_g.shape, depth), _layer(ln_b.shape, depth)],
        out_specs=rows,
        out_shape=jax.ShapeDtypeStruct((n, D_MODEL), F32),
        compiler_params=_cparams("parallel"),
        name="rwkv_out",
    )(o, bonus, gate, x2, vecs, P['head_ones'], w_o, ln_g, ln_b)


def _trunk(x, win_k, win_v, shift, wkv, P, bias):
    prompt = win_k is None
    B, T, D = x.shape
    n = B * T
    nkv = N_KV_HEADS * HEAD_DIM
    n_rwkv = DEPTH // 2
    v_first = None
    nk, nv, ns = [], [], []
    nk_all = nv_all = s_all = None
    for i in range(DEPTH):
        j = i // 2
        x2 = x.reshape(n, D)
        if i % 2 == 0:
            if prompt:
                q, k, v = _qkv_proj(x2, P['attn_w_qkv'], P['attn_b_qkv'], j)
                q, k, v = q.reshape(B, T, D), k.reshape(B, T, nkv), v.reshape(B, T, nkv)
                o = _swa_prompt(q, k, v, bias, P['attn_sinks'][j])
                nk.append(k[:, T - WINDOW:].reshape(B, WINDOW, N_KV_HEADS, HEAD_DIM))
                nv.append(v[:, T - WINDOW:].reshape(B, WINDOW, N_KV_HEADS, HEAD_DIM))
            else:
                q, kvt = _qkv_proj_t(x2, P['attn_wq'], P['attn_wkvt'], P['attn_bq'], P['attn_bkvt'], j)
                o, nk_all, nv_all = _swa_sample(q.reshape(B, T, D), kvt, win_k, win_v, j, bias,
                                                P['attn_sinks'][j], nk_all, nv_all)
            x2 = _proj_ln(o.reshape(n, D), P['attn_w_o'], P['attn_b_o'], x2, P['ln_g'], P['ln_b'], j, i)
        else:
            x_prev = jnp.zeros((B, 1, D), x.dtype) if prompt else shift[j].reshape(B, 1, D)
            bb, tt = (1, 256) if prompt else (min(32, B), T)
            r, logd, k, v, kk, a, gate, bonus = _rwkv_proj(x, x_prev, v_first, P, j, bb, tt)
            if j == 0:
                v_first = v
            C, groups = (WKV_ROWS, math.gcd(B, 4)) if prompt else (T, 1)
            o, s_all = _wkv(r, logd, k, v, kk, a, wkv, j, n_rwkv, s_all, C, groups)
            ns.append(x[:, -1])
            x2 = _rwkv_out(o.reshape(n, D), bonus.reshape(n, D), gate.reshape(n, D), x2, P, j, i)
        x2 = _ffn(x2, P['ffn_w_gu'], P['ffn_w_down'], P['ln_g'], P['ln_b'], i)
        x = x2.reshape(B, T, D)
    if prompt:
        nk_all, nv_all = jnp.stack(nk), jnp.stack(nv)
    else:
        nk_all = jnp.transpose(nk_all, (0, 1, 4, 2, 3))
        nv_all = jnp.transpose(nv_all, (0, 1, 4, 2, 3))
    return x, nk_all, nv_all, jnp.stack(ns), s_all


def kernel(x_prompt, x_sample, cache_win_k, cache_win_v, state_shift, state_wkv, rel_bias, ln_g, ln_b, attn_w_qkv, attn_b_qkv, attn_w_o, attn_b_o, attn_sinks, rwkv_mix, rwkv_w_rkv, rwkv_w0, rwkv_w1, rwkv_w2, rwkv_a0, rwkv_a1, rwkv_a2, rwkv_v0, rwkv_v1, rwkv_v2, rwkv_g1, rwkv_g2, rwkv_k_k, rwkv_k_a, rwkv_r_k, rwkv_ln_w, rwkv_ln_b, rwkv_w_o, ffn_w_gu, ffn_w_down):
    bf = lambda w: w.astype(BF)
    lane_head = np.arange(MXU_WIDTH) // RWKV_HEAD
    head_ones = jnp.asarray(lane_head[:, None] == lane_head[None, :], dtype=BF)
    n_rwkv = rwkv_w0.shape[0]
    nq = N_HEADS * HEAD_DIM
    v0_rows = jnp.concatenate([jnp.zeros((1, D_MODEL), F32), rwkv_v0], axis=0)
    rwkv_vecs = jnp.stack([rwkv_w0, rwkv_a0, v0_rows, rwkv_k_k, rwkv_k_a,
                           rwkv_r_k.reshape(n_rwkv, D_MODEL), rwkv_ln_w, rwkv_ln_b], axis=1)
    w_qkv = bf(attn_w_qkv)
    b_qkv = attn_b_qkv[:, None, :]
    P = dict(ln_g=ln_g, ln_b=ln_b,
             attn_w_qkv=w_qkv, attn_b_qkv=b_qkv,
             attn_wq=w_qkv[:, :, :nq], attn_wkvt=jnp.transpose(w_qkv[:, :, nq:], (0, 2, 1)),
             attn_bq=b_qkv[:, :, :nq], attn_bkvt=jnp.transpose(b_qkv[:, :, nq:], (0, 2, 1)),
             attn_w_o=bf(attn_w_o), attn_b_o=attn_b_o[:, None, :], attn_sinks=attn_sinks,
             rwkv_mix=rwkv_mix, rwkv_vecs=rwkv_vecs, rwkv_w_rkv=bf(rwkv_w_rkv),
             rwkv_w1=bf(rwkv_w1), rwkv_w2=bf(rwkv_w2), rwkv_a1=bf(rwkv_a1), rwkv_a2=bf(rwkv_a2),
             rwkv_v1=bf(rwkv_v1), rwkv_v2=bf(rwkv_v2), rwkv_g1=bf(rwkv_g1), rwkv_g2=bf(rwkv_g2),
             rwkv_w_o=bf(rwkv_w_o), ffn_w_gu=bf(ffn_w_gu), ffn_w_down=bf(ffn_w_down),
             head_ones=head_ones)

    T = x_sample.shape[1]
    L = cache_win_k.shape[2]
    qi, kc = np.arange(WINDOW)[:, None], np.arange(WINDOW)[None, :]
    d_prompt = np.where(kc <= qi, qi - kc, qi + WINDOW - kc)
    d_sample = np.arange(T)[:, None] + L - np.arange(L + T)[None, :]
    bias_prompt = _bias_table(rel_bias, d_prompt)
    bias_sample = _bias_table(rel_bias, d_sample)

    y_prompt, pk, pv, ps, pw = _trunk(x_prompt, None, None, None, None, P, bias_prompt)
    win_k = jnp.transpose(cache_win_k, (0, 1, 3, 4, 2))
    win_v = jnp.transpose(cache_win_v, (0, 1, 3, 4, 2))
    y_sample, sk, sv, ss, sw = _trunk(x_sample, win_k, win_v, state_shift, state_wkv,
                                      P, bias_sample)
    return (y_prompt, y_sample, pk, pv, ps, pw, sk, sv, ss, sw)
```

```python
import functools
import math

import numpy as np
import jax
import jax.numpy as jnp
from jax import lax
from jax.experimental import pallas as pl
from jax.experimental.pallas import tpu as pltpu

D_MODEL = 1024
DEPTH = 4
HEAD_DIM = 64
N_HEADS = 16
N_KV_HEADS = 4
GROUP = 4
WINDOW = 128
N_BUCKETS = 32
MAX_DISTANCE = 128
RWKV_HEAD = 64
RWKV_HEADS = 16
GN_EPS = 64e-5
D_FF = 2816
ALPHA = (2 * DEPTH) ** 0.25
LN_EPS = 1e-5
NEG = -1e30
MXU_WIDTH = 256
WKV_ROWS = 64

BF = jnp.bfloat16
F32 = jnp.float32
VMEM_LIMIT = 56 * 1024 * 1024

NT_DIMS = (((1,), (1,)), ((), ()))
TN_DIMS = (((0,), (0,)), ((), ()))

V_W0, V_A0, V_V0, V_KK, V_KA, V_RK, V_LNW, V_LNB = range(8)


def _cparams(*sem):
    return pltpu.CompilerParams(dimension_semantics=sem, vmem_limit_bytes=VMEM_LIMIT)


def _dot(a, b):
    return jnp.dot(a, b, preferred_element_type=F32)


def _layer_norm(z, g, b):
    mu = jnp.mean(z, axis=-1, keepdims=True)
    zc = z - mu
    var = jnp.mean(zc * zc, axis=-1, keepdims=True)
    return zc * lax.rsqrt(var + LN_EPS) * g + b


def _sigmoid(z):
    return 1.0 / (1.0 + jnp.exp(-z))


def _split2(z):
    hi = z.astype(BF)
    lo = (z - hi.astype(F32)).astype(BF)
    return hi, lo


def _split3(z):
    hi = z.astype(BF)
    r1 = z - hi.astype(F32)
    mid = r1.astype(BF)
    lo = (r1 - mid.astype(F32)).astype(BF)
    return hi, mid, lo


def _full(shape):
    n = len(shape)
    return pl.BlockSpec(shape, lambda *_: (0,) * n)


def _layer(shape, *lead):
    k = len(lead)
    rest = len(shape) - k
    return pl.BlockSpec((None,) * k + tuple(shape[k:]), lambda *_: tuple(lead) + (0,) * rest,
                        pipeline_mode=pl.Buffered(1))


def _bias_table_kernel(rb_ref, bucket_ref, valid_ref, o_ref):
    bucket = bucket_ref[...]
    valid = valid_ref[...] > 0
    for h in range(N_HEADS):
        acc = jnp.zeros(bucket.shape, F32)
        for b in range(N_BUCKETS):
            acc = jnp.where(bucket == b, rb_ref[b, h], acc)
        o_ref[h] = jnp.where(valid, acc, NEG)


def _bias_table(rel_bias, d):
    dc = np.maximum(d, 0)
    exact = N_BUCKETS // 2
    df = np.maximum(dc, 1).astype(np.float32)
    large = exact + (np.log(df / np.float32(exact)) / np.float32(math.log(MAX_DISTANCE / exact))
                     * np.float32(N_BUCKETS - exact)).astype(np.int32)
    bucket = np.where(dc < exact, dc, np.minimum(large, N_BUCKETS - 1)).astype(np.int32)
    valid = ((d >= 0) & (d < WINDOW)).astype(np.int32)
    return pl.pallas_call(
        _bias_table_kernel,
        out_shape=jax.ShapeDtypeStruct((N_HEADS,) + d.shape, F32),
        in_specs=[pl.BlockSpec(memory_space=pltpu.SMEM),
                  pl.BlockSpec(memory_space=pltpu.VMEM),
                  pl.BlockSpec(memory_space=pltpu.VMEM)],
        out_specs=pl.BlockSpec(memory_space=pltpu.VMEM),
        name="bias_table",
    )(rel_bias, jnp.asarray(bucket), jnp.asarray(valid))


def _qkv_kernel(x_ref, w_ref, b_ref, q_ref, k_ref, v_ref):
    acc = _dot(x_ref[...].astype(BF), w_ref[...]) + b_ref[...]
    nq = N_HEADS * HEAD_DIM
    nk = N_KV_HEADS * HEAD_DIM
    q_ref[...] = (acc[:, :nq] * HEAD_DIM ** -0.5).astype(q_ref.dtype)
    k_ref[...] = acc[:, nq:nq + nk]
    v_ref[...] = acc[:, nq + nk:]


def _qkv_proj(x2, w, b, layer, tm=1024):
    n = x2.shape[0]
    tm = min(tm, n)
    nq = N_HEADS * HEAD_DIM
    nk = N_KV_HEADS * HEAD_DIM
    return pl.pallas_call(
        _qkv_kernel,
        grid=(n // tm,),
        in_specs=[pl.BlockSpec((tm, D_MODEL), lambda i: (i, 0)),
                  _layer(w.shape, layer), _layer(b.shape, layer)],
        out_specs=[pl.BlockSpec((tm, nq), lambda i: (i, 0)),
                   pl.BlockSpec((tm, nk), lambda i: (i, 0)),
                   pl.BlockSpec((tm, nk), lambda i: (i, 0))],
        out_shape=[jax.ShapeDtypeStruct((n, nq), BF),
                   jax.ShapeDtypeStruct((n, nk), F32),
                   jax.ShapeDtypeStruct((n, nk), F32)],
        compiler_params=_cparams("parallel"),
        name="qkv_proj",
    )(x2, w, b)


def _swa_prompt_kernel(sink_ref, q_ref, kc_ref, kp_ref, vc_ref, vp_ref, bias_ref, o_ref):
    blk = pl.program_id(1)
    kcat = jnp.concatenate([kp_ref[0], kc_ref[0]], axis=0).astype(BF)
    vcat = jnp.concatenate([vp_ref[0], vc_ref[0]], axis=0).astype(BF)
    row = lax.broadcasted_iota(jnp.int32, (WINDOW, WINDOW), 0)
    col = lax.broadcasted_iota(jnp.int32, (WINDOW, WINDOW), 1)
    own = col <= row
    zero = jnp.zeros((WINDOW, WINDOW), BF)
    scores = []
    for h in range(N_HEADS):
        g = h // GROUP
        qh = q_ref[0, :, h * HEAD_DIM:(h + 1) * HEAD_DIM]
        kh = kcat[:, g * HEAD_DIM:(g + 1) * HEAD_DIM]
        s2 = lax.dot_general(qh, kh, NT_DIMS, preferred_element_type=F32)
        s_prev = jnp.where(blk == 0, NEG, s2[:, :WINDOW])
        scores.append(jnp.where(own, s2[:, WINDOW:], s_prev) + bias_ref[h])
    m, pb = [], []
    for g in range(N_KV_HEADS):
        s = jnp.concatenate(scores[g * GROUP:(g + 1) * GROUP], axis=0)
        mg = jnp.max(s, axis=-1, keepdims=True)
        m.append(mg)
        pb.append(jnp.exp(s - mg).astype(BF))
    ones = jnp.ones((2 * WINDOW, HEAD_DIM), BF)
    lo = lax.broadcasted_iota(jnp.int32, (WINDOW, 2 * HEAD_DIM), 1) < HEAD_DIM
    for g in range(N_KV_HEADS):
        vh = vcat[:, g * HEAD_DIM:(g + 1) * HEAD_DIM]
        vext = jnp.concatenate([vh, ones, ones, vh], axis=1)
        res, esink = [], []
        for j in range(GROUP):
            hs = slice(j * WINDOW, (j + 1) * WINDOW)
            ph = pb[g][hs]
            p2 = jnp.concatenate([jnp.where(own, zero, ph), jnp.where(own, ph, zero)], axis=1)
            res.append(_dot(p2, vext))
            esink.append(jnp.exp(sink_ref[g * GROUP + j] - m[g][hs]))
        for j in range(0, GROUP, 2):
            o_pair = jnp.where(lo, res[j][:, :2 * HEAD_DIM], res[j + 1][:, 2 * HEAD_DIM:])
            t_pair = jnp.where(lo, res[j][:, 2 * HEAD_DIM:], res[j + 1][:, :2 * HEAD_DIM])
            den = t_pair + jnp.where(lo, esink[j], esink[j + 1])
            c0 = (g * GROUP + j) * HEAD_DIM
            o_ref[0, :, c0:c0 + 2 * HEAD_DIM] = (o_pair / den).astype(o_ref.dtype)


def _swa_prompt(q, k, v, bias, sinks):
    B, T, _ = q.shape
    nb = T // WINDOW
    nk = N_KV_HEADS * HEAD_DIM
    cur = lambda b, i: (b, i, 0)
    prev = lambda b, i: (b, jnp.maximum(i - 1, 0), 0)
    return pl.pallas_call(
        _swa_prompt_kernel,
        grid=(B, nb),
        in_specs=[pl.BlockSpec(memory_space=pltpu.SMEM),
                  pl.BlockSpec((1, WINDOW, D_MODEL), cur),
                  pl.BlockSpec((1, WINDOW, nk), cur),
                  pl.BlockSpec((1, WINDOW, nk), prev),
                  pl.BlockSpec((1, WINDOW, nk), cur),
                  pl.BlockSpec((1, WINDOW, nk), prev),
                  _full(bias.shape)],
        out_specs=pl.BlockSpec((1, WINDOW, D_MODEL), cur),
        out_shape=jax.ShapeDtypeStruct((B, T, D_MODEL), BF),
        compiler_params=_cparams("parallel", "parallel"),
        name="swa_prompt",
    )(sinks, q, k, k, v, v, bias)


def _qkv_t_kernel(x_ref, wq_ref, wkvt_ref, bq_ref, bkvt_ref, q_ref, kvt_ref):
    xb = x_ref[...].astype(BF)
    q_ref[...] = (_dot(xb, wq_ref[...]) + bq_ref[...]) * HEAD_DIM ** -0.5
    kvt_ref[...] = lax.dot_general(wkvt_ref[...], xb, NT_DIMS, preferred_element_type=F32) + bkvt_ref[...]


def _qkv_proj_t(x2, wq, wkvt, bq, bkvt, layer, tm=256):
    n = x2.shape[0]
    tm = min(tm, n)
    nq = N_HEADS * HEAD_DIM
    nkv = 2 * N_KV_HEADS * HEAD_DIM
    return pl.pallas_call(
        _qkv_t_kernel,
        grid=(n // tm,),
        in_specs=[pl.BlockSpec((tm, D_MODEL), lambda i: (i, 0)),
                  _layer(wq.shape, layer), _layer(wkvt.shape, layer),
                  _layer(bq.shape, layer), _layer(bkvt.shape, layer)],
        out_specs=[pl.BlockSpec((tm, nq), lambda i: (i, 0)),
                   pl.BlockSpec((nkv, tm), lambda i: (0, i))],
        out_shape=[jax.ShapeDtypeStruct((n, nq), F32),
                   jax.ShapeDtypeStruct((nkv, n), F32)],
        compiler_params=_cparams("parallel"),
        name="qkv_proj_t",
    )(x2, wq, wkvt, bq, bkvt)


def _swa_sample_kernel(T, first, sink_ref, q_ref, kvt_ref, ck_ref, cv_ref, bc_ref, bn_ref, *rest):
    o_ref, nk_ref, nv_ref = rest[-3:]
    bb = q_ref.shape[0]
    L = ck_ref.shape[-1]
    HD = HEAD_DIM
    if first:
        for lyr in range(1, nk_ref.shape[0]):
            nk_ref[lyr] = jnp.zeros(nk_ref.shape[1:], F32)
            nv_ref[lyr] = jnp.zeros(nv_ref.shape[1:], F32)
        nk_out, nv_out = nk_ref.at[0], nv_ref.at[0]
    else:
        nk_out, nv_out = nk_ref, nv_ref
    units = [(b, g) for b in range(bb) for g in range(N_KV_HEADS)]
    lane = lax.broadcasted_iota(jnp.int32, (HD, L), 1)
    knew = [kvt_ref[g * HD:(g + 1) * HD, :] for g in range(N_KV_HEADS)]
    vnew = [kvt_ref[(N_KV_HEADS + g) * HD:(N_KV_HEADS + g + 1) * HD, :] for g in range(N_KV_HEADS)]

    def shift_in(old, new, b):
        moved = pltpu.roll(new, (L - T - b * T) % L, 1)
        return jnp.where(lane >= L - T, moved, pltpu.roll(old, L - T, 1))

    sc, sn = {}, {}
    for b in range(bb):
        q = q_ref[b]
        for g in range(N_KV_HEADS):
            qg = jnp.concatenate([q[:, (g * GROUP + j) * HD:(g * GROUP + j + 1) * HD]
                                  for j in range(GROUP)], axis=0).astype(BF)
            kc = ck_ref[b, g]
            nk_out[b, g] = shift_in(kc, knew[g], b)
            sc[b, g] = _dot(qg, kc.astype(BF)) + bc_ref[g]
            sn[b, g] = _dot(qg, knew[g][:, b * T:(b + 1) * T].astype(BF)) + bn_ref[g]
    sc_all = jnp.concatenate([sc[u] for u in units], axis=0)
    sn_all = jnp.concatenate([sn[u] for u in units], axis=0)
    m = jnp.maximum(jnp.max(sc_all, axis=-1, keepdims=True), jnp.max(sn_all, axis=-1, keepdims=True))
    pc_all = jnp.exp(sc_all - m)
    pn_all = jnp.exp(sn_all - m)
    tot = jnp.sum(pc_all, axis=-1, keepdims=True) + jnp.sum(pn_all, axis=-1, keepdims=True)
    pc_all = pc_all.astype(BF)
    pn_all = pn_all.astype(BF)
    gt = GROUP * T
    for b in range(bb):
        pieces = []
        for g in range(N_KV_HEADS):
            r0 = (b * N_KV_HEADS + g) * gt
            vc = cv_ref[b, g]
            nv_out[b, g] = shift_in(vc, vnew[g], b)
            og = (lax.dot_general(pc_all[r0:r0 + gt], vc.astype(BF), NT_DIMS, preferred_element_type=F32)
                  + lax.dot_general(pn_all[r0:r0 + gt], vnew[g][:, b * T:(b + 1) * T].astype(BF),
                                    NT_DIMS, preferred_element_type=F32))
            for j in range(GROUP):
                rj = slice(r0 + j * T, r0 + (j + 1) * T)
                inv = 1.0 / (tot[rj] + jnp.exp(sink_ref[g * GROUP + j] - m[rj]))
                pieces.append(og[j * T:(j + 1) * T] * inv)
        o_ref[b] = jnp.concatenate(pieces, axis=-1).astype(o_ref.dtype)


def _swa_sample(q, kvt, ck, cv, layer, bias, sinks, nk_prev, nv_prev):
    B, T, _ = q.shape
    nl, L = ck.shape[0], ck.shape[-1]
    first = nk_prev is None
    bb = L // T
    assert B % bb == 0 and L % T == 0
    bias = bias.reshape(N_KV_HEADS, GROUP * T, L + T)
    bc, bn = bias[:, :, :L], bias[:, :, L:]
    qblk = pl.BlockSpec((bb, T, D_MODEL), lambda i: (i, 0, 0))
    cblk = pl.BlockSpec((None, bb, N_KV_HEADS, HEAD_DIM, L), lambda i: (layer, i, 0, 0, 0))
    nblk = (pl.BlockSpec((nl, bb, N_KV_HEADS, HEAD_DIM, L), lambda i: (0, i, 0, 0, 0)) if first else cblk)
    ins = [sinks, q, kvt, ck, cv, bc, bn]
    in_specs = [pl.BlockSpec(memory_space=pltpu.SMEM), qblk,
                pl.BlockSpec((kvt.shape[0], bb * T), lambda i: (0, i)),
                cblk, cblk, _full(bc.shape), _full(bn.shape)]
    aliases = {}
    if not first:
        aliases = {len(ins): 1, len(ins) + 1: 2}
        ins += [nk_prev, nv_prev]
        in_specs += [pl.BlockSpec(memory_space=pl.ANY)] * 2
    return pl.pallas_call(
        functools.partial(_swa_sample_kernel, T, first),
        grid=(B // bb,),
        in_specs=in_specs,
        out_specs=[qblk, nblk, nblk],
        out_shape=[jax.ShapeDtypeStruct((B, T, D_MODEL), BF),
                   jax.ShapeDtypeStruct((nl, B, N_KV_HEADS, HEAD_DIM, L), F32),
                   jax.ShapeDtypeStruct((nl, B, N_KV_HEADS, HEAD_DIM, L), F32)],
        input_output_aliases=aliases,
        compiler_params=_cparams("parallel"),
        name="swa_sample",
    )(*ins)


def _proj_ln_kernel(which, a_ref, w_ref, b_ref, x_ref, g_ref, beta_ref, o_ref):
    y = _dot(a_ref[...].astype(BF), w_ref[...]) + b_ref[...]
    o_ref[...] = _layer_norm(ALPHA * x_ref[...] + y, g_ref[which:which + 1, :], beta_ref[which:which + 1, :])


def _proj_ln(a, w, b, x2, ln_g, ln_b, layer, depth, tm=1024):
    n, kdim = a.shape
    tm = min(tm, n)
    rows = lambda i: (i, 0)
    return pl.pallas_call(
        functools.partial(_proj_ln_kernel, 0),
        grid=(n // tm,),
        in_specs=[pl.BlockSpec((tm, kdim), rows), _layer(w.shape, layer), _layer(b.shape, layer),
                  pl.BlockSpec((tm, D_MODEL), rows), _layer(ln_g.shape, depth), _layer(ln_b.shape, depth)],
        out_specs=pl.BlockSpec((tm, D_MODEL), rows),
        out_shape=jax.ShapeDtypeStruct((n, D_MODEL), F32),
        compiler_params=_cparams("parallel"),
        name="proj_ln",
    )(a, w, b, x2, ln_g, ln_b)


def _ffn_kernel(tf, x_ref, wgu_ref, wd_ref, g_ref, beta_ref, o_ref):
    x = x_ref[...]
    xb = x.astype(BF)
    acc = None
    for c in range(0, D_FF, tf):
        gt = _dot(xb, wgu_ref[:, c:c + tf])
        ut = _dot(xb, wgu_ref[:, D_FF + c:D_FF + c + tf])
        h = (gt * _sigmoid(gt) * ut).astype(BF)
        part = _dot(h, wd_ref[c:c + tf, :])
        acc = part if acc is None else acc + part
    o_ref[...] = _layer_norm(ALPHA * x + acc, g_ref[1:2, :], beta_ref[1:2, :])


def _ffn(x2, w_gu, w_down, ln_g, ln_b, depth, tm=512, tf=1408):
    n = x2.shape[0]
    tm = min(tm, n)
    assert D_FF % tf == 0 and tf % 128 == 0
    rows = lambda i: (i, 0)
    return pl.pallas_call(
        functools.partial(_ffn_kernel, tf),
        grid=(n // tm,),
        in_specs=[pl.BlockSpec((tm, D_MODEL), rows),
                  _layer(w_gu.shape, depth), _layer(w_down.shape, depth),
                  _layer(ln_g.shape, depth), _layer(ln_b.shape, depth)],
        out_specs=pl.BlockSpec((tm, D_MODEL), rows),
        out_shape=jax.ShapeDtypeStruct((n, D_MODEL), F32),
        compiler_params=_cparams("parallel"),
        name="ffn",
    )(x2, w_gu, w_down, ln_g, ln_b)


def _head_sum(z, m_ref):
    hi, lo = _split2(z)
    m = m_ref[...]
    w = m.shape[0]
    return jnp.concatenate([_dot(hi[:, c:c + w], m) + _dot(lo[:, c:c + w], m)
                            for c in range(0, z.shape[1], w)], axis=1)


def _rwkv_proj_kernel(has_vres, *refs):
    if has_vres:
        (x_ref, xp_ref, mix_ref, vec_ref, wr_ref, wk_ref, wv_ref, w1_ref, w2_ref, a1_ref, a2_ref,
         g1_ref, g2_ref, m_ref, v1_ref, v2_ref, vf_ref,
         r_ref, w_ref, k_ref, v_ref, kk_ref, a_ref, g_ref, bonus_ref, carry_ref) = refs
    else:
        (x_ref, xp_ref, mix_ref, vec_ref, wr_ref, wk_ref, wv_ref, w1_ref, w2_ref, a1_ref, a2_ref,
         g1_ref, g2_ref, m_ref,
         r_ref, w_ref, k_ref, v_ref, kk_ref, a_ref, g_ref, bonus_ref, carry_ref) = refs
    bb, tt, D = x_ref.shape
    n = bb * tt
    t = pl.program_id(1)
    x3 = x_ref[...]
    prev = jnp.where(t == 0, xp_ref[...], carry_ref[...])
    carry_ref[...] = x3[:, tt - 1:tt, :]
    x = x3.reshape(n, D)
    prev_rows = jnp.broadcast_to(prev, (bb, tt, D)).reshape(n, D)
    row = lax.broadcasted_iota(jnp.int32, (n, D), 0)
    xs = jnp.where(row % tt == 0, prev_rows, pltpu.roll(x, 1, 0))
    xx = xs - x
    xr, xw, xk, xv, xa, xg = ((x + xx * mix_ref[c:c + 1, :]).astype(BF) for c in range(6))
    w0, a0, v0, k_k, k_a = (vec_ref[c:c + 1, :] for c in (V_W0, V_A0, V_V0, V_KK, V_KA))

    r = _dot(xr, wr_ref[...])
    k = _dot(xk, wk_ref[...])
    v = _dot(xv, wv_ref[...])
    z = w0 + _dot(jnp.tanh(_dot(xw, w1_ref[...])).astype(BF), w2_ref[...])
    logd = -math.exp(-0.5) * _sigmoid(z)
    if has_vres:
        gate = _sigmoid(v0 + _dot(_dot(xv, v1_ref[...]).astype(BF), v2_ref[...]))
        v = v + (vf_ref[...].reshape(n, D) - v) * gate
    a = _sigmoid(a0 + _dot(_dot(xa, a1_ref[...]).astype(BF), a2_ref[...]))
    g = _dot(_sigmoid(_dot(xg, g1_ref[...])).astype(BF), g2_ref[...])
    kk = k * k_k
    kk = kk * jnp.minimum(lax.rsqrt(_head_sum(kk * kk, m_ref)), 1e12)
    k = k * (1.0 + (a - 1.0) * k_a)
    bonus = _head_sum(r * k * vec_ref[V_RK:V_RK + 1, :], m_ref) * v
    for ref, val in ((r_ref, r), (w_ref, logd), (k_ref, k), (v_ref, v), (kk_ref, kk), (a_ref, a),
                     (g_ref, g), (bonus_ref, bonus)):
        ref[...] = val.reshape(bb, tt, D)


def _rwkv_proj(x, x_prev, v_first, P, j, bb, tt):
    B, T, D = x.shape
    has_vres = j > 0
    blk = pl.BlockSpec((bb, tt, D), lambda b, t: (b, t, 0))
    w_rkv = P['rwkv_w_rkv']
    ins = [x, x_prev, P['rwkv_mix'], P['rwkv_vecs'], w_rkv, w_rkv, w_rkv,
           P['rwkv_w1'], P['rwkv_w2'], P['rwkv_a1'], P['rwkv_a2'], P['rwkv_g1'], P['rwkv_g2'],
           P['head_ones']]
    in_specs = [blk, pl.BlockSpec((bb, 1, D), lambda b, t: (b, 0, 0)),
                _layer(ins[2].shape, j), _layer(ins[3].shape, j),
                _layer(w_rkv.shape, j, 0), _layer(w_rkv.shape, j, 1), _layer(w_rkv.shape, j, 2)]
    in_specs += [_layer(a.shape, j) for a in ins[7:13]] + [_full(P['head_ones'].shape)]
    if has_vres:
        ins += [P['rwkv_v1'], P['rwkv_v2'], v_first]
        in_specs += [_layer(P['rwkv_v1'].shape, j - 1), _layer(P['rwkv_v2'].shape, j - 1), blk]
    return pl.pallas_call(
        functools.partial(_rwkv_proj_kernel, has_vres),
        grid=(B // bb, T // tt),
        in_specs=in_specs,
        out_specs=[blk] * 8,
        out_shape=[jax.ShapeDtypeStruct((B, T, D), F32)] * 8,
        scratch_shapes=[pltpu.VMEM((bb, 1, D), F32)],
        compiler_params=_cparams("parallel", "arbitrary"),
        name="rwkv_proj",
    )(*ins)


def _wkv_kernel(has_s0, first, C, *refs):
    r_ref, w_ref, k_ref, v_ref, kk_ref, a_ref = refs[:6]
    s0_ref = refs[6] if has_s0 else None
    o_ref, so_ref, s_scr = refs[-3:]
    groups = r_ref.shape[0]
    R = WKV_ROWS
    N = RWKV_HEAD
    NP = RWKV_HEADS // 2
    nseq = R // C
    t = pl.program_id(1)
    so_cur = so_ref.at[0] if first else so_ref

    @pl.when(t == 0)
    def _():
        if has_s0:
            for s in range(groups * nseq):
                for p in range(NP):
                    s_scr[s, p] = jnp.concatenate([s0_ref[s, 2 * p], s0_ref[s, 2 * p + 1]], axis=1)
        else:
            s_scr[...] = jnp.zeros_like(s_scr)

    D = r_ref.shape[-1]
    GR = groups * R
    flat = lambda ref: ref[...].reshape(GR, D)
    logd = flat(w_ref)
    ri = lax.broadcasted_iota(jnp.int32, (GR, GR), 0)
    ci = lax.broadcasted_iota(jnp.int32, (GR, GR), 1)
    tril = jnp.logical_and(ri // C == ci // C, ci <= ri).astype(BF)
    cum = sum(_dot(tril, part) for part in _split3(logd))
    g_in = jnp.exp(cum)
    g_ex = jnp.exp(cum - logd)
    g_inv = jnp.exp(-cum)
    kk = flat(kk_ref)
    at = -(kk * g_ex)
    rt = flat(r_ref) * g_in
    bt = kk * flat(a_ref) * g_inv
    kt = flat(k_ref) * g_inv
    v = flat(v_ref)

    def lane_masks(width):
        rr = lax.broadcasted_iota(jnp.int32, (R, width), 0)
        cc = lax.broadcasted_iota(jnp.int32, (R, width), 1) % N
        same = rr // C == cc // C
        return jnp.logical_and(same, cc < rr), jnp.logical_and(same, cc <= rr)

    strict2, _ = lane_masks(2 * N)
    _, incl4 = lane_masks(4 * N)

    def bdiag(m):
        lo = lax.broadcasted_iota(jnp.int32, m.shape, 1) < N
        zero = jnp.zeros_like(m)
        return jnp.concatenate([jnp.where(lo, m, zero), jnp.where(lo, zero, m)], axis=0)

    units = [(gi, p) for gi in range(groups) for p in range(NP)]
    seqs = range(nseq)
    tile = lambda z, u: z[u[0] * R:(u[0] + 1) * R, 2 * N * u[1]:2 * N * (u[1] + 1)]
    rows = [slice(C * s, C * (s + 1)) for s in seqs]
    vp = {u: tile(v, u) for u in units}
    vpb = {u: vp[u].astype(BF) for u in units}
    btp = {u: tile(bt, u) for u in units}
    ktp = {u: tile(kt, u) for u in units}
    xa = {u: tile(at, u) for u in units}
    xr = {u: tile(rt, u) for u in units}
    S0 = {(u, s): s_scr[u[0] * nseq + s, u[1]] for u in units for s in seqs}

    xp = {u: jnp.concatenate([xa[u], xr[u]], axis=0).astype(BF) for u in units}
    bmat = {u: jnp.concatenate([bdiag(btp[u].astype(BF)), bdiag(ktp[u].astype(BF))], axis=0)
            for u in units}
    G = {u: lax.dot_general(xp[u], bmat[u], NT_DIMS, preferred_element_type=F32) for u in units}
    Z = {(u, s): lax.dot_general(
        xp[u] if nseq == 1 else
        jnp.concatenate([xa[u][rows[s]], xr[u][rows[s]]], axis=0).astype(BF),
        bdiag(S0[u, s].astype(BF)), NT_DIMS, preferred_element_type=F32) for u in units for s in seqs}
    za = {u: Z[u, 0][:C] if nseq == 1 else jnp.concatenate([Z[u, s][:C] for s in seqs], axis=0)
          for u in units}
    zr = {u: Z[u, 0][C:] if nseq == 1 else jnp.concatenate([Z[u, s][C:] for s in seqs], axis=0)
          for u in units}
    A = {u: jnp.where(strict2, G[u][:R, :2 * N], 0.0) for u in units}
    aak = {u: jnp.where(strict2, G[u][:R, 2 * N:], 0.0).astype(BF) for u in units}
    T2 = {u: jnp.where(incl4, G[u][R:], 0.0).astype(BF) for u in units}
    W = {u: za[u] + _dot(aak[u], bdiag(vpb[u])) for u in units}
    nsteps = max(1, int(math.log2(C)))
    for step in range(nsteps):
        Ab = {u: A[u].astype(BF) for u in units}
        if step + 1 < nsteps:
            res = {u: _dot(Ab[u], jnp.concatenate([bdiag(W[u].astype(BF)), bdiag(Ab[u])], axis=1))
                   for u in units}
            W = {u: W[u] + res[u][:, :2 * N] for u in units}
            A = {u: res[u][:, 2 * N:] for u in units}
        else:
            W = {u: W[u] + _dot(Ab[u], bdiag(W[u].astype(BF))) for u in units}
    wb = {u: W[u].astype(BF) for u in units}
    for u in units:
        uv = jnp.concatenate([bdiag(wb[u]), bdiag(vpb[u])], axis=0)
        o_ref[u[0], :, 2 * N * u[1]:2 * N * (u[1] + 1)] = zr[u] + _dot(T2[u], uv)
    lo = lax.broadcasted_iota(jnp.int32, (N, 2 * N), 1) < N
    for u in units:
        for s in seqs:
            uvs = jnp.concatenate([W[u][rows[s]], vp[u][rows[s]]], axis=0).astype(BF)
            ys = jnp.concatenate([btp[u][rows[s]], ktp[u][rows[s]]], axis=0).astype(BF)
            full = lax.dot_general(uvs, ys, TN_DIMS, preferred_element_type=F32)
            upd = jnp.where(lo, full[:N], full[N:])
            last = u[0] * R + C * (s + 1) - 1
            g_end = g_in[last:last + 1, 2 * N * u[1]:2 * N * (u[1] + 1)]
            s_scr[u[0] * nseq + s, u[1]] = (S0[u, s] + upd) * g_end

    @pl.when(t == pl.num_programs(1) - 1)
    def _():
        for s in range(groups * nseq):
            for p in range(NP):
                so_cur[s, 2 * p] = s_scr[s, p][:, :N]
                so_cur[s, 2 * p + 1] = s_scr[s, p][:, N:]
        if first:
            for lyr in range(1, so_ref.shape[0]):
                so_ref[lyr] = jnp.zeros(so_ref.shape[1:], F32)


def _wkv(r, logd, k, v, kk, a, s0, layer, nlayers, s_prev, C, groups):
    B, T, D = r.shape
    has_s0 = s0 is not None
    first = s_prev is None
    R = WKV_ROWS
    nseq = R // C
    nt = T // C
    assert (nseq == 1 or nt == 1) and T % C == 0
    nbk = B * T // (R * nt)
    assert nbk % groups == 0
    gs = groups * nseq
    blk = pl.BlockSpec((groups, None, R, D), lambda b, t: (b, t, 0, 0))
    sdims = (RWKV_HEADS, RWKV_HEAD, RWKV_HEAD)
    sblk = pl.BlockSpec((None, gs) + sdims, lambda b, t: (layer, b, 0, 0, 0))
    oblk = pl.BlockSpec((nlayers, gs) + sdims, lambda b, t: (0, b, 0, 0, 0)) if first else sblk
    ins = [x.reshape(nbk, nt, R, D) for x in (r, logd, k, v, kk, a)]
    in_specs = [blk] * 6
    if has_s0:
        ins.append(s0[layer])
        in_specs.append(pl.BlockSpec((gs,) + sdims, lambda b, t: (b, 0, 0, 0)))
    aliases = {}
    if not first:
        aliases = {len(ins): 1}
        ins.append(s_prev)
        in_specs.append(pl.BlockSpec(memory_space=pl.ANY))
    o, s_new = pl.pallas_call(
        functools.partial(_wkv_kernel, has_s0, first, C),
        grid=(nbk // groups, nt),
        in_specs=in_specs,
        out_specs=[blk, oblk],
        out_shape=[jax.ShapeDtypeStruct((nbk, nt, R, D), F32),
                   jax.ShapeDtypeStruct((nlayers, B) + sdims, F32)],
        scratch_shapes=[pltpu.VMEM((gs, RWKV_HEADS // 2, RWKV_HEAD, 2 * RWKV_HEAD), F32)],
        input_output_aliases=aliases,
        compiler_params=_cparams("parallel", "arbitrary"),
        name="wkv",
    )(*ins)
    return o.reshape(B, T, D), s_new


def _rwkv_out_kernel(o_ref, bonus_ref, gate_ref, x_ref, vec_ref, m_ref, wo_ref, g_ref, beta_ref, y_ref):
    ln_w, ln_b = (vec_ref[c:c + 1, :] for c in (V_LNW, V_LNB))
    o = o_ref[...]
    inv_n = 1.0 / RWKV_HEAD
    mu = _head_sum(o, m_ref) * inv_n
    oc = o - mu
    var = _head_sum(oc * oc, m_ref) * inv_n
    on = oc * lax.rsqrt(var + GN_EPS) * ln_w + ln_b
    y = _dot(((on + bonus_ref[...]) * gate_ref[...]).astype(BF), wo_ref[...])
    y_ref[...] = _layer_norm(ALPHA * x_ref[...] + y, g_ref[0:1, :], beta_ref[0:1, :])


def _rwkv_out(o, bonus, gate, x2, P, j, depth, tm=256):
    n = o.shape[0]
    tm = min(tm, n)
    rows = pl.BlockSpec((tm, D_MODEL), lambda i: (i, 0))
    vecs, w_o, ln_g, ln_b = P['rwkv_vecs'], P['rwkv_w_o'], P['ln_g'], P['ln_b']
    return pl.pallas_call(
        _rwkv_out_kernel,
        grid=(n // tm,),
        in_specs=[rows] * 4 + [_layer(vecs.shape, j), _full(P['head_ones'].shape), _layer(w_o.shape, j),
                               _layer(ln_g.shape, depth), _layer(ln_b.shape, depth)],
        out_specs=rows,
        out_shape=jax.ShapeDtypeStruct((n, D_MODEL), F32),
        compiler_params=_cparams("parallel"),
        name="rwkv_out",
    )(o, bonus, gate, x2, vecs, P['head_ones'], w_o, ln_g, ln_b)


def _trunk(x, win_k, win_v, shift, wkv, P, bias):
    prompt = win_k is None
    B, T, D = x.shape
    n = B * T
    nkv = N_KV_HEADS * HEAD_DIM
    n_rwkv = DEPTH // 2
    v_first = None
    nk, nv, ns = [], [], []
    nk_all = nv_all = s_all = None
    for i in range(DEPTH):
        j = i // 2
        x2 = x.reshape(n, D)
        if i % 2 == 0:
            if prompt:
                q, k, v = _qkv_proj(x2, P['attn_w_qkv'], P['attn_b_qkv'], j)
                q, k, v = q.reshape(B, T, D), k.reshape(B, T, nkv), v.reshape(B, T, nkv)
                o = _swa_prompt(q, k, v, bias, P['attn_sinks'][j])
                nk.append(k[:, T - WINDOW:].reshape(B, WINDOW, N_KV_HEADS, HEAD_DIM))
                nv.append(v[:, T - WINDOW:].reshape(B, WINDOW, N_KV_HEADS, HEAD_DIM))
            else:
                q, kvt = _qkv_proj_t(x2, P['attn_wq'], P['attn_wkvt'], P['attn_bq'], P['attn_bkvt'], j)
                o, nk_all, nv_all = _swa_sample(q.reshape(B, T, D), kvt, win_k, win_v, j, bias,
                                                P['attn_sinks'][j], nk_all, nv_all)
            x2 = _proj_ln(o.reshape(n, D), P['attn_w_o'], P['attn_b_o'], x2, P['ln_g'], P['ln_b'], j, i)
        else:
            x_prev = jnp.zeros((B, 1, D), x.dtype) if prompt else shift[j].reshape(B, 1, D)
            bb, tt = (1, 256) if prompt else (min(32, B), T)
            r, logd, k, v, kk, a, gate, bonus = _rwkv_proj(x, x_prev, v_first, P, j, bb, tt)
            if j == 0:
                v_first = v
            C, groups = (WKV_ROWS, math.gcd(B, 4)) if prompt else (T, 1)
            o, s_all = _wkv(r, logd, k, v, kk, a, wkv, j, n_rwkv, s_all, C, groups)
            ns.append(x[:, -1])
            x2 = _rwkv_out(o.reshape(n, D), bonus.reshape(n, D), gate.reshape(n, D), x2, P, j, i)
        x2 = _ffn(x2, P['ffn_w_gu'], P['ffn_w_down'], P['ln_g'], P['ln_b'], i)
        x = x2.reshape(B, T, D)
    if prompt:
        nk_all, nv_all = jnp.stack(nk), jnp.stack(nv)
    else:
        nk_all = jnp.transpose(nk_all, (0, 1, 4, 2, 3))
        nv_all = jnp.transpose(nv_all, (0, 1, 4, 2, 3))
    return x, nk_all, nv_all, jnp.stack(ns), s_all


def kernel(x_prompt, x_sample, cache_win_k, cache_win_v, state_shift, state_wkv, rel_bias, ln_g, ln_b, attn_w_qkv, attn_b_qkv, attn_w_o, attn_b_o, attn_sinks, rwkv_mix, rwkv_w_rkv, rwkv_w0, rwkv_w1, rwkv_w2, rwkv_a0, rwkv_a1, rwkv_a2, rwkv_v0, rwkv_v1, rwkv_v2, rwkv_g1, rwkv_g2, rwkv_k_k, rwkv_k_a, rwkv_r_k, rwkv_ln_w, rwkv_ln_b, rwkv_w_o, ffn_w_gu, ffn_w_down):
    bf = lambda w: w.astype(BF)
    lane_head = np.arange(MXU_WIDTH) // RWKV_HEAD
    head_ones = jnp.asarray(lane_head[:, None] == lane_head[None, :], dtype=BF)
    n_rwkv = rwkv_w0.shape[0]
    nq = N_HEADS * HEAD_DIM
    v0_rows = jnp.concatenate([jnp.zeros((1, D_MODEL), F32), rwkv_v0], axis=0)
    rwkv_vecs = jnp.stack([rwkv_w0, rwkv_a0, v0_rows, rwkv_k_k, rwkv_k_a,
                           rwkv_r_k.reshape(n_rwkv, D_MODEL), rwkv_ln_w, rwkv_ln_b], axis=1)
    w_qkv = bf(attn_w_qkv)
    b_qkv = attn_b_qkv[:, None, :]
    P = dict(ln_g=ln_g, ln_b=ln_b,
             attn_w_qkv=w_qkv, attn_b_qkv=b_qkv,
             attn_wq=w_qkv[:, :, :nq], attn_wkvt=jnp.transpose(w_qkv[:, :, nq:], (0, 2, 1)),
             attn_bq=b_qkv[:, :, :nq], attn_bkvt=jnp.transpose(b_qkv[:, :, nq:], (0, 2, 1)),
             attn_w_o=bf(attn_w_o), attn_b_o=attn_b_o[:, None, :], attn_sinks=attn_sinks,
             rwkv_mix=rwkv_mix, rwkv_vecs=rwkv_vecs, rwkv_w_rkv=bf(rwkv_w_rkv),
             rwkv_w1=bf(rwkv_w1), rwkv_w2=bf(rwkv_w2), rwkv_a1=bf(rwkv_a1), rwkv_a2=bf(rwkv_a2),
             rwkv_v1=bf(rwkv_v1), rwkv_v2=bf(rwkv_v2), rwkv_g1=bf(rwkv_g1), rwkv_g2=bf(rwkv_g2),
             rwkv_w_o=bf(rwkv_w_o), ffn_w_gu=bf(ffn_w_gu), ffn_w_down=bf(ffn_w_down),
             head_ones=head_ones)

    T = x_sample.shape[1]
    L = cache_win_k.shape[2]
    qi, kc = np.arange(WINDOW)[:, None], np.arange(WINDOW)[None, :]
    d_prompt = np.where(kc <= qi, qi - kc, qi + WINDOW - kc)
    d_sample = np.arange(T)[:, None] + L - np.arange(L + T)[None, :]
    bias_prompt = _bias_table(rel_bias, d_prompt)
    bias_sample = _bias_table(rel_bias, d_sample)

    y_prompt, pk, pv, ps, pw = _trunk(x_prompt, None, None, None, None, P, bias_prompt)
    win_k = jnp.transpose(cache_win_k, (0, 1, 3, 4, 2))
    win_v = jnp.transpose(cache_win_v, (0, 1, 3, 4, 2))
    y_sample, sk, sv, ss, sw = _trunk(x_sample, win_k, win_v, state_shift, state_wkv,
                                      P, bias_sample)
    return (y_prompt, y_sample, pk, pv, ps, pw, sk, sv, ss, sw)
```

```python
import functools
import math

import numpy as np
import jax
import jax.numpy as jnp
from jax import lax
from jax.experimental import pallas as pl
from jax.experimental.pallas import tpu as pltpu

D_MODEL = 1024
DEPTH = 4
HEAD_DIM = 64
N_HEADS = 16
N_KV_HEADS = 4
GROUP = 4
WINDOW = 128
N_BUCKETS = 32
MAX_DISTANCE = 128
RWKV_HEAD = 64
RWKV_HEADS = 16
GN_EPS = 64e-5
D_FF = 2816
ALPHA = (2 * DEPTH) ** 0.25
LN_EPS = 1e-5
NEG = -1e30
MXU_WIDTH = 256
WKV_ROWS = 64

BF = jnp.bfloat16
F32 = jnp.float32
VMEM_LIMIT = 56 * 1024 * 1024

NT_DIMS = (((1,), (1,)), ((), ()))
TN_DIMS = (((0,), (0,)), ((), ()))

V_W0, V_A0, V_V0, V_KK, V_KA, V_RK, V_LNW, V_LNB = range(8)


def _cparams(*sem):
    return pltpu.CompilerParams(dimension_semantics=sem, vmem_limit_bytes=VMEM_LIMIT)


def _dot(a, b):
    return jnp.dot(a, b, preferred_element_type=F32)


def _layer_norm(z, g, b):
    mu = jnp.mean(z, axis=-1, keepdims=True)
    zc = z - mu
    var = jnp.mean(zc * zc, axis=-1, keepdims=True)
    return zc * lax.rsqrt(var + LN_EPS) * g + b


def _sigmoid(z):
    return 1.0 / (1.0 + jnp.exp(-z))


def _split2(z):
    hi = z.astype(BF)
    lo = (z - hi.astype(F32)).astype(BF)
    return hi, lo


def _split3(z):
    hi = z.astype(BF)
    r1 = z - hi.astype(F32)
    mid = r1.astype(BF)
    lo = (r1 - mid.astype(F32)).astype(BF)
    return hi, mid, lo


def _full(shape):
    n = len(shape)
    return pl.BlockSpec(shape, lambda *_: (0,) * n)


def _layer(shape, *lead):
    k = len(lead)
    rest = len(shape) - k
    return pl.BlockSpec((None,) * k + tuple(shape[k:]), lambda *_: tuple(lead) + (0,) * rest,
                        pipeline_mode=pl.Buffered(1))


def _bias_table_kernel(rb_ref, bucket_ref, valid_ref, o_ref):
    bucket = bucket_ref[...]
    valid = valid_ref[...] > 0
    for h in range(N_HEADS):
        acc = jnp.zeros(bucket.shape, F32)
        for b in range(N_BUCKETS):
            acc = jnp.where(bucket == b, rb_ref[b, h], acc)
        o_ref[h] = jnp.where(valid, acc, NEG)


def _bias_table(rel_bias, d):
    dc = np.maximum(d, 0)
    exact = N_BUCKETS // 2
    df = np.maximum(dc, 1).astype(np.float32)
    large = exact + (np.log(df / np.float32(exact)) / np.float32(math.log(MAX_DISTANCE / exact))
                     * np.float32(N_BUCKETS - exact)).astype(np.int32)
    bucket = np.where(dc < exact, dc, np.minimum(large, N_BUCKETS - 1)).astype(np.int32)
    valid = ((d >= 0) & (d < WINDOW)).astype(np.int32)
    return pl.pallas_call(
        _bias_table_kernel,
        out_shape=jax.ShapeDtypeStruct((N_HEADS,) + d.shape, F32),
        in_specs=[pl.BlockSpec(memory_space=pltpu.SMEM),
                  pl.BlockSpec(memory_space=pltpu.VMEM),
                  pl.BlockSpec(memory_space=pltpu.VMEM)],
        out_specs=pl.BlockSpec(memory_space=pltpu.VMEM),
        name="bias_table",
    )(rel_bias, jnp.asarray(bucket), jnp.asarray(valid))


def _qkv_kernel(x_ref, w_ref, b_ref, q_ref, k_ref, v_ref):
    acc = _dot(x_ref[...].astype(BF), w_ref[...]) + b_ref[...]
    nq = N_HEADS * HEAD_DIM
    nk = N_KV_HEADS * HEAD_DIM
    q_ref[...] = (acc[:, :nq] * HEAD_DIM ** -0.5).astype(q_ref.dtype)
    k_ref[...] = acc[:, nq:nq + nk]
    v_ref[...] = acc[:, nq + nk:]


def _qkv_proj(x2, w, b, layer, tm=1024):
    n = x2.shape[0]
    tm = min(tm, n)
    nq = N_HEADS * HEAD_DIM
    nk = N_KV_HEADS * HEAD_DIM
    return pl.pallas_call(
        _qkv_kernel,
        grid=(n // tm,),
        in_specs=[pl.BlockSpec((tm, D_MODEL), lambda i: (i, 0)),
                  _layer(w.shape, layer), _layer(b.shape, layer)],
        out_specs=[pl.BlockSpec((tm, nq), lambda i: (i, 0)),
                   pl.BlockSpec((tm, nk), lambda i: (i, 0)),
                   pl.BlockSpec((tm, nk), lambda i: (i, 0))],
        out_shape=[jax.ShapeDtypeStruct((n, nq), BF),
                   jax.ShapeDtypeStruct((n, nk), F32),
                   jax.ShapeDtypeStruct((n, nk), F32)],
        compiler_params=_cparams("parallel"),
        name="qkv_proj",
    )(x2, w, b)


def _swa_prompt_kernel(sink_ref, q_ref, kc_ref, kp_ref, vc_ref, vp_ref, bias_ref, o_ref):
    step = pl.program_id(1)
    W = WINDOW
    nsub = q_ref.shape[1] // W
    kall = jnp.concatenate([kp_ref[0], kc_ref[0]], axis=0).astype(BF)
    vall = jnp.concatenate([vp_ref[0], vc_ref[0]], axis=0).astype(BF)
    row = lax.broadcasted_iota(jnp.int32, (W, W), 0)
    col = lax.broadcasted_iota(jnp.int32, (W, W), 1)
    own = col <= row
    zero = jnp.zeros((W, W), BF)
    subs = range(nsub)
    scores = {}
    for sub in subs:
        for h in range(N_HEADS):
            g = h // GROUP
            qh = q_ref[0, sub * W:(sub + 1) * W, h * HEAD_DIM:(h + 1) * HEAD_DIM]
            kh = kall[sub * W:(sub + 2) * W, g * HEAD_DIM:(g + 1) * HEAD_DIM]
            s2 = lax.dot_general(qh, kh, NT_DIMS, preferred_element_type=F32)
            s_prev = s2[:, :W]
            if sub == 0:
                s_prev = jnp.where(step == 0, NEG, s_prev)
            scores[sub, h] = jnp.where(own, s2[:, W:], s_prev) + bias_ref[h]
    m, pb = {}, {}
    for sub in subs:
        for g in range(N_KV_HEADS):
            s = jnp.concatenate([scores[sub, g * GROUP + j] for j in range(GROUP)], axis=0)
            mg = jnp.max(s, axis=-1, keepdims=True)
            m[sub, g] = mg
            pb[sub, g] = jnp.exp(s - mg).astype(BF)
    ones = jnp.ones((2 * W, HEAD_DIM), BF)
    lo = lax.broadcasted_iota(jnp.int32, (W, 2 * HEAD_DIM), 1) < HEAD_DIM
    for sub in subs:
        for g in range(N_KV_HEADS):
            vh = vall[sub * W:(sub + 2) * W, g * HEAD_DIM:(g + 1) * HEAD_DIM]
            vext = jnp.concatenate([vh, ones, ones, vh], axis=1)
            res, esink = [], []
            for j in range(GROUP):
                hs = slice(j * W, (j + 1) * W)
                ph = pb[sub, g][hs]
                p2 = jnp.concatenate([jnp.where(own, zero, ph), jnp.where(own, ph, zero)], axis=1)
                res.append(_dot(p2, vext))
                esink.append(jnp.exp(sink_ref[g * GROUP + j] - m[sub, g][hs]))
            for j in range(0, GROUP, 2):
                o_pair = jnp.where(lo, res[j][:, :2 * HEAD_DIM], res[j + 1][:, 2 * HEAD_DIM:])
                t_pair = jnp.where(lo, res[j][:, 2 * HEAD_DIM:], res[j + 1][:, :2 * HEAD_DIM])
                den = t_pair + jnp.where(lo, esink[j], esink[j + 1])
                c0 = (g * GROUP + j) * HEAD_DIM
                o_ref[0, sub * W:(sub + 1) * W, c0:c0 + 2 * HEAD_DIM] = (o_pair / den).astype(o_ref.dtype)


def _swa_prompt(q, k, v, bias, sinks, nsub=4):
    B, T, _ = q.shape
    nb = T // WINDOW
    nsub = math.gcd(nb, nsub)
    nk = N_KV_HEADS * HEAD_DIM
    cur = lambda b, i: (b, i, 0)
    prev = lambda b, i: (b, jnp.maximum(i * nsub - 1, 0), 0)
    return pl.pallas_call(
        _swa_prompt_kernel,
        grid=(B, nb // nsub),
        in_specs=[pl.BlockSpec(memory_space=pltpu.SMEM),
                  pl.BlockSpec((1, nsub * WINDOW, D_MODEL), cur),
                  pl.BlockSpec((1, nsub * WINDOW, nk), cur),
                  pl.BlockSpec((1, WINDOW, nk), prev),
                  pl.BlockSpec((1, nsub * WINDOW, nk), cur),
                  pl.BlockSpec((1, WINDOW, nk), prev),
                  _full(bias.shape)],
        out_specs=pl.BlockSpec((1, nsub * WINDOW, D_MODEL), cur),
        out_shape=jax.ShapeDtypeStruct((B, T, D_MODEL), BF),
        compiler_params=_cparams("parallel", "parallel"),
        name="swa_prompt",
    )(sinks, q, k, k, v, v, bias)


def _qkv_t_kernel(x_ref, wq_ref, wkvt_ref, bq_ref, bkvt_ref, q_ref, kvt_ref):
    xb = x_ref[...].astype(BF)
    q_ref[...] = (_dot(xb, wq_ref[...]) + bq_ref[...]) * HEAD_DIM ** -0.5
    kvt_ref[...] = lax.dot_general(wkvt_ref[...], xb, NT_DIMS, preferred_element_type=F32) + bkvt_ref[...]


def _qkv_proj_t(x2, wq, wkvt, bq, bkvt, layer, tm=256):
    n = x2.shape[0]
    tm = min(tm, n)
    nq = N_HEADS * HEAD_DIM
    nkv = 2 * N_KV_HEADS * HEAD_DIM
    return pl.pallas_call(
        _qkv_t_kernel,
        grid=(n // tm,),
        in_specs=[pl.BlockSpec((tm, D_MODEL), lambda i: (i, 0)),
                  _layer(wq.shape, layer), _layer(wkvt.shape, layer),
                  _layer(bq.shape, layer), _layer(bkvt.shape, layer)],
        out_specs=[pl.BlockSpec((tm, nq), lambda i: (i, 0)),
                   pl.BlockSpec((nkv, tm), lambda i: (0, i))],
        out_shape=[jax.ShapeDtypeStruct((n, nq), F32),
                   jax.ShapeDtypeStruct((nkv, n), F32)],
        compiler_params=_cparams("parallel"),
        name="qkv_proj_t",
    )(x2, wq, wkvt, bq, bkvt)


def _swa_sample_kernel(T, first, sink_ref, q_ref, kvt_ref, ck_ref, cv_ref, bc_ref, bn_ref, *rest):
    o_ref, nk_ref, nv_ref = rest[-3:]
    bb = q_ref.shape[0]
    L = ck_ref.shape[-1]
    HD = HEAD_DIM
    if first:
        for lyr in range(1, nk_ref.shape[0]):
            nk_ref[lyr] = jnp.zeros(nk_ref.shape[1:], F32)
            nv_ref[lyr] = jnp.zeros(nv_ref.shape[1:], F32)
        nk_out, nv_out = nk_ref.at[0], nv_ref.at[0]
    else:
        nk_out, nv_out = nk_ref, nv_ref
    units = [(b, g) for b in range(bb) for g in range(N_KV_HEADS)]
    lane = lax.broadcasted_iota(jnp.int32, (HD, L), 1)
    knew = [kvt_ref[g * HD:(g + 1) * HD, :] for g in range(N_KV_HEADS)]
    vnew = [kvt_ref[(N_KV_HEADS + g) * HD:(N_KV_HEADS + g + 1) * HD, :] for g in range(N_KV_HEADS)]

    def shift_in(old, new, b):
        moved = pltpu.roll(new, (L - T - b * T) % L, 1)
        return jnp.where(lane >= L - T, moved, pltpu.roll(old, L - T, 1))

    sc, sn = {}, {}
    for b in range(bb):
        q = q_ref[b]
        for g in range(N_KV_HEADS):
            qg = jnp.concatenate([q[:, (g * GROUP + j) * HD:(g * GROUP + j + 1) * HD]
                                  for j in range(GROUP)], axis=0).astype(BF)
            kc = ck_ref[b, g]
            nk_out[b, g] = shift_in(kc, knew[g], b)
            sc[b, g] = _dot(qg, kc.astype(BF)) + bc_ref[g]
            sn[b, g] = _dot(qg, knew[g][:, b * T:(b + 1) * T].astype(BF)) + bn_ref[g]
    sc_all = jnp.concatenate([sc[u] for u in units], axis=0)
    sn_all = jnp.concatenate([sn[u] for u in units], axis=0)
    m = jnp.maximum(jnp.max(sc_all, axis=-1, keepdims=True), jnp.max(sn_all, axis=-1, keepdims=True))
    pc_all = jnp.exp(sc_all - m)
    pn_all = jnp.exp(sn_all - m)
    tot = jnp.sum(pc_all, axis=-1, keepdims=True) + jnp.sum(pn_all, axis=-1, keepdims=True)
    pc_all = pc_all.astype(BF)
    pn_all = pn_all.astype(BF)
    gt = GROUP * T
    for b in range(bb):
        pieces = []
        for g in range(N_KV_HEADS):
            r0 = (b * N_KV_HEADS + g) * gt
            vc = cv_ref[b, g]
            nv_out[b, g] = shift_in(vc, vnew[g], b)
            og = (lax.dot_general(pc_all[r0:r0 + gt], vc.astype(BF), NT_DIMS, preferred_element_type=F32)
                  + lax.dot_general(pn_all[r0:r0 + gt], vnew[g][:, b * T:(b + 1) * T].astype(BF),
                                    NT_DIMS, preferred_element_type=F32))
            for j in range(GROUP):
                rj = slice(r0 + j * T, r0 + (j + 1) * T)
                inv = 1.0 / (tot[rj] + jnp.exp(sink_ref[g * GROUP + j] - m[rj]))
                pieces.append(og[j * T:(j + 1) * T] * inv)
        o_ref[b] = jnp.concatenate(pieces, axis=-1).astype(o_ref.dtype)


def _swa_sample(q, kvt, ck, cv, layer, bias, sinks, nk_prev, nv_prev):
    B, T, _ = q.shape
    nl, L = ck.shape[0], ck.shape[-1]
    first = nk_prev is None
    bb = L // T
    assert B % bb == 0 and L % T == 0
    bias = bias.reshape(N_KV_HEADS, GROUP * T, L + T)
    bc, bn = bias[:, :, :L], bias[:, :, L:]
    qblk = pl.BlockSpec((bb, T, D_MODEL), lambda i: (i, 0, 0))
    cblk = pl.BlockSpec((None, bb, N_KV_HEADS, HEAD_DIM, L), lambda i: (layer, i, 0, 0, 0))
    nblk = (pl.BlockSpec((nl, bb, N_KV_HEADS, HEAD_DIM, L), lambda i: (0, i, 0, 0, 0)) if first else cblk)
    ins = [sinks, q, kvt, ck, cv, bc, bn]
    in_specs = [pl.BlockSpec(memory_space=pltpu.SMEM), qblk,
                pl.BlockSpec((kvt.shape[0], bb * T), lambda i: (0, i)),
                cblk, cblk, _full(bc.shape), _full(bn.shape)]
    aliases = {}
    if not first:
        aliases = {len(ins): 1, len(ins) + 1: 2}
        ins += [nk_prev, nv_prev]
        in_specs += [pl.BlockSpec(memory_space=pl.ANY)] * 2
    return pl.pallas_call(
        functools.partial(_swa_sample_kernel, T, first),
        grid=(B // bb,),
        in_specs=in_specs,
        out_specs=[qblk, nblk, nblk],
        out_shape=[jax.ShapeDtypeStruct((B, T, D_MODEL), BF),
                   jax.ShapeDtypeStruct((nl, B, N_KV_HEADS, HEAD_DIM, L), F32),
                   jax.ShapeDtypeStruct((nl, B, N_KV_HEADS, HEAD_DIM, L), F32)],
        input_output_aliases=aliases,
        compiler_params=_cparams("parallel"),
        name="swa_sample",
    )(*ins)


def _proj_ln_kernel(which, a_ref, w_ref, b_ref, x_ref, g_ref, beta_ref, o_ref):
    y = _dot(a_ref[...].astype(BF), w_ref[...]) + b_ref[...]
    o_ref[...] = _layer_norm(ALPHA * x_ref[...] + y, g_ref[which:which + 1, :], beta_ref[which:which + 1, :])


def _proj_ln(a, w, b, x2, ln_g, ln_b, layer, depth, tm=1024):
    n, kdim = a.shape
    tm = min(tm, n)
    rows = lambda i: (i, 0)
    return pl.pallas_call(
        functools.partial(_proj_ln_kernel, 0),
        grid=(n // tm,),
        in_specs=[pl.BlockSpec((tm, kdim), rows), _layer(w.shape, layer), _layer(b.shape, layer),
                  pl.BlockSpec((tm, D_MODEL), rows), _layer(ln_g.shape, depth), _layer(ln_b.shape, depth)],
        out_specs=pl.BlockSpec((tm, D_MODEL), rows),
        out_shape=jax.ShapeDtypeStruct((n, D_MODEL), F32),
        compiler_params=_cparams("parallel"),
        name="proj_ln",
    )(a, w, b, x2, ln_g, ln_b)


def _ffn_kernel(tf, x_ref, wgu_ref, wd_ref, g_ref, beta_ref, o_ref):
    x = x_ref[...]
    xb = x.astype(BF)
    acc = None
    for c in range(0, D_FF, tf):
        gt = _dot(xb, wgu_ref[:, c:c + tf])
        ut = _dot(xb, wgu_ref[:, D_FF + c:D_FF + c + tf])
        h = (gt * _sigmoid(gt) * ut).astype(BF)
        part = _dot(h, wd_ref[c:c + tf, :])
        acc = part if acc is None else acc + part
    o_ref[...] = _layer_norm(ALPHA * x + acc, g_ref[1:2, :], beta_ref[1:2, :])


def _ffn(x2, w_gu, w_down, ln_g, ln_b, depth, tm=512, tf=1408):
    n = x2.shape[0]
    tm = min(tm, n)
    assert D_FF % tf == 0 and tf % 128 == 0
    rows = lambda i: (i, 0)
    return pl.pallas_call(
        functools.partial(_ffn_kernel, tf),
        grid=(n // tm,),
        in_specs=[pl.BlockSpec((tm, D_MODEL), rows),
                  _layer(w_gu.shape, depth), _layer(w_down.shape, depth),
                  _layer(ln_g.shape, depth), _layer(ln_b.shape, depth)],
        out_specs=pl.BlockSpec((tm, D_MODEL), rows),
        out_shape=jax.ShapeDtypeStruct((n, D_MODEL), F32),
        compiler_params=_cparams("parallel"),
        name="ffn",
    )(x2, w_gu, w_down, ln_g, ln_b)


def _head_sum(z, m_ref):
    hi, lo = _split2(z)
    m = m_ref[...]
    w = m.shape[0]
    return jnp.concatenate([_dot(hi[:, c:c + w], m) + _dot(lo[:, c:c + w], m)
                            for c in range(0, z.shape[1], w)], axis=1)


def _rwkv_proj_kernel(has_vres, *refs):
    if has_vres:
        (x_ref, xp_ref, mix_ref, vec_ref, wr_ref, wk_ref, wv_ref, w1_ref, w2_ref, a1_ref, a2_ref,
         g1_ref, g2_ref, m_ref, v1_ref, v2_ref, vf_ref,
         r_ref, w_ref, k_ref, v_ref, kk_ref, a_ref, g_ref, bonus_ref, carry_ref) = refs
    else:
        (x_ref, xp_ref, mix_ref, vec_ref, wr_ref, wk_ref, wv_ref, w1_ref, w2_ref, a1_ref, a2_ref,
         g1_ref, g2_ref, m_ref,
         r_ref, w_ref, k_ref, v_ref, kk_ref, a_ref, g_ref, bonus_ref, carry_ref) = refs
    bb, tt, D = x_ref.shape
    n = bb * tt
    t = pl.program_id(1)
    x3 = x_ref[...]
    prev = jnp.where(t == 0, xp_ref[...], carry_ref[...])
    carry_ref[...] = x3[:, tt - 1:tt, :]
    x = x3.reshape(n, D)
    prev_rows = jnp.broadcast_to(prev, (bb, tt, D)).reshape(n, D)
    row = lax.broadcasted_iota(jnp.int32, (n, D), 0)
    xs = jnp.where(row % tt == 0, prev_rows, pltpu.roll(x, 1, 0))
    xx = xs - x
    xr, xw, xk, xv, xa, xg = ((x + xx * mix_ref[c:c + 1, :]).astype(BF) for c in range(6))
    w0, a0, v0, k_k, k_a = (vec_ref[c:c + 1, :] for c in (V_W0, V_A0, V_V0, V_KK, V_KA))

    r = _dot(xr, wr_ref[...])
    k = _dot(xk, wk_ref[...])
    v = _dot(xv, wv_ref[...])
    z = w0 + _dot(jnp.tanh(_dot(xw, w1_ref[...])).astype(BF), w2_ref[...])
    logd = -math.exp(-0.5) * _sigmoid(z)
    if has_vres:
        gate = _sigmoid(v0 + _dot(_dot(xv, v1_ref[...]).astype(BF), v2_ref[...]))
        v = v + (vf_ref[...].reshape(n, D) - v) * gate
    a = _sigmoid(a0 + _dot(_dot(xa, a1_ref[...]).astype(BF), a2_ref[...]))
    g = _dot(_sigmoid(_dot(xg, g1_ref[...])).astype(BF), g2_ref[...])
    kk = k * k_k
    kk = kk * jnp.minimum(lax.rsqrt(_head_sum(kk * kk, m_ref)), 1e12)
    k = k * (1.0 + (a - 1.0) * k_a)
    bonus = _head_sum(r * k * vec_ref[V_RK:V_RK + 1, :], m_ref) * v
    for ref, val in ((r_ref, r), (w_ref, logd), (k_ref, k), (v_ref, v), (kk_ref, kk), (a_ref, a),
                     (g_ref, g), (bonus_ref, bonus)):
        ref[...] = val.reshape(bb, tt, D)


def _rwkv_proj(x, x_prev, v_first, P, j, bb, tt):
    B, T, D = x.shape
    has_vres = j > 0
    blk = pl.BlockSpec((bb, tt, D), lambda b, t: (b, t, 0))
    w_rkv = P['rwkv_w_rkv']
    ins = [x, x_prev, P['rwkv_mix'], P['rwkv_vecs'], w_rkv, w_rkv, w_rkv,
           P['rwkv_w1'], P['rwkv_w2'], P['rwkv_a1'], P['rwkv_a2'], P['rwkv_g1'], P['rwkv_g2'],
           P['head_ones']]
    in_specs = [blk, pl.BlockSpec((bb, 1, D), lambda b, t: (b, 0, 0)),
                _layer(ins[2].shape, j), _layer(ins[3].shape, j),
                _layer(w_rkv.shape, j, 0), _layer(w_rkv.shape, j, 1), _layer(w_rkv.shape, j, 2)]
    in_specs += [_layer(a.shape, j) for a in ins[7:13]] + [_full(P['head_ones'].shape)]
    if has_vres:
        ins += [P['rwkv_v1'], P['rwkv_v2'], v_first]
        in_specs += [_layer(P['rwkv_v1'].shape, j - 1), _layer(P['rwkv_v2'].shape, j - 1), blk]
    return pl.pallas_call(
        functools.partial(_rwkv_proj_kernel, has_vres),
        grid=(B // bb, T // tt),
        in_specs=in_specs,
        out_specs=[blk] * 8,
        out_shape=[jax.ShapeDtypeStruct((B, T, D), F32)] * 8,
        scratch_shapes=[pltpu.VMEM((bb, 1, D), F32)],
        compiler_params=_cparams("parallel", "arbitrary"),
        name="rwkv_proj",
    )(*ins)


def _wkv_kernel(has_s0, first, C, *refs):
    r_ref, w_ref, k_ref, v_ref, kk_ref, a_ref = refs[:6]
    s0_ref = refs[6] if has_s0 else None
    o_ref, so_ref, s_scr = refs[-3:]
    groups = r_ref.shape[0]
    R = WKV_ROWS
    N = RWKV_HEAD
    NP = RWKV_HEADS // 2
    nseq = R // C
    t = pl.program_id(1)
    so_cur = so_ref.at[0] if first else so_ref

    @pl.when(t == 0)
    def _():
        if has_s0:
            for s in range(groups * nseq):
                for p in range(NP):
                    s_scr[s, p] = jnp.concatenate([s0_ref[s, 2 * p], s0_ref[s, 2 * p + 1]], axis=1)
        else:
            s_scr[...] = jnp.zeros_like(s_scr)

    D = r_ref.shape[-1]
    GR = groups * R
    flat = lambda ref: ref[...].reshape(GR, D)
    logd = flat(w_ref)
    ri = lax.broadcasted_iota(jnp.int32, (GR, GR), 0)
    ci = lax.broadcasted_iota(jnp.int32, (GR, GR), 1)
    tril = jnp.logical_and(ri // C == ci // C, ci <= ri).astype(BF)
    cum = sum(_dot(tril, part) for part in _split3(logd))
    g_in = jnp.exp(cum)
    g_ex = jnp.exp(cum - logd)
    g_inv = jnp.exp(-cum)
    kk = flat(kk_ref)
    at = -(kk * g_ex)
    rt = flat(r_ref) * g_in
    bt = kk * flat(a_ref) * g_inv
    kt = flat(k_ref) * g_inv
    v = flat(v_ref)

    def lane_masks(width):
        rr = lax.broadcasted_iota(jnp.int32, (R, width), 0)
        cc = lax.broadcasted_iota(jnp.int32, (R, width), 1) % N
        same = rr // C == cc // C
        return jnp.logical_and(same, cc < rr), jnp.logical_and(same, cc <= rr)

    strict2, _ = lane_masks(2 * N)
    _, incl4 = lane_masks(4 * N)

    def bdiag(m):
        lo = lax.broadcasted_iota(jnp.int32, m.shape, 1) < N
        zero = jnp.zeros_like(m)
        return jnp.concatenate([jnp.where(lo, m, zero), jnp.where(lo, zero, m)], axis=0)

    units = [(gi, p) for gi in range(groups) for p in range(NP)]
    seqs = range(nseq)
    tile = lambda z, u: z[u[0] * R:(u[0] + 1) * R, 2 * N * u[1]:2 * N * (u[1] + 1)]
    rows = [slice(C * s, C * (s + 1)) for s in seqs]
    vp = {u: tile(v, u) for u in units}
    vpb = {u: vp[u].astype(BF) for u in units}
    btp = {u: tile(bt, u) for u in units}
    ktp = {u: tile(kt, u) for u in units}
    xa = {u: tile(at, u) for u in units}
    xr = {u: tile(rt, u) for u in units}
    S0 = {(u, s): s_scr[u[0] * nseq + s, u[1]] for u in units for s in seqs}

    xp = {u: jnp.concatenate([xa[u], xr[u]], axis=0).astype(BF) for u in units}
    bmat = {u: jnp.concatenate([bdiag(btp[u].astype(BF)), bdiag(ktp[u].astype(BF))], axis=0)
            for u in units}
    G = {u: lax.dot_general(xp[u], bmat[u], NT_DIMS, preferred_element_type=F32) for u in units}
    Z = {(u, s): lax.dot_general(
        xp[u] if nseq == 1 else
        jnp.concatenate([xa[u][rows[s]], xr[u][rows[s]]], axis=0).astype(BF),
        bdiag(S0[u, s].astype(BF)), NT_DIMS, preferred_element_type=F32) for u in units for s in seqs}
    za = {u: Z[u, 0][:C] if nseq == 1 else jnp.concatenate([Z[u, s][:C] for s in seqs], axis=0)
          for u in units}
    zr = {u: Z[u, 0][C:] if nseq == 1 else jnp.concatenate([Z[u, s][C:] for s in seqs], axis=0)
          for u in units}
    A = {u: jnp.where(strict2, G[u][:R, :2 * N], 0.0) for u in units}
    aak = {u: jnp.where(strict2, G[u][:R, 2 * N:], 0.0).astype(BF) for u in units}
    T2 = {u: jnp.where(incl4, G[u][R:], 0.0).astype(BF) for u in units}
    W = {u: za[u] + _dot(aak[u], bdiag(vpb[u])) for u in units}
    nsteps = max(1, int(math.log2(C)))
    for step in range(nsteps):
        Ab = {u: A[u].astype(BF) for u in units}
        if step + 1 < nsteps:
            res = {u: _dot(Ab[u], jnp.concatenate([bdiag(W[u].astype(BF)), bdiag(Ab[u])], axis=1))
                   for u in units}
            W = {u: W[u] + res[u][:, :2 * N] for u in units}
            A = {u: res[u][:, 2 * N:] for u in units}
        else:
            W = {u: W[u] + _dot(Ab[u], bdiag(W[u].astype(BF))) for u in units}
    wb = {u: W[u].astype(BF) for u in units}
    for u in units:
        uv = jnp.concatenate([bdiag(wb[u]), bdiag(vpb[u])], axis=0)
        o_ref[u[0], :, 2 * N * u[1]:2 * N * (u[1] + 1)] = zr[u] + _dot(T2[u], uv)
    lo = lax.broadcasted_iota(jnp.int32, (N, 2 * N), 1) < N
    for u in units:
        for s in seqs:
            uvs = jnp.concatenate([W[u][rows[s]], vp[u][rows[s]]], axis=0).astype(BF)
            ys = jnp.concatenate([btp[u][rows[s]], ktp[u][rows[s]]], axis=0).astype(BF)
            full = lax.dot_general(uvs, ys, TN_DIMS, preferred_element_type=F32)
            upd = jnp.where(lo, full[:N], full[N:])
            last = u[0] * R + C * (s + 1) - 1
            g_end = g_in[last:last + 1, 2 * N * u[1]:2 * N * (u[1] + 1)]
            s_scr[u[0] * nseq + s, u[1]] = (S0[u, s] + upd) * g_end

    @pl.when(t == pl.num_programs(1) - 1)
    def _():
        for s in range(groups * nseq):
            for p in range(NP):
                so_cur[s, 2 * p] = s_scr[s, p][:, :N]
                so_cur[s, 2 * p + 1] = s_scr[s, p][:, N:]
        if first:
            for lyr in range(1, so_ref.shape[0]):
                so_ref[lyr] = jnp.zeros(so_ref.shape[1:], F32)


def _wkv(r, logd, k, v, kk, a, s0, layer, nlayers, s_prev, C, groups):
    B, T, D = r.shape
    has_s0 = s0 is not None
    first = s_prev is None
    R = WKV_ROWS
    nseq = R // C
    nt = T // C
    assert (nseq == 1 or nt == 1) and T % C == 0
    nbk = B * T // (R * nt)
    assert nbk % groups == 0
    gs = groups * nseq
    blk = pl.BlockSpec((groups, None, R, D), lambda b, t: (b, t, 0, 0))
    sdims = (RWKV_HEADS, RWKV_HEAD, RWKV_HEAD)
    sblk = pl.BlockSpec((None, gs) + sdims, lambda b, t: (layer, b, 0, 0, 0))
    oblk = pl.BlockSpec((nlayers, gs) + sdims, lambda b, t: (0, b, 0, 0, 0)) if first else sblk
    ins = [x.reshape(nbk, nt, R, D) for x in (r, logd, k, v, kk, a)]
    in_specs = [blk] * 6
    if has_s0:
        ins.append(s0[layer])
        in_specs.append(pl.BlockSpec((gs,) + sdims, lambda b, t: (b, 0, 0, 0)))
    aliases = {}
    if not first:
        aliases = {len(ins): 1}
        ins.append(s_prev)
        in_specs.append(pl.BlockSpec(memory_space=pl.ANY))
    o, s_new = pl.pallas_call(
        functools.partial(_wkv_kernel, has_s0, first, C),
        grid=(nbk // groups, nt),
        in_specs=in_specs,
        out_specs=[blk, oblk],
        out_shape=[jax.ShapeDtypeStruct((nbk, nt, R, D), F32),
                   jax.ShapeDtypeStruct((nlayers, B) + sdims, F32)],
        scratch_shapes=[pltpu.VMEM((gs, RWKV_HEADS // 2, RWKV_HEAD, 2 * RWKV_HEAD), F32)],
        input_output_aliases=aliases,
        compiler_params=_cparams("parallel", "arbitrary"),
        name="wkv",
    )(*ins)
    return o.reshape(B, T, D), s_new


def _rwkv_out_kernel(o_ref, bonus_ref, gate_ref, x_ref, vec_ref, m_ref, wo_ref, g_ref, beta_ref, y_ref):
    ln_w, ln_b = (vec_ref[c:c + 1, :] for c in (V_LNW, V_LNB))
    o = o_ref[...]
    inv_n = 1.0 / RWKV_HEAD
    mu = _head_sum(o, m_ref) * inv_n
    oc = o - mu
    var = _head_sum(oc * oc, m_ref) * inv_n
    on = oc * lax.rsqrt(var + GN_EPS) * ln_w + ln_b
    y = _dot(((on + bonus_ref[...]) * gate_ref[...]).astype(BF), wo_ref[...])
    y_ref[...] = _layer_norm(ALPHA * x_ref[...] + y, g_ref[0:1, :], beta_ref[0:1, :])


def _rwkv_out(o, bonus, gate, x2, P, j, depth, tm=512):
    n = o.shape[0]
    tm = min(tm, n)
    rows = pl.BlockSpec((tm, D_MODEL), lambda i: (i, 0))
    vecs, w_o, ln_g, ln_b = P['rwkv_vecs'], P['rwkv_w_o'], P['ln_g'], P['ln_b']
    return pl.pallas_call(
        _rwkv_out_kernel,
        grid=(n // tm,),
        in_specs=[rows] * 4 + [_layer(vecs.shape, j), _full(P['head_ones'].shape), _layer(w_o.shape, j),
                               _layer(ln_g.shape, depth), _layer(ln_b.shape, depth)],
        out_specs=rows,
        out_shape=jax.ShapeDtypeStruct((n, D_MODEL), F32),
        compiler_params=_cparams("parallel"),
        name="rwkv_out",
    )(o, bonus, gate, x2, vecs, P['head_ones'], w_o, ln_g, ln_b)


def _trunk(x, win_k, win_v, shift, wkv, P, bias):
    prompt = win_k is None
    B, T, D = x.shape
    n = B * T
    nkv = N_KV_HEADS * HEAD_DIM
    n_rwkv = DEPTH // 2
    v_first = None
    nk, nv, ns = [], [], []
    nk_all = nv_all = s_all = None
    for i in range(DEPTH):
        j = i // 2
        x2 = x.reshape(n, D)
        if i % 2 == 0:
            if prompt:
                q, k, v = _qkv_proj(x2, P['attn_w_qkv'], P['attn_b_qkv'], j)
                q, k, v = q.reshape(B, T, D), k.reshape(B, T, nkv), v.reshape(B, T, nkv)
                o = _swa_prompt(q, k, v, bias, P['attn_sinks'][j])
                nk.append(k[:, T - WINDOW:].reshape(B, WINDOW, N_KV_HEADS, HEAD_DIM))
                nv.append(v[:, T - WINDOW:].reshape(B, WINDOW, N_KV_HEADS, HEAD_DIM))
            else:
                q, kvt = _qkv_proj_t(x2, P['attn_wq'], P['attn_wkvt'], P['attn_bq'], P['attn_bkvt'], j)
                o, nk_all, nv_all = _swa_sample(q.reshape(B, T, D), kvt, win_k, win_v, j, bias,
                                                P['attn_sinks'][j], nk_all, nv_all)
            x2 = _proj_ln(o.reshape(n, D), P['attn_w_o'], P['attn_b_o'], x2, P['ln_g'], P['ln_b'], j, i)
        else:
            x_prev = jnp.zeros((B, 1, D), x.dtype) if prompt else shift[j].reshape(B, 1, D)
            bb, tt = (1, 256) if prompt else (min(32, B), T)
            r, logd, k, v, kk, a, gate, bonus = _rwkv_proj(x, x_prev, v_first, P, j, bb, tt)
            if j == 0:
                v_first = v
            C, groups = (WKV_ROWS, math.gcd(B, 4)) if prompt else (T, 1)
            o, s_all = _wkv(r, logd, k, v, kk, a, wkv, j, n_rwkv, s_all, C, groups)
            ns.append(x[:, -1])
            x2 = _rwkv_out(o.reshape(n, D), bonus.reshape(n, D), gate.reshape(n, D), x2, P, j, i)
        x2 = _ffn(x2, P['ffn_w_gu'], P['ffn_w_down'], P['ln_g'], P['ln_b'], i)
        x = x2.reshape(B, T, D)
    if prompt:
        nk_all, nv_all = jnp.stack(nk), jnp.stack(nv)
    else:
        nk_all = jnp.transpose(nk_all, (0, 1, 4, 2, 3))
        nv_all = jnp.transpose(nv_all, (0, 1, 4, 2, 3))
    return x, nk_all, nv_all, jnp.stack(ns), s_all


def kernel(x_prompt, x_sample, cache_win_k, cache_win_v, state_shift, state_wkv, rel_bias, ln_g, ln_b, attn_w_qkv, attn_b_qkv, attn_w_o, attn_b_o, attn_sinks, rwkv_mix, rwkv_w_rkv, rwkv_w0, rwkv_w1, rwkv_w2, rwkv_a0, rwkv_a1, rwkv_a2, rwkv_v0, rwkv_v1, rwkv_v2, rwkv_g1, rwkv_g2, rwkv_k_k, rwkv_k_a, rwkv_r_k, rwkv_ln_w, rwkv_ln_b, rwkv_w_o, ffn_w_gu, ffn_w_down):
    bf = lambda w: w.astype(BF)
    lane_head = np.arange(MXU_WIDTH) // RWKV_HEAD
    head_ones = jnp.asarray(lane_head[:, None] == lane_head[None, :], dtype=BF)
    n_rwkv = rwkv_w0.shape[0]
    nq = N_HEADS * HEAD_DIM
    v0_rows = jnp.concatenate([jnp.zeros((1, D_MODEL), F32), rwkv_v0], axis=0)
    rwkv_vecs = jnp.stack([rwkv_w0, rwkv_a0, v0_rows, rwkv_k_k, rwkv_k_a,
                           rwkv_r_k.reshape(n_rwkv, D_MODEL), rwkv_ln_w, rwkv_ln_b], axis=1)
    w_qkv = bf(attn_w_qkv)
    b_qkv = attn_b_qkv[:, None, :]
    P = dict(ln_g=ln_g, ln_b=ln_b,
             attn_w_qkv=w_qkv, attn_b_qkv=b_qkv,
             attn_wq=w_qkv[:, :, :nq], attn_wkvt=jnp.transpose(w_qkv[:, :, nq:], (0, 2, 1)),
             attn_bq=b_qkv[:, :, :nq], attn_bkvt=jnp.transpose(b_qkv[:, :, nq:], (0, 2, 1)),
             attn_w_o=bf(attn_w_o), attn_b_o=attn_b_o[:, None, :], attn_sinks=attn_sinks,
             rwkv_mix=rwkv_mix, rwkv_vecs=rwkv_vecs, rwkv_w_rkv=bf(rwkv_w_rkv),
             rwkv_w1=bf(rwkv_w1), rwkv_w2=bf(rwkv_w2), rwkv_a1=bf(rwkv_a1), rwkv_a2=bf(rwkv_a2),
             rwkv_v1=bf(rwkv_v1), rwkv_v2=bf(rwkv_v2), rwkv_g1=bf(rwkv_g1), rwkv_g2=bf(rwkv_g2),
             rwkv_w_o=bf(rwkv_w_o), ffn_w_gu=bf(ffn_w_gu), ffn_w_down=bf(ffn_w_down),
             head_ones=head_ones)

    T = x_sample.shape[1]
    L = cache_win_k.shape[2]
    qi, kc = np.arange(WINDOW)[:, None], np.arange(WINDOW)[None, :]
    d_prompt = np.where(kc <= qi, qi - kc, qi + WINDOW - kc)
    d_sample = np.arange(T)[:, None] + L - np.arange(L + T)[None, :]
    bias_prompt = _bias_table(rel_bias, d_prompt)
    bias_sample = _bias_table(rel_bias, d_sample)

    y_prompt, pk, pv, ps, pw = _trunk(x_prompt, None, None, None, None, P, bias_prompt)
    win_k = jnp.transpose(cache_win_k, (0, 1, 3, 4, 2))
    win_v = jnp.transpose(cache_win_v, (0, 1, 3, 4, 2))
    y_sample, sk, sv, ss, sw = _trunk(x_sample, win_k, win_v, state_shift, state_wkv,
                                      P, bias_sample)
    return (y_prompt, y_sample, pk, pv, ps, pw, sk, sv, ss, sw)
```

```python
import functools
import math

import numpy as np
import jax
import jax.numpy as jnp
from jax import lax
from jax.experimental import pallas as pl
from jax.experimental.pallas import tpu as pltpu

D_MODEL = 1024
DEPTH = 4
HEAD_DIM = 64
N_HEADS = 16
N_KV_HEADS = 4
GROUP = 4
WINDOW = 128
N_BUCKETS = 32
MAX_DISTANCE = 128
RWKV_HEAD = 64
RWKV_HEADS = 16
GN_EPS = 64e-5
D_FF = 2816
ALPHA = (2 * DEPTH) ** 0.25
LN_EPS = 1e-5
NEG = -1e30
MXU_WIDTH = 256
WKV_ROWS = 64

BF = jnp.bfloat16
F32 = jnp.float32
VMEM_LIMIT = 56 * 1024 * 1024

NT_DIMS = (((1,), (1,)), ((), ()))
TN_DIMS = (((0,), (0,)), ((), ()))

V_W0, V_A0, V_V0, V_KK, V_KA, V_RK, V_LNW, V_LNB = range(8)


def _cparams(*sem):
    return pltpu.CompilerParams(dimension_semantics=sem, vmem_limit_bytes=VMEM_LIMIT)


def _dot(a, b):
    return jnp.dot(a, b, preferred_element_type=F32)


def _layer_norm(z, g, b):
    mu = jnp.mean(z, axis=-1, keepdims=True)
    zc = z - mu
    var = jnp.mean(zc * zc, axis=-1, keepdims=True)
    return zc * lax.rsqrt(var + LN_EPS) * g + b


def _sigmoid(z):
    return 1.0 / (1.0 + jnp.exp(-z))


def _split2(z):
    hi = z.astype(BF)
    lo = (z - hi.astype(F32)).astype(BF)
    return hi, lo


def _split3(z):
    hi = z.astype(BF)
    r1 = z - hi.astype(F32)
    mid = r1.astype(BF)
    lo = (r1 - mid.astype(F32)).astype(BF)
    return hi, mid, lo


def _full(shape):
    n = len(shape)
    return pl.BlockSpec(shape, lambda *_: (0,) * n)


def _layer(shape, *lead):
    k = len(lead)
    rest = len(shape) - k
    return pl.BlockSpec((None,) * k + tuple(shape[k:]), lambda *_: tuple(lead) + (0,) * rest,
                        pipeline_mode=pl.Buffered(1))


def _bias_table_kernel(rb_ref, bucket_ref, valid_ref, o_ref):
    bucket = bucket_ref[...]
    valid = valid_ref[...] > 0
    for h in range(N_HEADS):
        acc = jnp.zeros(bucket.shape, F32)
        for b in range(N_BUCKETS):
            acc = jnp.where(bucket == b, rb_ref[b, h], acc)
        o_ref[h] = jnp.where(valid, acc, NEG)


def _bias_table(rel_bias, d):
    dc = np.maximum(d, 0)
    exact = N_BUCKETS // 2
    df = np.maximum(dc, 1).astype(np.float32)
    large = exact + (np.log(df / np.float32(exact)) / np.float32(math.log(MAX_DISTANCE / exact))
                     * np.float32(N_BUCKETS - exact)).astype(np.int32)
    bucket = np.where(dc < exact, dc, np.minimum(large, N_BUCKETS - 1)).astype(np.int32)
    valid = ((d >= 0) & (d < WINDOW)).astype(np.int32)
    return pl.pallas_call(
        _bias_table_kernel,
        out_shape=jax.ShapeDtypeStruct((N_HEADS,) + d.shape, F32),
        in_specs=[pl.BlockSpec(memory_space=pltpu.SMEM),
                  pl.BlockSpec(memory_space=pltpu.VMEM),
                  pl.BlockSpec(memory_space=pltpu.VMEM)],
        out_specs=pl.BlockSpec(memory_space=pltpu.VMEM),
        name="bias_table",
    )(rel_bias, jnp.asarray(bucket), jnp.asarray(valid))


def _qkv_kernel(x_ref, w_ref, b_ref, q_ref, k_ref, v_ref):
    acc = _dot(x_ref[...].astype(BF), w_ref[...]) + b_ref[...]
    nq = N_HEADS * HEAD_DIM
    nk = N_KV_HEADS * HEAD_DIM
    q_ref[...] = (acc[:, :nq] * HEAD_DIM ** -0.5).astype(q_ref.dtype)
    k_ref[...] = acc[:, nq:nq + nk]
    v_ref[...] = acc[:, nq + nk:]


def _qkv_proj(x2, w, b, layer, tm=1024):
    n = x2.shape[0]
    tm = min(tm, n)
    nq = N_HEADS * HEAD_DIM
    nk = N_KV_HEADS * HEAD_DIM
    return pl.pallas_call(
        _qkv_kernel,
        grid=(n // tm,),
        in_specs=[pl.BlockSpec((tm, D_MODEL), lambda i: (i, 0)),
                  _layer(w.shape, layer), _layer(b.shape, layer)],
        out_specs=[pl.BlockSpec((tm, nq), lambda i: (i, 0)),
                   pl.BlockSpec((tm, nk), lambda i: (i, 0)),
                   pl.BlockSpec((tm, nk), lambda i: (i, 0))],
        out_shape=[jax.ShapeDtypeStruct((n, nq), BF),
                   jax.ShapeDtypeStruct((n, nk), F32),
                   jax.ShapeDtypeStruct((n, nk), F32)],
        compiler_params=_cparams("parallel"),
        name="qkv_proj",
    )(x2, w, b)


def _swa_prompt_kernel(sink_ref, q_ref, kc_ref, kp_ref, vc_ref, vp_ref, bias_ref, o_ref):
    step = pl.program_id(1)
    W = WINDOW
    nsub = q_ref.shape[1] // W
    kall = jnp.concatenate([kp_ref[0], kc_ref[0]], axis=0).astype(BF)
    vall = jnp.concatenate([vp_ref[0], vc_ref[0]], axis=0).astype(BF)
    row = lax.broadcasted_iota(jnp.int32, (W, W), 0)
    col = lax.broadcasted_iota(jnp.int32, (W, W), 1)
    own = col <= row
    zero = jnp.zeros((W, W), BF)
    subs = range(nsub)
    scores = {}
    for sub in subs:
        for h in range(N_HEADS):
            g = h // GROUP
            qh = q_ref[0, sub * W:(sub + 1) * W, h * HEAD_DIM:(h + 1) * HEAD_DIM]
            kh = kall[sub * W:(sub + 2) * W, g * HEAD_DIM:(g + 1) * HEAD_DIM]
            s2 = lax.dot_general(qh, kh, NT_DIMS, preferred_element_type=F32)
            s_prev = s2[:, :W]
            if sub == 0:
                s_prev = jnp.where(step == 0, NEG, s_prev)
            scores[sub, h] = jnp.where(own, s2[:, W:], s_prev) + bias_ref[h]
    m, pb = {}, {}
    for sub in subs:
        for g in range(N_KV_HEADS):
            s = jnp.concatenate([scores[sub, g * GROUP + j] for j in range(GROUP)], axis=0)
            mg = jnp.max(s, axis=-1, keepdims=True)
            m[sub, g] = mg
            pb[sub, g] = jnp.exp(s - mg).astype(BF)
    ones = jnp.ones((2 * W, HEAD_DIM), BF)
    lo = lax.broadcasted_iota(jnp.int32, (W, 2 * HEAD_DIM), 1) < HEAD_DIM
    for sub in subs:
        for g in range(N_KV_HEADS):
            vh = vall[sub * W:(sub + 2) * W, g * HEAD_DIM:(g + 1) * HEAD_DIM]
            vext = jnp.concatenate([vh, ones, ones, vh], axis=1)
            res, esink = [], []
            for j in range(GROUP):
                hs = slice(j * W, (j + 1) * W)
                ph = pb[sub, g][hs]
                p2 = jnp.concatenate([jnp.where(own, zero, ph), jnp.where(own, ph, zero)], axis=1)
                res.append(_dot(p2, vext))
                esink.append(jnp.exp(sink_ref[g * GROUP + j] - m[sub, g][hs]))
            for j in range(0, GROUP, 2):
                o_pair = jnp.where(lo, res[j][:, :2 * HEAD_DIM], res[j + 1][:, 2 * HEAD_DIM:])
                t_pair = jnp.where(lo, res[j][:, 2 * HEAD_DIM:], res[j + 1][:, :2 * HEAD_DIM])
                den = t_pair + jnp.where(lo, esink[j], esink[j + 1])
                c0 = (g * GROUP + j) * HEAD_DIM
                o_ref[0, sub * W:(sub + 1) * W, c0:c0 + 2 * HEAD_DIM] = (o_pair / den).astype(o_ref.dtype)


def _swa_prompt(q, k, v, bias, sinks, nsub=4):
    B, T, _ = q.shape
    nb = T // WINDOW
    nsub = math.gcd(nb, nsub)
    nk = N_KV_HEADS * HEAD_DIM
    cur = lambda b, i: (b, i, 0)
    prev = lambda b, i: (b, jnp.maximum(i * nsub - 1, 0), 0)
    return pl.pallas_call(
        _swa_prompt_kernel,
        grid=(B, nb // nsub),
        in_specs=[pl.BlockSpec(memory_space=pltpu.SMEM),
                  pl.BlockSpec((1, nsub * WINDOW, D_MODEL), cur),
                  pl.BlockSpec((1, nsub * WINDOW, nk), cur),
                  pl.BlockSpec((1, WINDOW, nk), prev),
                  pl.BlockSpec((1, nsub * WINDOW, nk), cur),
                  pl.BlockSpec((1, WINDOW, nk), prev),
                  _full(bias.shape)],
        out_specs=pl.BlockSpec((1, nsub * WINDOW, D_MODEL), cur),
        out_shape=jax.ShapeDtypeStruct((B, T, D_MODEL), BF),
        compiler_params=_cparams("parallel", "parallel"),
        name="swa_prompt",
    )(sinks, q, k, k, v, v, bias)


def _qkv_t_kernel(x_ref, wq_ref, wkvt_ref, bq_ref, bkvt_ref, q_ref, kvt_ref):
    xb = x_ref[...].astype(BF)
    q_ref[...] = (_dot(xb, wq_ref[...]) + bq_ref[...]) * HEAD_DIM ** -0.5
    kvt_ref[...] = lax.dot_general(wkvt_ref[...], xb, NT_DIMS, preferred_element_type=F32) + bkvt_ref[...]


def _qkv_proj_t(x2, wq, wkvt, bq, bkvt, layer, tm=256):
    n = x2.shape[0]
    tm = min(tm, n)
    nq = N_HEADS * HEAD_DIM
    nkv = 2 * N_KV_HEADS * HEAD_DIM
    return pl.pallas_call(
        _qkv_t_kernel,
        grid=(n // tm,),
        in_specs=[pl.BlockSpec((tm, D_MODEL), lambda i: (i, 0)),
                  _layer(wq.shape, layer), _layer(wkvt.shape, layer),
                  _layer(bq.shape, layer), _layer(bkvt.shape, layer)],
        out_specs=[pl.BlockSpec((tm, nq), lambda i: (i, 0)),
                   pl.BlockSpec((nkv, tm), lambda i: (0, i))],
        out_shape=[jax.ShapeDtypeStruct((n, nq), F32),
                   jax.ShapeDtypeStruct((nkv, n), F32)],
        compiler_params=_cparams("parallel"),
        name="qkv_proj_t",
    )(x2, wq, wkvt, bq, bkvt)


def _swa_sample_kernel(T, first, sink_ref, q_ref, kvt_ref, ck_ref, cv_ref, bc_ref, bn_ref, *rest):
    o_ref, nk_ref, nv_ref = rest[-3:]
    bb = q_ref.shape[0]
    L = ck_ref.shape[-1]
    HD = HEAD_DIM
    if first:
        for lyr in range(1, nk_ref.shape[0]):
            nk_ref[lyr] = jnp.zeros(nk_ref.shape[1:], F32)
            nv_ref[lyr] = jnp.zeros(nv_ref.shape[1:], F32)
        nk_out, nv_out = nk_ref.at[0], nv_ref.at[0]
    else:
        nk_out, nv_out = nk_ref, nv_ref
    units = [(b, g) for b in range(bb) for g in range(N_KV_HEADS)]
    lane = lax.broadcasted_iota(jnp.int32, (HD, L), 1)
    knew = [kvt_ref[g * HD:(g + 1) * HD, :] for g in range(N_KV_HEADS)]
    vnew = [kvt_ref[(N_KV_HEADS + g) * HD:(N_KV_HEADS + g + 1) * HD, :] for g in range(N_KV_HEADS)]

    def shift_in(old, new, b):
        moved = pltpu.roll(new, (L - T - b * T) % L, 1)
        return jnp.where(lane >= L - T, moved, pltpu.roll(old, L - T, 1))

    sc, sn = {}, {}
    for b in range(bb):
        q = q_ref[b]
        for g in range(N_KV_HEADS):
            qg = jnp.concatenate([q[:, (g * GROUP + j) * HD:(g * GROUP + j + 1) * HD]
                                  for j in range(GROUP)], axis=0).astype(BF)
            kc = ck_ref[b, g]
            nk_out[b, g] = shift_in(kc, knew[g], b)
            sc[b, g] = _dot(qg, kc.astype(BF)) + bc_ref[g]
            sn[b, g] = _dot(qg, knew[g][:, b * T:(b + 1) * T].astype(BF)) + bn_ref[g]
    sc_all = jnp.concatenate([sc[u] for u in units], axis=0)
    sn_all = jnp.concatenate([sn[u] for u in units], axis=0)
    m = jnp.maximum(jnp.max(sc_all, axis=-1, keepdims=True), jnp.max(sn_all, axis=-1, keepdims=True))
    pc_all = jnp.exp(sc_all - m)
    pn_all = jnp.exp(sn_all - m)
    tot = jnp.sum(pc_all, axis=-1, keepdims=True) + jnp.sum(pn_all, axis=-1, keepdims=True)
    pc_all = pc_all.astype(BF)
    pn_all = pn_all.astype(BF)
    gt = GROUP * T
    for b in range(bb):
        pieces = []
        for g in range(N_KV_HEADS):
            r0 = (b * N_KV_HEADS + g) * gt
            vc = cv_ref[b, g]
            nv_out[b, g] = shift_in(vc, vnew[g], b)
            og = (lax.dot_general(pc_all[r0:r0 + gt], vc.astype(BF), NT_DIMS, preferred_element_type=F32)
                  + lax.dot_general(pn_all[r0:r0 + gt], vnew[g][:, b * T:(b + 1) * T].astype(BF),
                                    NT_DIMS, preferred_element_type=F32))
            for j in range(GROUP):
                rj = slice(r0 + j * T, r0 + (j + 1) * T)
                inv = 1.0 / (tot[rj] + jnp.exp(sink_ref[g * GROUP + j] - m[rj]))
                pieces.append(og[j * T:(j + 1) * T] * inv)
        o_ref[b] = jnp.concatenate(pieces, axis=-1).astype(o_ref.dtype)


def _swa_sample(q, kvt, ck, cv, layer, bias, sinks, nk_prev, nv_prev):
    B, T, _ = q.shape
    nl, L = ck.shape[0], ck.shape[-1]
    first = nk_prev is None
    bb = L // T
    assert B % bb == 0 and L % T == 0
    bias = bias.reshape(N_KV_HEADS, GROUP * T, L + T)
    bc, bn = bias[:, :, :L], bias[:, :, L:]
    qblk = pl.BlockSpec((bb, T, D_MODEL), lambda i: (i, 0, 0))
    cblk = pl.BlockSpec((None, bb, N_KV_HEADS, HEAD_DIM, L), lambda i: (layer, i, 0, 0, 0))
    nblk = (pl.BlockSpec((nl, bb, N_KV_HEADS, HEAD_DIM, L), lambda i: (0, i, 0, 0, 0)) if first else cblk)
    ins = [sinks, q, kvt, ck, cv, bc, bn]
    in_specs = [pl.BlockSpec(memory_space=pltpu.SMEM), qblk,
                pl.BlockSpec((kvt.shape[0], bb * T), lambda i: (0, i)),
                cblk, cblk, _full(bc.shape), _full(bn.shape)]
    aliases = {}
    if not first:
        aliases = {len(ins): 1, len(ins) + 1: 2}
        ins += [nk_prev, nv_prev]
        in_specs += [pl.BlockSpec(memory_space=pl.ANY)] * 2
    return pl.pallas_call(
        functools.partial(_swa_sample_kernel, T, first),
        grid=(B // bb,),
        in_specs=in_specs,
        out_specs=[qblk, nblk, nblk],
        out_shape=[jax.ShapeDtypeStruct((B, T, D_MODEL), BF),
                   jax.ShapeDtypeStruct((nl, B, N_KV_HEADS, HEAD_DIM, L), F32),
                   jax.ShapeDtypeStruct((nl, B, N_KV_HEADS, HEAD_DIM, L), F32)],
        input_output_aliases=aliases,
        compiler_params=_cparams("parallel"),
        name="swa_sample",
    )(*ins)


def _proj_ln_kernel(which, a_ref, w_ref, b_ref, x_ref, g_ref, beta_ref, o_ref):
    y = _dot(a_ref[...].astype(BF), w_ref[...]) + b_ref[...]
    o_ref[...] = _layer_norm(ALPHA * x_ref[...] + y, g_ref[which:which + 1, :], beta_ref[which:which + 1, :])


def _proj_ln(a, w, b, x2, ln_g, ln_b, layer, depth, tm=1024):
    n, kdim = a.shape
    tm = min(tm, n)
    rows = lambda i: (i, 0)
    return pl.pallas_call(
        functools.partial(_proj_ln_kernel, 0),
        grid=(n // tm,),
        in_specs=[pl.BlockSpec((tm, kdim), rows), _layer(w.shape, layer), _layer(b.shape, layer),
                  pl.BlockSpec((tm, D_MODEL), rows), _layer(ln_g.shape, depth), _layer(ln_b.shape, depth)],
        out_specs=pl.BlockSpec((tm, D_MODEL), rows),
        out_shape=jax.ShapeDtypeStruct((n, D_MODEL), F32),
        compiler_params=_cparams("parallel"),
        name="proj_ln",
    )(a, w, b, x2, ln_g, ln_b)


def _ffn_kernel(tf, x_ref, wgu_ref, wd_ref, g_ref, beta_ref, o_ref):
    x = x_ref[...]
    xb = x.astype(BF)
    acc = None
    for c in range(0, D_FF, tf):
        e = min(c + tf, D_FF)
        gt = _dot(xb, wgu_ref[:, c:e])
        ut = _dot(xb, wgu_ref[:, D_FF + c:D_FF + e])
        h = (gt * _sigmoid(gt) * ut).astype(BF)
        part = _dot(h, wd_ref[c:e, :])
        acc = part if acc is None else acc + part
    o_ref[...] = _layer_norm(ALPHA * x + acc, g_ref[1:2, :], beta_ref[1:2, :])


def _ffn(x2, w_gu, w_down, ln_g, ln_b, depth, tm=512, tf=6 * MXU_WIDTH):
    n = x2.shape[0]
    tm = min(tm, n)
    assert D_FF % MXU_WIDTH == 0 and tf % MXU_WIDTH == 0
    rows = lambda i: (i, 0)
    return pl.pallas_call(
        functools.partial(_ffn_kernel, tf),
        grid=(n // tm,),
        in_specs=[pl.BlockSpec((tm, D_MODEL), rows),
                  _layer(w_gu.shape, depth), _layer(w_down.shape, depth),
                  _layer(ln_g.shape, depth), _layer(ln_b.shape, depth)],
        out_specs=pl.BlockSpec((tm, D_MODEL), rows),
        out_shape=jax.ShapeDtypeStruct((n, D_MODEL), F32),
        compiler_params=_cparams("parallel"),
        name="ffn",
    )(x2, w_gu, w_down, ln_g, ln_b)


def _head_sum(z, m_ref):
    hi, lo = _split2(z)
    m = m_ref[...]
    w = m.shape[0]
    return jnp.concatenate([_dot(hi[:, c:c + w], m) + _dot(lo[:, c:c + w], m)
                            for c in range(0, z.shape[1], w)], axis=1)


def _rwkv_proj_kernel(has_vres, *refs):
    if has_vres:
        (x_ref, xp_ref, mix_ref, vec_ref, wr_ref, wk_ref, wv_ref, w1_ref, w2_ref, a1_ref, a2_ref,
         g1_ref, g2_ref, m_ref, v1_ref, v2_ref, vf_ref,
         r_ref, w_ref, k_ref, v_ref, kk_ref, a_ref, g_ref, bonus_ref, carry_ref) = refs
    else:
        (x_ref, xp_ref, mix_ref, vec_ref, wr_ref, wk_ref, wv_ref, w1_ref, w2_ref, a1_ref, a2_ref,
         g1_ref, g2_ref, m_ref,
         r_ref, w_ref, k_ref, v_ref, kk_ref, a_ref, g_ref, bonus_ref, carry_ref) = refs
    bb, tt, D = x_ref.shape
    n = bb * tt
    t = pl.program_id(1)
    x3 = x_ref[...]
    prev = jnp.where(t == 0, xp_ref[...], carry_ref[...])
    carry_ref[...] = x3[:, tt - 1:tt, :]
    x = x3.reshape(n, D)
    prev_rows = jnp.broadcast_to(prev, (bb, tt, D)).reshape(n, D)
    row = lax.broadcasted_iota(jnp.int32, (n, D), 0)
    xs = jnp.where(row % tt == 0, prev_rows, pltpu.roll(x, 1, 0))
    xx = xs - x
    xr, xw, xk, xv, xa, xg = ((x + xx * mix_ref[c:c + 1, :]).astype(BF) for c in range(6))
    w0, a0, v0, k_k, k_a = (vec_ref[c:c + 1, :] for c in (V_W0, V_A0, V_V0, V_KK, V_KA))

    a = _sigmoid(a0 + _dot(_dot(xa, a1_ref[...]).astype(BF), a2_ref[...]))
    k = _dot(xk, wk_ref[...])
    r = _dot(xr, wr_ref[...])
    kk = k * k_k
    kk = kk * jnp.minimum(lax.rsqrt(_head_sum(kk * kk, m_ref)), 1e12)
    k = k * (1.0 + (a - 1.0) * k_a)
    v = _dot(xv, wv_ref[...])
    if has_vres:
        gate = _sigmoid(v0 + _dot(_dot(xv, v1_ref[...]).astype(BF), v2_ref[...]))
        v = v + (vf_ref[...].reshape(n, D) - v) * gate
    rk_sum = _head_sum(r * k * vec_ref[V_RK:V_RK + 1, :], m_ref)
    z = w0 + _dot(jnp.tanh(_dot(xw, w1_ref[...])).astype(BF), w2_ref[...])
    logd = -math.exp(-0.5) * _sigmoid(z)
    g = _dot(_sigmoid(_dot(xg, g1_ref[...])).astype(BF), g2_ref[...])
    bonus = rk_sum * v
    for ref, val in ((r_ref, r), (w_ref, logd), (k_ref, k), (v_ref, v), (kk_ref, kk), (a_ref, a),
                     (g_ref, g), (bonus_ref, bonus)):
        ref[...] = val.reshape(bb, tt, D)


def _rwkv_proj(x, x_prev, v_first, P, j, bb, tt):
    B, T, D = x.shape
    has_vres = j > 0
    blk = pl.BlockSpec((bb, tt, D), lambda b, t: (b, t, 0))
    w_rkv = P['rwkv_w_rkv']
    ins = [x, x_prev, P['rwkv_mix'], P['rwkv_vecs'], w_rkv, w_rkv, w_rkv,
           P['rwkv_w1'], P['rwkv_w2'], P['rwkv_a1'], P['rwkv_a2'], P['rwkv_g1'], P['rwkv_g2'],
           P['head_ones']]
    in_specs = [blk, pl.BlockSpec((bb, 1, D), lambda b, t: (b, 0, 0)),
                _layer(ins[2].shape, j), _layer(ins[3].shape, j),
                _layer(w_rkv.shape, j, 0), _layer(w_rkv.shape, j, 1), _layer(w_rkv.shape, j, 2)]
    in_specs += [_layer(a.shape, j) for a in ins[7:13]] + [_full(P['head_ones'].shape)]
    if has_vres:
        ins += [P['rwkv_v1'], P['rwkv_v2'], v_first]
        in_specs += [_layer(P['rwkv_v1'].shape, j - 1), _layer(P['rwkv_v2'].shape, j - 1), blk]
    return pl.pallas_call(
        functools.partial(_rwkv_proj_kernel, has_vres),
        grid=(B // bb, T // tt),
        in_specs=in_specs,
        out_specs=[blk] * 8,
        out_shape=[jax.ShapeDtypeStruct((B, T, D), F32)] * 8,
        scratch_shapes=[pltpu.VMEM((bb, 1, D), F32)],
        compiler_params=_cparams("parallel", "arbitrary"),
        name="rwkv_proj",
    )(*ins)


def _wkv_kernel(has_s0, first, C, *refs):
    r_ref, w_ref, k_ref, v_ref, kk_ref, a_ref = refs[:6]
    s0_ref = refs[6] if has_s0 else None
    o_ref, so_ref, s_scr = refs[-3:]
    groups = r_ref.shape[0]
    R = WKV_ROWS
    N = RWKV_HEAD
    NP = RWKV_HEADS // 2
    nseq = R // C
    t = pl.program_id(1)
    so_cur = so_ref.at[0] if first else so_ref

    @pl.when(t == 0)
    def _():
        if has_s0:
            for s in range(groups * nseq):
                for p in range(NP):
                    s_scr[s, p] = jnp.concatenate([s0_ref[s, 2 * p], s0_ref[s, 2 * p + 1]], axis=1)
        else:
            s_scr[...] = jnp.zeros_like(s_scr)

    D = r_ref.shape[-1]
    GR = groups * R
    flat = lambda ref: ref[...].reshape(GR, D)
    logd = flat(w_ref)
    ri = lax.broadcasted_iota(jnp.int32, (GR, GR), 0)
    ci = lax.broadcasted_iota(jnp.int32, (GR, GR), 1)
    tril = jnp.logical_and(ri // C == ci // C, ci <= ri).astype(BF)
    cum = sum(_dot(tril, part) for part in _split3(logd))
    g_in = jnp.exp(cum)
    g_ex = jnp.exp(cum - logd)
    g_inv = jnp.exp(-cum)
    kk = flat(kk_ref)
    at = -(kk * g_ex)
    rt = flat(r_ref) * g_in
    bt = kk * flat(a_ref) * g_inv
    kt = flat(k_ref) * g_inv
    v = flat(v_ref)

    def lane_masks(width):
        rr = lax.broadcasted_iota(jnp.int32, (R, width), 0)
        cc = lax.broadcasted_iota(jnp.int32, (R, width), 1) % N
        same = rr // C == cc // C
        return jnp.logical_and(same, cc < rr), jnp.logical_and(same, cc <= rr)

    strict2, _ = lane_masks(2 * N)
    _, incl4 = lane_masks(4 * N)

    def bdiag(m):
        lo = lax.broadcasted_iota(jnp.int32, m.shape, 1) < N
        zero = jnp.zeros_like(m)
        return jnp.concatenate([jnp.where(lo, m, zero), jnp.where(lo, zero, m)], axis=0)

    units = [(gi, p) for gi in range(groups) for p in range(NP)]
    seqs = range(nseq)
    tile = lambda z, u: z[u[0] * R:(u[0] + 1) * R, 2 * N * u[1]:2 * N * (u[1] + 1)]
    rows = [slice(C * s, C * (s + 1)) for s in seqs]
    vp = {u: tile(v, u) for u in units}
    vpb = {u: vp[u].astype(BF) for u in units}
    btp = {u: tile(bt, u) for u in units}
    ktp = {u: tile(kt, u) for u in units}
    xa = {u: tile(at, u) for u in units}
    xr = {u: tile(rt, u) for u in units}
    S0 = {(u, s): s_scr[u[0] * nseq + s, u[1]] for u in units for s in seqs}

    xp = {u: jnp.concatenate([xa[u], xr[u]], axis=0).astype(BF) for u in units}
    bmat = {u: jnp.concatenate([bdiag(btp[u].astype(BF)), bdiag(ktp[u].astype(BF))], axis=0)
            for u in units}
    G = {u: lax.dot_general(xp[u], bmat[u], NT_DIMS, preferred_element_type=F32) for u in units}
    Z = {(u, s): lax.dot_general(
        xp[u] if nseq == 1 else
        jnp.concatenate([xa[u][rows[s]], xr[u][rows[s]]], axis=0).astype(BF),
        bdiag(S0[u, s].astype(BF)), NT_DIMS, preferred_element_type=F32) for u in units for s in seqs}
    za = {u: Z[u, 0][:C] if nseq == 1 else jnp.concatenate([Z[u, s][:C] for s in seqs], axis=0)
          for u in units}
    zr = {u: Z[u, 0][C:] if nseq == 1 else jnp.concatenate([Z[u, s][C:] for s in seqs], axis=0)
          for u in units}
    A = {u: jnp.where(strict2, G[u][:R, :2 * N], 0.0) for u in units}
    aak = {u: jnp.where(strict2, G[u][:R, 2 * N:], 0.0).astype(BF) for u in units}
    T2 = {u: jnp.where(incl4, G[u][R:], 0.0).astype(BF) for u in units}
    W = {u: za[u] + _dot(aak[u], bdiag(vpb[u])) for u in units}
    nsteps = max(1, int(math.log2(C)))
    for step in range(nsteps):
        Ab = {u: A[u].astype(BF) for u in units}
        if step + 1 < nsteps:
            res = {u: _dot(Ab[u], jnp.concatenate([bdiag(W[u].astype(BF)), bdiag(Ab[u])], axis=1))
                   for u in units}
            W = {u: W[u] + res[u][:, :2 * N] for u in units}
            A = {u: res[u][:, 2 * N:] for u in units}
        else:
            W = {u: W[u] + _dot(Ab[u], bdiag(W[u].astype(BF))) for u in units}
    wb = {u: W[u].astype(BF) for u in units}
    for u in units:
        uv = jnp.concatenate([bdiag(wb[u]), bdiag(vpb[u])], axis=0)
        o_ref[u[0], :, 2 * N * u[1]:2 * N * (u[1] + 1)] = zr[u] + _dot(T2[u], uv)
    lo = lax.broadcasted_iota(jnp.int32, (N, 2 * N), 1) < N
    for u in units:
        for s in seqs:
            uvs = jnp.concatenate([W[u][rows[s]], vp[u][rows[s]]], axis=0).astype(BF)
            ys = jnp.concatenate([btp[u][rows[s]], ktp[u][rows[s]]], axis=0).astype(BF)
            full = lax.dot_general(uvs, ys, TN_DIMS, preferred_element_type=F32)
            upd = jnp.where(lo, full[:N], full[N:])
            last = u[0] * R + C * (s + 1) - 1
            g_end = g_in[last:last + 1, 2 * N * u[1]:2 * N * (u[1] + 1)]
            s_scr[u[0] * nseq + s, u[1]] = (S0[u, s] + upd) * g_end

    @pl.when(t == pl.num_programs(1) - 1)
    def _():
        for s in range(groups * nseq):
            for p in range(NP):
                so_cur[s, 2 * p] = s_scr[s, p][:, :N]
                so_cur[s, 2 * p + 1] = s_scr[s, p][:, N:]
        if first:
            for lyr in range(1, so_ref.shape[0]):
                so_ref[lyr] = jnp.zeros(so_ref.shape[1:], F32)


def _wkv(r, logd, k, v, kk, a, s0, layer, nlayers, s_prev, C, groups):
    B, T, D = r.shape
    has_s0 = s0 is not None
    first = s_prev is None
    R = WKV_ROWS
    nseq = R // C
    nt = T // C
    assert (nseq == 1 or nt == 1) and T % C == 0
    nbk = B * T // (R * nt)
    assert nbk % groups == 0
    gs = groups * nseq
    blk = pl.BlockSpec((groups, None, R, D), lambda b, t: (b, t, 0, 0))
    sdims = (RWKV_HEADS, RWKV_HEAD, RWKV_HEAD)
    sblk = pl.BlockSpec((None, gs) + sdims, lambda b, t: (layer, b, 0, 0, 0))
    oblk = pl.BlockSpec((nlayers, gs) + sdims, lambda b, t: (0, b, 0, 0, 0)) if first else sblk
    ins = [x.reshape(nbk, nt, R, D) for x in (r, logd, k, v, kk, a)]
    in_specs = [blk] * 6
    if has_s0:
        ins.append(s0[layer])
        in_specs.append(pl.BlockSpec((gs,) + sdims, lambda b, t: (b, 0, 0, 0)))
    aliases = {}
    if not first:
        aliases = {len(ins): 1}
        ins.append(s_prev)
        in_specs.append(pl.BlockSpec(memory_space=pl.ANY))
    o, s_new = pl.pallas_call(
        functools.partial(_wkv_kernel, has_s0, first, C),
        grid=(nbk // groups, nt),
        in_specs=in_specs,
        out_specs=[blk, oblk],
        out_shape=[jax.ShapeDtypeStruct((nbk, nt, R, D), F32),
                   jax.ShapeDtypeStruct((nlayers, B) + sdims, F32)],
        scratch_shapes=[pltpu.VMEM((gs, RWKV_HEADS // 2, RWKV_HEAD, 2 * RWKV_HEAD), F32)],
        input_output_aliases=aliases,
        compiler_params=_cparams("parallel", "arbitrary"),
        name="wkv",
    )(*ins)
    return o.reshape(B, T, D), s_new


def _rwkv_out_kernel(o_ref, bonus_ref, gate_ref, x_ref, vec_ref, m_ref, wo_ref, g_ref, beta_ref, y_ref):
    ln_w, ln_b = (vec_ref[c:c + 1, :] for c in (V_LNW, V_LNB))
    o = o_ref[...]
    inv_n = 1.0 / RWKV_HEAD
    mu = _head_sum(o, m_ref) * inv_n
    oc = o - mu
    var = _head_sum(oc * oc, m_ref) * inv_n
    on = oc * lax.rsqrt(var + GN_EPS) * ln_w + ln_b
    y = _dot(((on + bonus_ref[...]) * gate_ref[...]).astype(BF), wo_ref[...])
    y_ref[...] = _layer_norm(ALPHA * x_ref[...] + y, g_ref[0:1, :], beta_ref[0:1, :])


def _rwkv_out(o, bonus, gate, x2, P, j, depth, tm=512):
    n = o.shape[0]
    tm = min(tm, n)
    rows = pl.BlockSpec((tm, D_MODEL), lambda i: (i, 0))
    vecs, w_o, ln_g, ln_b = P['rwkv_vecs'], P['rwkv_w_o'], P['ln_g'], P['ln_b']
    return pl.pallas_call(
        _rwkv_out_kernel,
        grid=(n // tm,),
        in_specs=[rows] * 4 + [_layer(vecs.shape, j), _full(P['head_ones'].shape), _layer(w_o.shape, j),
                               _layer(ln_g.shape, depth), _layer(ln_b.shape, depth)],
        out_specs=rows,
        out_shape=jax.ShapeDtypeStruct((n, D_MODEL), F32),
        compiler_params=_cparams("parallel"),
        name="rwkv_out",
    )(o, bonus, gate, x2, vecs, P['head_ones'], w_o, ln_g, ln_b)


def _trunk(x, win_k, win_v, shift, wkv, P, bias):
    prompt = win_k is None
    B, T, D = x.shape
    n = B * T
    nkv = N_KV_HEADS * HEAD_DIM
    n_rwkv = DEPTH // 2
    v_first = None
    nk, nv, ns = [], [], []
    nk_all = nv_all = s_all = None
    for i in range(DEPTH):
        j = i // 2
        x2 = x.reshape(n, D)
        if i % 2 == 0:
            if prompt:
                q, k, v = _qkv_proj(x2, P['attn_w_qkv'], P['attn_b_qkv'], j)
                q, k, v = q.reshape(B, T, D), k.reshape(B, T, nkv), v.reshape(B, T, nkv)
                o = _swa_prompt(q, k, v, bias, P['attn_sinks'][j])
                nk.append(k[:, T - WINDOW:].reshape(B, WINDOW, N_KV_HEADS, HEAD_DIM))
                nv.append(v[:, T - WINDOW:].reshape(B, WINDOW, N_KV_HEADS, HEAD_DIM))
            else:
                q, kvt = _qkv_proj_t(x2, P['attn_wq'], P['attn_wkvt'], P['attn_bq'], P['attn_bkvt'], j)
                o, nk_all, nv_all = _swa_sample(q.reshape(B, T, D), kvt, win_k, win_v, j, bias,
                                                P['attn_sinks'][j], nk_all, nv_all)
            x2 = _proj_ln(o.reshape(n, D), P['attn_w_o'], P['attn_b_o'], x2, P['ln_g'], P['ln_b'], j, i)
        else:
            x_prev = jnp.zeros((B, 1, D), x.dtype) if prompt else shift[j].reshape(B, 1, D)
            bb, tt = (1, 256) if prompt else (min(32, B), T)
            r, logd, k, v, kk, a, gate, bonus = _rwkv_proj(x, x_prev, v_first, P, j, bb, tt)
            if j == 0:
                v_first = v
            C, groups = (WKV_ROWS, math.gcd(B, 4)) if prompt else (T, 1)
            o, s_all = _wkv(r, logd, k, v, kk, a, wkv, j, n_rwkv, s_all, C, groups)
            ns.append(x[:, -1])
            x2 = _rwkv_out(o.reshape(n, D), bonus.reshape(n, D), gate.reshape(n, D), x2, P, j, i)
        x2 = _ffn(x2, P['ffn_w_gu'], P['ffn_w_down'], P['ln_g'], P['ln_b'], i)
        x = x2.reshape(B, T, D)
    if prompt:
        nk_all, nv_all = jnp.stack(nk), jnp.stack(nv)
    else:
        nk_all = jnp.transpose(nk_all, (0, 1, 4, 2, 3))
        nv_all = jnp.transpose(nv_all, (0, 1, 4, 2, 3))
    return x, nk_all, nv_all, jnp.stack(ns), s_all


def kernel(x_prompt, x_sample, cache_win_k, cache_win_v, state_shift, state_wkv, rel_bias, ln_g, ln_b, attn_w_qkv, attn_b_qkv, attn_w_o, attn_b_o, attn_sinks, rwkv_mix, rwkv_w_rkv, rwkv_w0, rwkv_w1, rwkv_w2, rwkv_a0, rwkv_a1, rwkv_a2, rwkv_v0, rwkv_v1, rwkv_v2, rwkv_g1, rwkv_g2, rwkv_k_k, rwkv_k_a, rwkv_r_k, rwkv_ln_w, rwkv_ln_b, rwkv_w_o, ffn_w_gu, ffn_w_down):
    bf = lambda w: w.astype(BF)
    lane_head = np.arange(MXU_WIDTH) // RWKV_HEAD
    head_ones = jnp.asarray(lane_head[:, None] == lane_head[None, :], dtype=BF)
    n_rwkv = rwkv_w0.shape[0]
    nq = N_HEADS * HEAD_DIM
    v0_rows = jnp.concatenate([jnp.zeros((1, D_MODEL), F32), rwkv_v0], axis=0)
    rwkv_vecs = jnp.stack([rwkv_w0, rwkv_a0, v0_rows, rwkv_k_k, rwkv_k_a,
                           rwkv_r_k.reshape(n_rwkv, D_MODEL), rwkv_ln_w, rwkv_ln_b], axis=1)
    w_qkv = bf(attn_w_qkv)
    b_qkv = attn_b_qkv[:, None, :]
    P = dict(ln_g=ln_g, ln_b=ln_b,
             attn_w_qkv=w_qkv, attn_b_qkv=b_qkv,
             attn_wq=w_qkv[:, :, :nq], attn_wkvt=jnp.transpose(w_qkv[:, :, nq:], (0, 2, 1)),
             attn_bq=b_qkv[:, :, :nq], attn_bkvt=jnp.transpose(b_qkv[:, :, nq:], (0, 2, 1)),
             attn_w_o=bf(attn_w_o), attn_b_o=attn_b_o[:, None, :], attn_sinks=attn_sinks,
             rwkv_mix=rwkv_mix, rwkv_vecs=rwkv_vecs, rwkv_w_rkv=bf(rwkv_w_rkv),
             rwkv_w1=bf(rwkv_w1), rwkv_w2=bf(rwkv_w2), rwkv_a1=bf(rwkv_a1), rwkv_a2=bf(rwkv_a2),
             rwkv_v1=bf(rwkv_v1), rwkv_v2=bf(rwkv_v2), rwkv_g1=bf(rwkv_g1), rwkv_g2=bf(rwkv_g2),
             rwkv_w_o=bf(rwkv_w_o), ffn_w_gu=bf(ffn_w_gu), ffn_w_down=bf(ffn_w_down),
             head_ones=head_ones)

    T = x_sample.shape[1]
    L = cache_win_k.shape[2]
    qi, kc = np.arange(WINDOW)[:, None], np.arange(WINDOW)[None, :]
    d_prompt = np.where(kc <= qi, qi - kc, qi + WINDOW - kc)
    d_sample = np.arange(T)[:, None] + L - np.arange(L + T)[None, :]
    bias_prompt = _bias_table(rel_bias, d_prompt)
    bias_sample = _bias_table(rel_bias, d_sample)

    y_prompt, pk, pv, ps, pw = _trunk(x_prompt, None, None, None, None, P, bias_prompt)
    win_k = jnp.transpose(cache_win_k, (0, 1, 3, 4, 2))
    win_v = jnp.transpose(cache_win_v, (0, 1, 3, 4, 2))
    y_sample, sk, sv, ss, sw = _trunk(x_sample, win_k, win_v, state_shift, state_wkv,
                                      P, bias_sample)
    return (y_prompt, y_sample, pk, pv, ps, pw, sk, sv, ss, sw)
```

```python
import functools
import math

import numpy as np
import jax
import jax.numpy as jnp
from jax import lax
from jax.experimental import pallas as pl
from jax.experimental.pallas import tpu as pltpu

D_MODEL = 1024
DEPTH = 4
HEAD_DIM = 64
N_HEADS = 16
N_KV_HEADS = 4
GROUP = 4
WINDOW = 128
N_BUCKETS = 32
MAX_DISTANCE = 128
RWKV_HEAD = 64
RWKV_HEADS = 16
GN_EPS = 64e-5
D_FF = 2816
ALPHA = (2 * DEPTH) ** 0.25
LN_EPS = 1e-5
NEG = -1e30
MXU_WIDTH = 256
WKV_ROWS = 64

BF = jnp.bfloat16
F32 = jnp.float32
VMEM_LIMIT = 56 * 1024 * 1024

NT_DIMS = (((1,), (1,)), ((), ()))
TN_DIMS = (((0,), (0,)), ((), ()))

V_W0, V_A0, V_V0, V_KK, V_KA, V_RK, V_LNW, V_LNB = range(8)


def _cparams(*sem):
    return pltpu.CompilerParams(dimension_semantics=sem, vmem_limit_bytes=VMEM_LIMIT)


def _dot(a, b):
    return jnp.dot(a, b, preferred_element_type=F32)


def _layer_norm(z, g, b):
    mu = jnp.mean(z, axis=-1, keepdims=True)
    zc = z - mu
    var = jnp.mean(zc * zc, axis=-1, keepdims=True)
    return zc * lax.rsqrt(var + LN_EPS) * g + b


def _sigmoid(z):
    return 1.0 / (1.0 + jnp.exp(-z))


def _split2(z):
    hi = z.astype(BF)
    lo = (z - hi.astype(F32)).astype(BF)
    return hi, lo


def _split3(z):
    hi = z.astype(BF)
    r1 = z - hi.astype(F32)
    mid = r1.astype(BF)
    lo = (r1 - mid.astype(F32)).astype(BF)
    return hi, mid, lo


def _full(shape):
    n = len(shape)
    return pl.BlockSpec(shape, lambda *_: (0,) * n)


def _layer(shape, *lead):
    k = len(lead)
    rest = len(shape) - k
    return pl.BlockSpec((None,) * k + tuple(shape[k:]), lambda *_: tuple(lead) + (0,) * rest,
                        pipeline_mode=pl.Buffered(1))


def _bias_table_kernel(rb_ref, bucket_ref, valid_ref, o_ref):
    bucket = bucket_ref[...]
    valid = valid_ref[...] > 0
    for h in range(N_HEADS):
        acc = jnp.zeros(bucket.shape, F32)
        for b in range(N_BUCKETS):
            acc = jnp.where(bucket == b, rb_ref[b, h], acc)
        o_ref[h] = jnp.where(valid, acc, NEG)


def _bias_table(rel_bias, d):
    dc = np.maximum(d, 0)
    exact = N_BUCKETS // 2
    df = np.maximum(dc, 1).astype(np.float32)
    large = exact + (np.log(df / np.float32(exact)) / np.float32(math.log(MAX_DISTANCE / exact))
                     * np.float32(N_BUCKETS - exact)).astype(np.int32)
    bucket = np.where(dc < exact, dc, np.minimum(large, N_BUCKETS - 1)).astype(np.int32)
    valid = ((d >= 0) & (d < WINDOW)).astype(np.int32)
    return pl.pallas_call(
        _bias_table_kernel,
        out_shape=jax.ShapeDtypeStruct((N_HEADS,) + d.shape, F32),
        in_specs=[pl.BlockSpec(memory_space=pltpu.SMEM),
                  pl.BlockSpec(memory_space=pltpu.VMEM),
                  pl.BlockSpec(memory_space=pltpu.VMEM)],
        out_specs=pl.BlockSpec(memory_space=pltpu.VMEM),
        name="bias_table",
    )(rel_bias, jnp.asarray(bucket), jnp.asarray(valid))


def _attn_prompt_kernel(sink_ref, x_ref, wqkv_ref, bqkv_ref, bias_ref, wo_ref, bo_ref, g_ref, beta_ref,
                        y_ref, kwin_ref, vwin_ref, kprev_ref, vprev_ref, obuf_ref):
    step = pl.program_id(1)
    W = WINDOW
    R = x_ref.shape[1]
    nsub = R // W
    nq = N_HEADS * HEAD_DIM
    nk = N_KV_HEADS * HEAD_DIM

    @pl.when(step == 0)
    def _():
        kprev_ref[...] = jnp.zeros_like(kprev_ref)
        vprev_ref[...] = jnp.zeros_like(vprev_ref)

    x = x_ref[0]
    qkv = _dot(x.astype(BF), wqkv_ref[...]) + bqkv_ref[...]
    q = (qkv[:, :nq] * HEAD_DIM ** -0.5).astype(BF)
    k = qkv[:, nq:nq + nk]
    v = qkv[:, nq + nk:]
    kwin_ref[0] = k[R - W:]
    vwin_ref[0] = v[R - W:]
    kall = jnp.concatenate([kprev_ref[...], k], axis=0).astype(BF)
    vall = jnp.concatenate([vprev_ref[...], v], axis=0).astype(BF)
    row = lax.broadcasted_iota(jnp.int32, (W, W), 0)
    col = lax.broadcasted_iota(jnp.int32, (W, W), 1)
    own = col <= row
    zero = jnp.zeros((W, W), BF)
    subs = range(nsub)
    scores = {}
    for sub in subs:
        for h in range(N_HEADS):
            g = h // GROUP
            qh = q[sub * W:(sub + 1) * W, h * HEAD_DIM:(h + 1) * HEAD_DIM]
            kh = kall[sub * W:(sub + 2) * W, g * HEAD_DIM:(g + 1) * HEAD_DIM]
            s2 = lax.dot_general(qh, kh, NT_DIMS, preferred_element_type=F32)
            s_prev = s2[:, :W]
            if sub == 0:
                s_prev = jnp.where(step == 0, NEG, s_prev)
            scores[sub, h] = jnp.where(own, s2[:, W:], s_prev) + bias_ref[h]
    m, pb = {}, {}
    for sub in subs:
        for g in range(N_KV_HEADS):
            s = jnp.concatenate([scores[sub, g * GROUP + j] for j in range(GROUP)], axis=0)
            mg = jnp.max(s, axis=-1, keepdims=True)
            m[sub, g] = mg
            pb[sub, g] = jnp.exp(s - mg).astype(BF)
    ones = jnp.ones((2 * W, HEAD_DIM), BF)
    lo = lax.broadcasted_iota(jnp.int32, (W, 2 * HEAD_DIM), 1) < HEAD_DIM
    for sub in subs:
        for g in range(N_KV_HEADS):
            vh = vall[sub * W:(sub + 2) * W, g * HEAD_DIM:(g + 1) * HEAD_DIM]
            vext = jnp.concatenate([vh, ones, ones, vh], axis=1)
            res, esink = [], []
            for j in range(GROUP):
                hs = slice(j * W, (j + 1) * W)
                ph = pb[sub, g][hs]
                p2 = jnp.concatenate([jnp.where(own, zero, ph), jnp.where(own, ph, zero)], axis=1)
                res.append(_dot(p2, vext))
                esink.append(jnp.exp(sink_ref[g * GROUP + j] - m[sub, g][hs]))
            for j in range(0, GROUP, 2):
                o_pair = jnp.where(lo, res[j][:, :2 * HEAD_DIM], res[j + 1][:, 2 * HEAD_DIM:])
                t_pair = jnp.where(lo, res[j][:, 2 * HEAD_DIM:], res[j + 1][:, :2 * HEAD_DIM])
                den = t_pair + jnp.where(lo, esink[j], esink[j + 1])
                c0 = (g * GROUP + j) * HEAD_DIM
                obuf_ref[sub * W:(sub + 1) * W, c0:c0 + 2 * HEAD_DIM] = (o_pair / den).astype(BF)
    kprev_ref[...] = k[R - W:]
    vprev_ref[...] = v[R - W:]
    y = _dot(obuf_ref[...], wo_ref[...]) + bo_ref[...]
    y_ref[0] = _layer_norm(ALPHA * x + y, g_ref[0:1, :], beta_ref[0:1, :])


def _attn_prompt(x, P, j, depth, bias, nsub=4):
    B, T, D = x.shape
    nb = T // WINDOW
    nsub = math.gcd(nb, nsub)
    R = nsub * WINDOW
    nk = N_KV_HEADS * HEAD_DIM
    w, b, w_o, b_o = P['attn_w_qkv'], P['attn_b_qkv'], P['attn_w_o'], P['attn_b_o']
    ln_g, ln_b = P['ln_g'], P['ln_b']
    rows = pl.BlockSpec((1, R, D), lambda bi, i: (bi, i, 0))
    win = pl.BlockSpec((1, WINDOW, nk), lambda bi, i: (bi, 0, 0))
    return pl.pallas_call(
        _attn_prompt_kernel,
        grid=(B, nb // nsub),
        in_specs=[pl.BlockSpec(memory_space=pltpu.SMEM), rows,
                  _layer(w.shape, j), _layer(b.shape, j), _full(bias.shape),
                  _layer(w_o.shape, j), _layer(b_o.shape, j),
                  _layer(ln_g.shape, depth), _layer(ln_b.shape, depth)],
        out_specs=[rows, win, win],
        out_shape=[jax.ShapeDtypeStruct((B, T, D), F32),
                   jax.ShapeDtypeStruct((B, WINDOW, nk), F32),
                   jax.ShapeDtypeStruct((B, WINDOW, nk), F32)],
        scratch_shapes=[pltpu.VMEM((WINDOW, nk), F32), pltpu.VMEM((WINDOW, nk), F32),
                        pltpu.VMEM((R, D), BF)],
        compiler_params=_cparams("parallel", "arbitrary"),
        name="attn_prompt",
    )(P['attn_sinks'][j], x, w, b, bias, w_o, b_o, ln_g, ln_b)


def _qkv_t_kernel(x_ref, wq_ref, wkv_ref, bq_ref, bkv_ref, q_ref, kvt_ref):
    xb = x_ref[...].astype(BF)
    q_ref[...] = (_dot(xb, wq_ref[...]) + bq_ref[...]) * HEAD_DIM ** -0.5
    kvt_ref[...] = (_dot(xb, wkv_ref[...]) + bkv_ref[...]).T


def _qkv_proj_t(x2, w, b, layer, tm=256):
    n = x2.shape[0]
    tm = min(tm, n)
    nq = N_HEADS * HEAD_DIM
    nkv = 2 * N_KV_HEADS * HEAD_DIM
    assert nq % nkv == 0
    once = pl.Buffered(1)
    return pl.pallas_call(
        _qkv_t_kernel,
        grid=(n // tm,),
        in_specs=[pl.BlockSpec((tm, D_MODEL), lambda i: (i, 0)),
                  pl.BlockSpec((None, D_MODEL, nq), lambda i: (layer, 0, 0), pipeline_mode=once),
                  pl.BlockSpec((None, D_MODEL, nkv), lambda i: (layer, 0, nq // nkv), pipeline_mode=once),
                  pl.BlockSpec((None, 1, nq), lambda i: (layer, 0, 0), pipeline_mode=once),
                  pl.BlockSpec((None, 1, nkv), lambda i: (layer, 0, nq // nkv), pipeline_mode=once)],
        out_specs=[pl.BlockSpec((tm, nq), lambda i: (i, 0)),
                   pl.BlockSpec((nkv, tm), lambda i: (0, i))],
        out_shape=[jax.ShapeDtypeStruct((n, nq), F32),
                   jax.ShapeDtypeStruct((nkv, n), F32)],
        compiler_params=_cparams("parallel"),
        name="qkv_proj_t",
    )(x2, w, w, b, b)


def _swa_sample_kernel(T, first, sink_ref, q_ref, kvt_ref, ck_ref, cv_ref, bc_ref, bn_ref, *rest):
    o_ref, nk_ref, nv_ref = rest[-3:]
    bb = q_ref.shape[0]
    L = ck_ref.shape[-1]
    HD = HEAD_DIM
    if first:
        for lyr in range(1, nk_ref.shape[0]):
            nk_ref[lyr] = jnp.zeros(nk_ref.shape[1:], F32)
            nv_ref[lyr] = jnp.zeros(nv_ref.shape[1:], F32)
        nk_out, nv_out = nk_ref.at[0], nv_ref.at[0]
    else:
        nk_out, nv_out = nk_ref, nv_ref
    units = [(b, g) for b in range(bb) for g in range(N_KV_HEADS)]
    lane = lax.broadcasted_iota(jnp.int32, (HD, L), 1)
    knew = [kvt_ref[g * HD:(g + 1) * HD, :] for g in range(N_KV_HEADS)]
    vnew = [kvt_ref[(N_KV_HEADS + g) * HD:(N_KV_HEADS + g + 1) * HD, :] for g in range(N_KV_HEADS)]

    def shift_in(old, new, b):
        moved = pltpu.roll(new, (L - T - b * T) % L, 1)
        return jnp.where(lane >= L - T, moved, pltpu.roll(old, L - T, 1))

    sc, sn = {}, {}
    for b in range(bb):
        q = q_ref[b]
        for g in range(N_KV_HEADS):
            qg = jnp.concatenate([q[:, (g * GROUP + j) * HD:(g * GROUP + j + 1) * HD]
                                  for j in range(GROUP)], axis=0).astype(BF)
            kc = ck_ref[b, g]
            nk_out[b, g] = shift_in(kc, knew[g], b)
            sc[b, g] = _dot(qg, kc.astype(BF)) + bc_ref[g]
            sn[b, g] = _dot(qg, knew[g][:, b * T:(b + 1) * T].astype(BF)) + bn_ref[g]
    sc_all = jnp.concatenate([sc[u] for u in units], axis=0)
    sn_all = jnp.concatenate([sn[u] for u in units], axis=0)
    m = jnp.maximum(jnp.max(sc_all, axis=-1, keepdims=True), jnp.max(sn_all, axis=-1, keepdims=True))
    pc_all = jnp.exp(sc_all - m)
    pn_all = jnp.exp(sn_all - m)
    tot = jnp.sum(pc_all, axis=-1, keepdims=True) + jnp.sum(pn_all, axis=-1, keepdims=True)
    pc_all = pc_all.astype(BF)
    pn_all = pn_all.astype(BF)
    gt = GROUP * T
    for b in range(bb):
        pieces = []
        for g in range(N_KV_HEADS):
            r0 = (b * N_KV_HEADS + g) * gt
            vc = cv_ref[b, g]
            nv_out[b, g] = shift_in(vc, vnew[g], b)
            og = (lax.dot_general(pc_all[r0:r0 + gt], vc.astype(BF), NT_DIMS, preferred_element_type=F32)
                  + lax.dot_general(pn_all[r0:r0 + gt], vnew[g][:, b * T:(b + 1) * T].astype(BF),
                                    NT_DIMS, preferred_element_type=F32))
            for j in range(GROUP):
                rj = slice(r0 + j * T, r0 + (j + 1) * T)
                inv = 1.0 / (tot[rj] + jnp.exp(sink_ref[g * GROUP + j] - m[rj]))
                pieces.append(og[j * T:(j + 1) * T] * inv)
        o_ref[b] = jnp.concatenate(pieces, axis=-1).astype(o_ref.dtype)


def _swa_sample(q, kvt, ck, cv, layer, bias, sinks, nk_prev, nv_prev):
    B, T, _ = q.shape
    nl, L = ck.shape[0], ck.shape[-1]
    first = nk_prev is None
    bb = L // T
    assert B % bb == 0 and L % T == 0
    bias = bias.reshape(N_KV_HEADS, GROUP * T, L + T)
    bc, bn = bias[:, :, :L], bias[:, :, L:]
    qblk = pl.BlockSpec((bb, T, D_MODEL), lambda i: (i, 0, 0))
    cblk = pl.BlockSpec((None, bb, N_KV_HEADS, HEAD_DIM, L), lambda i: (layer, i, 0, 0, 0))
    nblk = (pl.BlockSpec((nl, bb, N_KV_HEADS, HEAD_DIM, L), lambda i: (0, i, 0, 0, 0)) if first else cblk)
    ins = [sinks, q, kvt, ck, cv, bc, bn]
    in_specs = [pl.BlockSpec(memory_space=pltpu.SMEM), qblk,
                pl.BlockSpec((kvt.shape[0], bb * T), lambda i: (0, i)),
                cblk, cblk, _full(bc.shape), _full(bn.shape)]
    aliases = {}
    if not first:
        aliases = {len(ins): 1, len(ins) + 1: 2}
        ins += [nk_prev, nv_prev]
        in_specs += [pl.BlockSpec(memory_space=pl.ANY)] * 2
    return pl.pallas_call(
        functools.partial(_swa_sample_kernel, T, first),
        grid=(B // bb,),
        in_specs=in_specs,
        out_specs=[qblk, nblk, nblk],
        out_shape=[jax.ShapeDtypeStruct((B, T, D_MODEL), BF),
                   jax.ShapeDtypeStruct((nl, B, N_KV_HEADS, HEAD_DIM, L), F32),
                   jax.ShapeDtypeStruct((nl, B, N_KV_HEADS, HEAD_DIM, L), F32)],
        input_output_aliases=aliases,
        compiler_params=_cparams("parallel"),
        name="swa_sample",
    )(*ins)


def _proj_ln_kernel(which, a_ref, w_ref, b_ref, x_ref, g_ref, beta_ref, o_ref):
    y = _dot(a_ref[...].astype(BF), w_ref[...]) + b_ref[...]
    o_ref[...] = _layer_norm(ALPHA * x_ref[...] + y, g_ref[which:which + 1, :], beta_ref[which:which + 1, :])


def _proj_ln(a, w, b, x2, ln_g, ln_b, layer, depth, tm=1024):
    n, kdim = a.shape
    tm = min(tm, n)
    rows = lambda i: (i, 0)
    return pl.pallas_call(
        functools.partial(_proj_ln_kernel, 0),
        grid=(n // tm,),
        in_specs=[pl.BlockSpec((tm, kdim), rows), _layer(w.shape, layer), _layer(b.shape, layer),
                  pl.BlockSpec((tm, D_MODEL), rows), _layer(ln_g.shape, depth), _layer(ln_b.shape, depth)],
        out_specs=pl.BlockSpec((tm, D_MODEL), rows),
        out_shape=jax.ShapeDtypeStruct((n, D_MODEL), F32),
        compiler_params=_cparams("parallel"),
        name="proj_ln",
    )(a, w, b, x2, ln_g, ln_b)


def _ffn_kernel(tf, x_ref, wgu_ref, wd_ref, g_ref, beta_ref, o_ref):
    x = x_ref[...]
    xb = x.astype(BF)
    acc = None
    for c in range(0, D_FF, tf):
        e = min(c + tf, D_FF)
        gt = _dot(xb, wgu_ref[:, c:e])
        ut = _dot(xb, wgu_ref[:, D_FF + c:D_FF + e])
        h = (gt * _sigmoid(gt) * ut).astype(BF)
        part = _dot(h, wd_ref[c:e, :])
        acc = part if acc is None else acc + part
    o_ref[...] = _layer_norm(ALPHA * x + acc, g_ref[1:2, :], beta_ref[1:2, :])


def _ffn(x2, w_gu, w_down, ln_g, ln_b, depth, tm=512, tf=6 * MXU_WIDTH):
    n = x2.shape[0]
    tm = min(tm, n)
    assert D_FF % MXU_WIDTH == 0 and tf % MXU_WIDTH == 0
    rows = lambda i: (i, 0)
    return pl.pallas_call(
        functools.partial(_ffn_kernel, tf),
        grid=(n // tm,),
        in_specs=[pl.BlockSpec((tm, D_MODEL), rows),
                  _layer(w_gu.shape, depth), _layer(w_down.shape, depth),
                  _layer(ln_g.shape, depth), _layer(ln_b.shape, depth)],
        out_specs=pl.BlockSpec((tm, D_MODEL), rows),
        out_shape=jax.ShapeDtypeStruct((n, D_MODEL), F32),
        compiler_params=_cparams("parallel"),
        name="ffn",
    )(x2, w_gu, w_down, ln_g, ln_b)


def _head_sum(z, m_ref):
    hi, lo = _split2(z)
    m = m_ref[...]
    w = m.shape[0]
    return jnp.concatenate([_dot(hi[:, c:c + w], m) + _dot(lo[:, c:c + w], m)
                            for c in range(0, z.shape[1], w)], axis=1)


def _rwkv_proj_kernel(has_vres, *refs):
    if has_vres:
        (x_ref, xp_ref, mix_ref, vec_ref, wr_ref, wk_ref, wv_ref, w1_ref, w2_ref, a1_ref, a2_ref,
         g1_ref, g2_ref, m_ref, v1_ref, v2_ref, vf_ref,
         r_ref, w_ref, k_ref, v_ref, kk_ref, a_ref, g_ref, bonus_ref, carry_ref) = refs
    else:
        (x_ref, xp_ref, mix_ref, vec_ref, wr_ref, wk_ref, wv_ref, w1_ref, w2_ref, a1_ref, a2_ref,
         g1_ref, g2_ref, m_ref,
         r_ref, w_ref, k_ref, v_ref, kk_ref, a_ref, g_ref, bonus_ref, carry_ref) = refs
    bb, tt, D = x_ref.shape
    n = bb * tt
    t = pl.program_id(1)
    x3 = x_ref[...]
    prev = jnp.where(t == 0, xp_ref[...], carry_ref[...])
    carry_ref[...] = x3[:, tt - 1:tt, :]
    x = x3.reshape(n, D)
    prev_rows = jnp.broadcast_to(prev, (bb, tt, D)).reshape(n, D)
    row = lax.broadcasted_iota(jnp.int32, (n, D), 0)
    xs = jnp.where(row % tt == 0, prev_rows, pltpu.roll(x, 1, 0))
    xx = xs - x
    xr, xw, xk, xv, xa, xg = ((x + xx * mix_ref[c:c + 1, :]).astype(BF) for c in range(6))
    w0, a0, v0, k_k, k_a = (vec_ref[c:c + 1, :] for c in (V_W0, V_A0, V_V0, V_KK, V_KA))

    a = _sigmoid(a0 + _dot(_dot(xa, a1_ref[...]).astype(BF), a2_ref[...]))
    k = _dot(xk, wk_ref[...])
    r = _dot(xr, wr_ref[...])
    kk = k * k_k
    kk = kk * jnp.minimum(lax.rsqrt(_head_sum(kk * kk, m_ref)), 1e12)
    k = k * (1.0 + (a - 1.0) * k_a)
    v = _dot(xv, wv_ref[...])
    if has_vres:
        gate = _sigmoid(v0 + _dot(_dot(xv, v1_ref[...]).astype(BF), v2_ref[...]))
        v = v + (vf_ref[...].reshape(n, D) - v) * gate
    rk_sum = _head_sum(r * k * vec_ref[V_RK:V_RK + 1, :], m_ref)
    z = w0 + _dot(jnp.tanh(_dot(xw, w1_ref[...])).astype(BF), w2_ref[...])
    logd = -math.exp(-0.5) * _sigmoid(z)
    g = _dot(_sigmoid(_dot(xg, g1_ref[...])).astype(BF), g2_ref[...])
    bonus = rk_sum * v
    for ref, val in ((r_ref, r), (w_ref, logd), (k_ref, k), (v_ref, v), (kk_ref, kk), (a_ref, a),
                     (g_ref, g), (bonus_ref, bonus)):
        ref[...] = val.reshape(bb, tt, D)


def _rwkv_proj(x, x_prev, v_first, P, j, bb, tt):
    B, T, D = x.shape
    has_vres = j > 0
    blk = pl.BlockSpec((bb, tt, D), lambda b, t: (b, t, 0))
    w_rkv = P['rwkv_w_rkv']
    ins = [x, x_prev, P['rwkv_mix'], P['rwkv_vecs'], w_rkv, w_rkv, w_rkv,
           P['rwkv_w1'], P['rwkv_w2'], P['rwkv_a1'], P['rwkv_a2'], P['rwkv_g1'], P['rwkv_g2'],
           P['head_ones']]
    in_specs = [blk, pl.BlockSpec((bb, 1, D), lambda b, t: (b, 0, 0)),
                _layer(ins[2].shape, j), _layer(ins[3].shape, j),
                _layer(w_rkv.shape, j, 0), _layer(w_rkv.shape, j, 1), _layer(w_rkv.shape, j, 2)]
    in_specs += [_layer(a.shape, j) for a in ins[7:13]] + [_full(P['head_ones'].shape)]
    if has_vres:
        ins += [P['rwkv_v1'], P['rwkv_v2'], v_first]
        in_specs += [_layer(P['rwkv_v1'].shape, j - 1), _layer(P['rwkv_v2'].shape, j - 1), blk]
    return pl.pallas_call(
        functools.partial(_rwkv_proj_kernel, has_vres),
        grid=(B // bb, T // tt),
        in_specs=in_specs,
        out_specs=[blk] * 8,
        out_shape=[jax.ShapeDtypeStruct((B, T, D), F32)] * 8,
        scratch_shapes=[pltpu.VMEM((bb, 1, D), F32)],
        compiler_params=_cparams("parallel", "arbitrary"),
        name="rwkv_proj",
    )(*ins)


def _wkv_kernel(has_s0, first, C, *refs):
    r_ref, w_ref, k_ref, v_ref, kk_ref, a_ref = refs[:6]
    s0_ref = refs[6] if has_s0 else None
    o_ref, so_ref, s_scr = refs[-3:]
    groups = r_ref.shape[0]
    R = WKV_ROWS
    N = RWKV_HEAD
    NP = RWKV_HEADS // 2
    nseq = R // C
    t = pl.program_id(1)
    so_cur = so_ref.at[0] if first else so_ref

    @pl.when(t == 0)
    def _():
        if has_s0:
            for s in range(groups * nseq):
                for p in range(NP):
                    s_scr[s, p] = jnp.concatenate([s0_ref[s, 2 * p], s0_ref[s, 2 * p + 1]], axis=1)
        else:
            s_scr[...] = jnp.zeros_like(s_scr)

    D = r_ref.shape[-1]
    GR = groups * R
    flat = lambda ref: ref[...].reshape(GR, D)
    logd = flat(w_ref)
    ri = lax.broadcasted_iota(jnp.int32, (GR, GR), 0)
    ci = lax.broadcasted_iota(jnp.int32, (GR, GR), 1)
    tril = jnp.logical_and(ri // C == ci // C, ci <= ri).astype(BF)
    cum = sum(_dot(tril, part) for part in _split3(logd))
    g_in = jnp.exp(cum)
    g_ex = jnp.exp(cum - logd)
    g_inv = jnp.exp(-cum)
    kk = flat(kk_ref)
    at = -(kk * g_ex)
    rt = flat(r_ref) * g_in
    bt = kk * flat(a_ref) * g_inv
    kt = flat(k_ref) * g_inv
    v = flat(v_ref)

    def lane_masks(width):
        rr = lax.broadcasted_iota(jnp.int32, (R, width), 0)
        cc = lax.broadcasted_iota(jnp.int32, (R, width), 1) % N
        same = rr // C == cc // C
        return jnp.logical_and(same, cc < rr), jnp.logical_and(same, cc <= rr)

    strict2, _ = lane_masks(2 * N)
    _, incl4 = lane_masks(4 * N)

    def bdiag(m):
        lo = lax.broadcasted_iota(jnp.int32, m.shape, 1) < N
        zero = jnp.zeros_like(m)
        return jnp.concatenate([jnp.where(lo, m, zero), jnp.where(lo, zero, m)], axis=0)

    units = [(gi, p) for gi in range(groups) for p in range(NP)]
    seqs = range(nseq)
    tile = lambda z, u: z[u[0] * R:(u[0] + 1) * R, 2 * N * u[1]:2 * N * (u[1] + 1)]
    rows = [slice(C * s, C * (s + 1)) for s in seqs]
    vp = {u: tile(v, u) for u in units}
    vpb = {u: vp[u].astype(BF) for u in units}
    btp = {u: tile(bt, u) for u in units}
    ktp = {u: tile(kt, u) for u in units}
    xa = {u: tile(at, u) for u in units}
    xr = {u: tile(rt, u) for u in units}
    S0 = {(u, s): s_scr[u[0] * nseq + s, u[1]] for u in units for s in seqs}

    xp = {u: jnp.concatenate([xa[u], xr[u]], axis=0).astype(BF) for u in units}
    bmat = {u: jnp.concatenate([bdiag(btp[u].astype(BF)), bdiag(ktp[u].astype(BF))], axis=0)
            for u in units}
    G = {u: lax.dot_general(xp[u], bmat[u], NT_DIMS, preferred_element_type=F32) for u in units}
    Z = {(u, s): lax.dot_general(
        xp[u] if nseq == 1 else
        jnp.concatenate([xa[u][rows[s]], xr[u][rows[s]]], axis=0).astype(BF),
        bdiag(S0[u, s].astype(BF)), NT_DIMS, preferred_element_type=F32) for u in units for s in seqs}
    za = {u: Z[u, 0][:C] if nseq == 1 else jnp.concatenate([Z[u, s][:C] for s in seqs], axis=0)
          for u in units}
    zr = {u: Z[u, 0][C:] if nseq == 1 else jnp.concatenate([Z[u, s][C:] for s in seqs], axis=0)
          for u in units}
    A = {u: jnp.where(strict2, G[u][:R, :2 * N], 0.0) for u in units}
    aak = {u: jnp.where(strict2, G[u][:R, 2 * N:], 0.0).astype(BF) for u in units}
    T2 = {u: jnp.where(incl4, G[u][R:], 0.0).astype(BF) for u in units}
    W = {u: za[u] + _dot(aak[u], bdiag(vpb[u])) for u in units}
    nsteps = max(1, int(math.log2(C)))
    for step in range(nsteps):
        Ab = {u: A[u].astype(BF) for u in units}
        if step + 1 < nsteps:
            res = {u: _dot(Ab[u], jnp.concatenate([bdiag(W[u].astype(BF)), bdiag(Ab[u])], axis=1))
                   for u in units}
            W = {u: W[u] + res[u][:, :2 * N] for u in units}
            A = {u: res[u][:, 2 * N:] for u in units}
        else:
            W = {u: W[u] + _dot(Ab[u], bdiag(W[u].astype(BF))) for u in units}
    wb = {u: W[u].astype(BF) for u in units}
    for u in units:
        uv = jnp.concatenate([bdiag(wb[u]), bdiag(vpb[u])], axis=0)
        o_ref[u[0], :, 2 * N * u[1]:2 * N * (u[1] + 1)] = zr[u] + _dot(T2[u], uv)
    lo = lax.broadcasted_iota(jnp.int32, (N, 2 * N), 1) < N
    for u in units:
        for s in seqs:
            uvs = jnp.concatenate([W[u][rows[s]], vp[u][rows[s]]], axis=0).astype(BF)
            ys = jnp.concatenate([btp[u][rows[s]], ktp[u][rows[s]]], axis=0).astype(BF)
            full = lax.dot_general(uvs, ys, TN_DIMS, preferred_element_type=F32)
            upd = jnp.where(lo, full[:N], full[N:])
            last = u[0] * R + C * (s + 1) - 1
            g_end = g_in[last:last + 1, 2 * N * u[1]:2 * N * (u[1] + 1)]
            s_scr[u[0] * nseq + s, u[1]] = (S0[u, s] + upd) * g_end

    @pl.when(t == pl.num_programs(1) - 1)
    def _():
        for s in range(groups * nseq):
            for p in range(NP):
                so_cur[s, 2 * p] = s_scr[s, p][:, :N]
                so_cur[s, 2 * p + 1] = s_scr[s, p][:, N:]
        if first:
            for lyr in range(1, so_ref.shape[0]):
                so_ref[lyr] = jnp.zeros(so_ref.shape[1:], F32)


def _wkv(r, logd, k, v, kk, a, s0, layer, nlayers, s_prev, C, groups):
    B, T, D = r.shape
    has_s0 = s0 is not None
    first = s_prev is None
    R = WKV_ROWS
    nseq = R // C
    nt = T // C
    assert (nseq == 1 or nt == 1) and T % C == 0
    nbk = B * T // (R * nt)
    assert nbk % groups == 0
    gs = groups * nseq
    blk = pl.BlockSpec((groups, None, R, D), lambda b, t: (b, t, 0, 0))
    sdims = (RWKV_HEADS, RWKV_HEAD, RWKV_HEAD)
    sblk = pl.BlockSpec((None, gs) + sdims, lambda b, t: (layer, b, 0, 0, 0))
    oblk = pl.BlockSpec((nlayers, gs) + sdims, lambda b, t: (0, b, 0, 0, 0)) if first else sblk
    ins = [x.reshape(nbk, nt, R, D) for x in (r, logd, k, v, kk, a)]
    in_specs = [blk] * 6
    if has_s0:
        ins.append(s0[layer])
        in_specs.append(pl.BlockSpec((gs,) + sdims, lambda b, t: (b, 0, 0, 0)))
    aliases = {}
    if not first:
        aliases = {len(ins): 1}
        ins.append(s_prev)
        in_specs.append(pl.BlockSpec(memory_space=pl.ANY))
    o, s_new = pl.pallas_call(
        functools.partial(_wkv_kernel, has_s0, first, C),
        grid=(nbk // groups, nt),
        in_specs=in_specs,
        out_specs=[blk, oblk],
        out_shape=[jax.ShapeDtypeStruct((nbk, nt, R, D), F32),
                   jax.ShapeDtypeStruct((nlayers, B) + sdims, F32)],
        scratch_shapes=[pltpu.VMEM((gs, RWKV_HEADS // 2, RWKV_HEAD, 2 * RWKV_HEAD), F32)],
        input_output_aliases=aliases,
        compiler_params=_cparams("parallel", "arbitrary"),
        name="wkv",
    )(*ins)
    return o.reshape(B, T, D), s_new


def _rwkv_out_kernel(o_ref, bonus_ref, gate_ref, x_ref, vec_ref, m_ref, wo_ref, g_ref, beta_ref, y_ref):
    ln_w, ln_b = (vec_ref[c:c + 1, :] for c in (V_LNW, V_LNB))
    o = o_ref[...]
    inv_n = 1.0 / RWKV_HEAD
    mu = _head_sum(o, m_ref) * inv_n
    oc = o - mu
    var = _head_sum(oc * oc, m_ref) * inv_n
    on = oc * lax.rsqrt(var + GN_EPS) * ln_w + ln_b
    y = _dot(((on + bonus_ref[...]) * gate_ref[...]).astype(BF), wo_ref[...])
    y_ref[...] = _layer_norm(ALPHA * x_ref[...] + y, g_ref[0:1, :], beta_ref[0:1, :])


def _rwkv_out(o, bonus, gate, x2, P, j, depth, tm=512):
    n = o.shape[0]
    tm = min(tm, n)
    rows = pl.BlockSpec((tm, D_MODEL), lambda i: (i, 0))
    vecs, w_o, ln_g, ln_b = P['rwkv_vecs'], P['rwkv_w_o'], P['ln_g'], P['ln_b']
    return pl.pallas_call(
        _rwkv_out_kernel,
        grid=(n // tm,),
        in_specs=[rows] * 4 + [_layer(vecs.shape, j), _full(P['head_ones'].shape), _layer(w_o.shape, j),
                               _layer(ln_g.shape, depth), _layer(ln_b.shape, depth)],
        out_specs=rows,
        out_shape=jax.ShapeDtypeStruct((n, D_MODEL), F32),
        compiler_params=_cparams("parallel"),
        name="rwkv_out",
    )(o, bonus, gate, x2, vecs, P['head_ones'], w_o, ln_g, ln_b)


def _trunk(x, win_k, win_v, shift, wkv, P, bias):
    prompt = win_k is None
    B, T, D = x.shape
    n = B * T
    nkv = N_KV_HEADS * HEAD_DIM
    n_rwkv = DEPTH // 2
    v_first = None
    nk, nv, ns = [], [], []
    nk_all = nv_all = s_all = None
    for i in range(DEPTH):
        j = i // 2
        x2 = x.reshape(n, D)
        if i % 2 == 0:
            if prompt:
                y, k_win, v_win = _attn_prompt(x, P, j, i, bias)
                x2 = y.reshape(n, D)
                nk.append(k_win.reshape(B, WINDOW, N_KV_HEADS, HEAD_DIM))
                nv.append(v_win.reshape(B, WINDOW, N_KV_HEADS, HEAD_DIM))
            else:
                q, kvt = _qkv_proj_t(x2, P['attn_w_qkv'], P['attn_b_qkv'], j)
                o, nk_all, nv_all = _swa_sample(q.reshape(B, T, D), kvt, win_k, win_v, j, bias,
                                                P['attn_sinks'][j], nk_all, nv_all)
                x2 = _proj_ln(o.reshape(n, D), P['attn_w_o'], P['attn_b_o'], x2, P['ln_g'], P['ln_b'],
                              j, i)
        else:
            x_prev = jnp.zeros((B, 1, D), x.dtype) if prompt else shift[j].reshape(B, 1, D)
            bb, tt = (1, 256) if prompt else (min(32, B), T)
            r, logd, k, v, kk, a, gate, bonus = _rwkv_proj(x, x_prev, v_first, P, j, bb, tt)
            if j == 0:
                v_first = v
            C, groups = (WKV_ROWS, math.gcd(B, 4)) if prompt else (T, 1)
            o, s_all = _wkv(r, logd, k, v, kk, a, wkv, j, n_rwkv, s_all, C, groups)
            ns.append(x[:, -1])
            x2 = _rwkv_out(o.reshape(n, D), bonus.reshape(n, D), gate.reshape(n, D), x2, P, j, i)
        x2 = _ffn(x2, P['ffn_w_gu'], P['ffn_w_down'], P['ln_g'], P['ln_b'], i)
        x = x2.reshape(B, T, D)
    if prompt:
        nk_all, nv_all = jnp.stack(nk), jnp.stack(nv)
    else:
        nk_all = jnp.transpose(nk_all, (0, 1, 4, 2, 3))
        nv_all = jnp.transpose(nv_all, (0, 1, 4, 2, 3))
    return x, nk_all, nv_all, jnp.stack(ns), s_all


def kernel(x_prompt, x_sample, cache_win_k, cache_win_v, state_shift, state_wkv, rel_bias, ln_g, ln_b, attn_w_qkv, attn_b_qkv, attn_w_o, attn_b_o, attn_sinks, rwkv_mix, rwkv_w_rkv, rwkv_w0, rwkv_w1, rwkv_w2, rwkv_a0, rwkv_a1, rwkv_a2, rwkv_v0, rwkv_v1, rwkv_v2, rwkv_g1, rwkv_g2, rwkv_k_k, rwkv_k_a, rwkv_r_k, rwkv_ln_w, rwkv_ln_b, rwkv_w_o, ffn_w_gu, ffn_w_down):
    bf = lambda w: w.astype(BF)
    lane_head = np.arange(MXU_WIDTH) // RWKV_HEAD
    head_ones = jnp.asarray(lane_head[:, None] == lane_head[None, :], dtype=BF)
    n_rwkv = rwkv_w0.shape[0]
    v0_rows = jnp.concatenate([jnp.zeros((1, D_MODEL), F32), rwkv_v0], axis=0)
    rwkv_vecs = jnp.stack([rwkv_w0, rwkv_a0, v0_rows, rwkv_k_k, rwkv_k_a,
                           rwkv_r_k.reshape(n_rwkv, D_MODEL), rwkv_ln_w, rwkv_ln_b], axis=1)
    P = dict(ln_g=ln_g, ln_b=ln_b,
             attn_w_qkv=bf(attn_w_qkv), attn_b_qkv=attn_b_qkv[:, None, :],
             attn_w_o=bf(attn_w_o), attn_b_o=attn_b_o[:, None, :], attn_sinks=attn_sinks,
             rwkv_mix=rwkv_mix, rwkv_vecs=rwkv_vecs, rwkv_w_rkv=bf(rwkv_w_rkv),
             rwkv_w1=bf(rwkv_w1), rwkv_w2=bf(rwkv_w2), rwkv_a1=bf(rwkv_a1), rwkv_a2=bf(rwkv_a2),
             rwkv_v1=bf(rwkv_v1), rwkv_v2=bf(rwkv_v2), rwkv_g1=bf(rwkv_g1), rwkv_g2=bf(rwkv_g2),
             rwkv_w_o=bf(rwkv_w_o), ffn_w_gu=bf(ffn_w_gu), ffn_w_down=bf(ffn_w_down),
             head_ones=head_ones)

    T = x_sample.shape[1]
    L = cache_win_k.shape[2]
    qi, kc = np.arange(WINDOW)[:, None], np.arange(WINDOW)[None, :]
    d_prompt = np.where(kc <= qi, qi - kc, qi + WINDOW - kc)
    d_sample = np.arange(T)[:, None] + L - np.arange(L + T)[None, :]
    bias_prompt = _bias_table(rel_bias, d_prompt)
    bias_sample = _bias_table(rel_bias, d_sample)

    y_prompt, pk, pv, ps, pw = _trunk(x_prompt, None, None, None, None, P, bias_prompt)
    win_k = jnp.transpose(cache_win_k, (0, 1, 3, 4, 2))
    win_v = jnp.transpose(cache_win_v, (0, 1, 3, 4, 2))
    y_sample, sk, sv, ss, sw = _trunk(x_sample, win_k, win_v, state_shift, state_wkv,
                                      P, bias_sample)
    return (y_prompt, y_sample, pk, pv, ps, pw, sk, sv, ss, sw)
```

```python
import functools
import math

import numpy as np
import jax
import jax.numpy as jnp
from jax import lax
from jax.experimental import pallas as pl
from jax.experimental.pallas import tpu as pltpu

D_MODEL = 1024
DEPTH = 4
HEAD_DIM = 64
N_HEADS = 16
N_KV_HEADS = 4
GROUP = 4
WINDOW = 128
N_BUCKETS = 32
MAX_DISTANCE = 128
RWKV_HEAD = 64
RWKV_HEADS = 16
GN_EPS = 64e-5
D_FF = 2816
ALPHA = (2 * DEPTH) ** 0.25
LN_EPS = 1e-5
NEG = -1e30
MXU_WIDTH = 256
WKV_ROWS = 64

BF = jnp.bfloat16
F32 = jnp.float32
VMEM_LIMIT = 56 * 1024 * 1024

NT_DIMS = (((1,), (1,)), ((), ()))
TN_DIMS = (((0,), (0,)), ((), ()))

V_W0, V_A0, V_V0, V_KK, V_KA, V_RK, V_LNW, V_LNB = range(8)


def _cparams(*sem):
    return pltpu.CompilerParams(dimension_semantics=sem, vmem_limit_bytes=VMEM_LIMIT)


def _dot(a, b):
    return jnp.dot(a, b, preferred_element_type=F32)


def _layer_norm(z, g, b):
    mu = jnp.mean(z, axis=-1, keepdims=True)
    zc = z - mu
    var = jnp.mean(zc * zc, axis=-1, keepdims=True)
    return zc * lax.rsqrt(var + LN_EPS) * g + b


def _sigmoid(z):
    return 1.0 / (1.0 + jnp.exp(-z))


def _split2(z):
    hi = z.astype(BF)
    lo = (z - hi.astype(F32)).astype(BF)
    return hi, lo


def _split3(z):
    hi = z.astype(BF)
    r1 = z - hi.astype(F32)
    mid = r1.astype(BF)
    lo = (r1 - mid.astype(F32)).astype(BF)
    return hi, mid, lo


def _full(shape):
    n = len(shape)
    return pl.BlockSpec(shape, lambda *_: (0,) * n)


def _layer(shape, *lead):
    k = len(lead)
    rest = len(shape) - k
    return pl.BlockSpec((None,) * k + tuple(shape[k:]), lambda *_: tuple(lead) + (0,) * rest,
                        pipeline_mode=pl.Buffered(1))


def _bias_table_kernel(rb_ref, bucket_ref, valid_ref, o_ref):
    bucket = bucket_ref[...]
    valid = valid_ref[...] > 0
    for h in range(N_HEADS):
        acc = jnp.zeros(bucket.shape, F32)
        for b in range(N_BUCKETS):
            acc = jnp.where(bucket == b, rb_ref[b, h], acc)
        o_ref[h] = jnp.where(valid, acc, NEG)


def _bias_table(rel_bias, d):
    dc = np.maximum(d, 0)
    exact = N_BUCKETS // 2
    df = np.maximum(dc, 1).astype(np.float32)
    large = exact + (np.log(df / np.float32(exact)) / np.float32(math.log(MAX_DISTANCE / exact))
                     * np.float32(N_BUCKETS - exact)).astype(np.int32)
    bucket = np.where(dc < exact, dc, np.minimum(large, N_BUCKETS - 1)).astype(np.int32)
    valid = ((d >= 0) & (d < WINDOW)).astype(np.int32)
    return pl.pallas_call(
        _bias_table_kernel,
        out_shape=jax.ShapeDtypeStruct((N_HEADS,) + d.shape, F32),
        in_specs=[pl.BlockSpec(memory_space=pltpu.SMEM),
                  pl.BlockSpec(memory_space=pltpu.VMEM),
                  pl.BlockSpec(memory_space=pltpu.VMEM)],
        out_specs=pl.BlockSpec(memory_space=pltpu.VMEM),
        name="bias_table",
    )(rel_bias, jnp.asarray(bucket), jnp.asarray(valid))


def _attn_prompt_kernel(sink_ref, x_ref, wqkv_ref, bqkv_ref, bias_ref, wo_ref, bo_ref, g_ref, beta_ref,
                        y_ref, kwin_ref, vwin_ref, kprev_ref, vprev_ref, obuf_ref):
    step = pl.program_id(1)
    W = WINDOW
    R = x_ref.shape[1]
    nsub = R // W
    nq = N_HEADS * HEAD_DIM
    nk = N_KV_HEADS * HEAD_DIM

    @pl.when(step == 0)
    def _():
        kprev_ref[...] = jnp.zeros_like(kprev_ref)
        vprev_ref[...] = jnp.zeros_like(vprev_ref)

    x = x_ref[0]
    qkv = _dot(x.astype(BF), wqkv_ref[...]) + bqkv_ref[...]
    q = (qkv[:, :nq] * HEAD_DIM ** -0.5).astype(BF)
    k = qkv[:, nq:nq + nk]
    v = qkv[:, nq + nk:]
    kwin_ref[0] = k[R - W:]
    vwin_ref[0] = v[R - W:]
    kall = jnp.concatenate([kprev_ref[...], k], axis=0).astype(BF)
    vall = jnp.concatenate([vprev_ref[...], v], axis=0).astype(BF)
    row = lax.broadcasted_iota(jnp.int32, (W, W), 0)
    col = lax.broadcasted_iota(jnp.int32, (W, W), 1)
    own = col <= row
    zero = jnp.zeros((W, W), BF)
    subs = range(nsub)
    scores = {}
    for sub in subs:
        for h in range(N_HEADS):
            g = h // GROUP
            qh = q[sub * W:(sub + 1) * W, h * HEAD_DIM:(h + 1) * HEAD_DIM]
            kh = kall[sub * W:(sub + 2) * W, g * HEAD_DIM:(g + 1) * HEAD_DIM]
            s2 = lax.dot_general(qh, kh, NT_DIMS, preferred_element_type=F32)
            s_prev = s2[:, :W]
            if sub == 0:
                s_prev = jnp.where(step == 0, NEG, s_prev)
            scores[sub, h] = jnp.where(own, s2[:, W:], s_prev) + bias_ref[h]
    m, pb = {}, {}
    for sub in subs:
        for g in range(N_KV_HEADS):
            s = jnp.concatenate([scores[sub, g * GROUP + j] for j in range(GROUP)], axis=0)
            mg = jnp.max(s, axis=-1, keepdims=True)
            m[sub, g] = mg
            pb[sub, g] = jnp.exp(s - mg).astype(BF)
    ones = jnp.ones((2 * W, HEAD_DIM), BF)
    lo = lax.broadcasted_iota(jnp.int32, (W, 2 * HEAD_DIM), 1) < HEAD_DIM
    for sub in subs:
        for g in range(N_KV_HEADS):
            vh = vall[sub * W:(sub + 2) * W, g * HEAD_DIM:(g + 1) * HEAD_DIM]
            vext = jnp.concatenate([vh, ones, ones, vh], axis=1)
            res, esink = [], []
            for j in range(GROUP):
                hs = slice(j * W, (j + 1) * W)
                ph = pb[sub, g][hs]
                p2 = jnp.concatenate([jnp.where(own, zero, ph), jnp.where(own, ph, zero)], axis=1)
                res.append(_dot(p2, vext))
                esink.append(jnp.exp(sink_ref[g * GROUP + j] - m[sub, g][hs]))
            for j in range(0, GROUP, 2):
                o_pair = jnp.where(lo, res[j][:, :2 * HEAD_DIM], res[j + 1][:, 2 * HEAD_DIM:])
                t_pair = jnp.where(lo, res[j][:, 2 * HEAD_DIM:], res[j + 1][:, :2 * HEAD_DIM])
                den = t_pair + jnp.where(lo, esink[j], esink[j + 1])
                c0 = (g * GROUP + j) * HEAD_DIM
                obuf_ref[sub * W:(sub + 1) * W, c0:c0 + 2 * HEAD_DIM] = (o_pair / den).astype(BF)
    kprev_ref[...] = k[R - W:]
    vprev_ref[...] = v[R - W:]
    y = _dot(obuf_ref[...], wo_ref[...]) + bo_ref[...]
    y_ref[0] = _layer_norm(ALPHA * x + y, g_ref[0:1, :], beta_ref[0:1, :])


def _attn_prompt(x, P, j, depth, bias, nsub=4):
    B, T, D = x.shape
    nb = T // WINDOW
    nsub = math.gcd(nb, nsub)
    R = nsub * WINDOW
    nk = N_KV_HEADS * HEAD_DIM
    w, b, w_o, b_o = P['attn_w_qkv'], P['attn_b_qkv'], P['attn_w_o'], P['attn_b_o']
    ln_g, ln_b = P['ln_g'], P['ln_b']
    rows = pl.BlockSpec((1, R, D), lambda bi, i: (bi, i, 0))
    win = pl.BlockSpec((1, WINDOW, nk), lambda bi, i: (bi, 0, 0))
    return pl.pallas_call(
        _attn_prompt_kernel,
        grid=(B, nb // nsub),
        in_specs=[pl.BlockSpec(memory_space=pltpu.SMEM), rows,
                  _layer(w.shape, j), _layer(b.shape, j), _full(bias.shape),
                  _layer(w_o.shape, j), _layer(b_o.shape, j),
                  _layer(ln_g.shape, depth), _layer(ln_b.shape, depth)],
        out_specs=[rows, win, win],
        out_shape=[jax.ShapeDtypeStruct((B, T, D), F32),
                   jax.ShapeDtypeStruct((B, WINDOW, nk), F32),
                   jax.ShapeDtypeStruct((B, WINDOW, nk), F32)],
        scratch_shapes=[pltpu.VMEM((WINDOW, nk), F32), pltpu.VMEM((WINDOW, nk), F32),
                        pltpu.VMEM((R, D), BF)],
        compiler_params=_cparams("parallel", "arbitrary"),
        name="attn_prompt",
    )(P['attn_sinks'][j], x, w, b, bias, w_o, b_o, ln_g, ln_b)


def _qkv_t_kernel(x_ref, wq_ref, wkv_ref, bq_ref, bkv_ref, q_ref, kvt_ref):
    xb = x_ref[...].astype(BF)
    q_ref[...] = (_dot(xb, wq_ref[...]) + bq_ref[...]) * HEAD_DIM ** -0.5
    kvt_ref[...] = (_dot(xb, wkv_ref[...]) + bkv_ref[...]).T


def _qkv_proj_t(x2, w, b, layer, tm=256):
    n = x2.shape[0]
    tm = min(tm, n)
    nq = N_HEADS * HEAD_DIM
    nkv = 2 * N_KV_HEADS * HEAD_DIM
    assert nq % nkv == 0
    once = pl.Buffered(1)
    return pl.pallas_call(
        _qkv_t_kernel,
        grid=(n // tm,),
        in_specs=[pl.BlockSpec((tm, D_MODEL), lambda i: (i, 0)),
                  pl.BlockSpec((None, D_MODEL, nq), lambda i: (layer, 0, 0), pipeline_mode=once),
                  pl.BlockSpec((None, D_MODEL, nkv), lambda i: (layer, 0, nq // nkv), pipeline_mode=once),
                  pl.BlockSpec((None, 1, nq), lambda i: (layer, 0, 0), pipeline_mode=once),
                  pl.BlockSpec((None, 1, nkv), lambda i: (layer, 0, nq // nkv), pipeline_mode=once)],
        out_specs=[pl.BlockSpec((tm, nq), lambda i: (i, 0)),
                   pl.BlockSpec((nkv, tm), lambda i: (0, i))],
        out_shape=[jax.ShapeDtypeStruct((n, nq), F32),
                   jax.ShapeDtypeStruct((nkv, n), F32)],
        compiler_params=_cparams("parallel"),
        name="qkv_proj_t",
    )(x2, w, w, b, b)


def _swa_sample_kernel(T, first, sink_ref, q_ref, kvt_ref, ck_ref, cv_ref, bc_ref, bn_ref, *rest):
    o_ref, nk_ref, nv_ref = rest[-3:]
    bb = q_ref.shape[0]
    L = ck_ref.shape[-1]
    HD = HEAD_DIM
    if first:
        for lyr in range(1, nk_ref.shape[0]):
            nk_ref[lyr] = jnp.zeros(nk_ref.shape[1:], F32)
            nv_ref[lyr] = jnp.zeros(nv_ref.shape[1:], F32)
        nk_out, nv_out = nk_ref.at[0], nv_ref.at[0]
    else:
        nk_out, nv_out = nk_ref, nv_ref
    units = [(b, g) for b in range(bb) for g in range(N_KV_HEADS)]
    lane = lax.broadcasted_iota(jnp.int32, (HD, L), 1)
    knew = [kvt_ref[g * HD:(g + 1) * HD, :] for g in range(N_KV_HEADS)]
    vnew = [kvt_ref[(N_KV_HEADS + g) * HD:(N_KV_HEADS + g + 1) * HD, :] for g in range(N_KV_HEADS)]

    def shift_in(old, new, b):
        moved = pltpu.roll(new, (L - T - b * T) % L, 1)
        return jnp.where(lane >= L - T, moved, pltpu.roll(old, L - T, 1))

    sc, sn = {}, {}
    for b in range(bb):
        q = q_ref[b]
        for g in range(N_KV_HEADS):
            qg = jnp.concatenate([q[:, (g * GROUP + j) * HD:(g * GROUP + j + 1) * HD]
                                  for j in range(GROUP)], axis=0).astype(BF)
            kc = ck_ref[b, g]
            nk_out[b, g] = shift_in(kc, knew[g], b)
            sc[b, g] = _dot(qg, kc.astype(BF)) + bc_ref[g]
            sn[b, g] = _dot(qg, knew[g][:, b * T:(b + 1) * T].astype(BF)) + bn_ref[g]
    sc_all = jnp.concatenate([sc[u] for u in units], axis=0)
    sn_all = jnp.concatenate([sn[u] for u in units], axis=0)
    m = jnp.maximum(jnp.max(sc_all, axis=-1, keepdims=True), jnp.max(sn_all, axis=-1, keepdims=True))
    pc_all = jnp.exp(sc_all - m)
    pn_all = jnp.exp(sn_all - m)
    tot = jnp.sum(pc_all, axis=-1, keepdims=True) + jnp.sum(pn_all, axis=-1, keepdims=True)
    pc_all = pc_all.astype(BF)
    pn_all = pn_all.astype(BF)
    gt = GROUP * T
    for b in range(bb):
        pieces = []
        for g in range(N_KV_HEADS):
            r0 = (b * N_KV_HEADS + g) * gt
            vc = cv_ref[b, g]
            nv_out[b, g] = shift_in(vc, vnew[g], b)
            og = (lax.dot_general(pc_all[r0:r0 + gt], vc.astype(BF), NT_DIMS, preferred_element_type=F32)
                  + lax.dot_general(pn_all[r0:r0 + gt], vnew[g][:, b * T:(b + 1) * T].astype(BF),
                                    NT_DIMS, preferred_element_type=F32))
            for j in range(GROUP):
                rj = slice(r0 + j * T, r0 + (j + 1) * T)
                inv = 1.0 / (tot[rj] + jnp.exp(sink_ref[g * GROUP + j] - m[rj]))
                pieces.append(og[j * T:(j + 1) * T] * inv)
        o_ref[b] = jnp.concatenate(pieces, axis=-1).astype(o_ref.dtype)


def _swa_sample(q, kvt, ck, cv, layer, bias, sinks, nk_prev, nv_prev):
    B, T, _ = q.shape
    nl, L = ck.shape[0], ck.shape[-1]
    first = nk_prev is None
    bb = L // T
    assert B % bb == 0 and L % T == 0
    bias = bias.reshape(N_KV_HEADS, GROUP * T, L + T)
    bc, bn = bias[:, :, :L], bias[:, :, L:]
    qblk = pl.BlockSpec((bb, T, D_MODEL), lambda i: (i, 0, 0))
    cblk = pl.BlockSpec((None, bb, N_KV_HEADS, HEAD_DIM, L), lambda i: (layer, i, 0, 0, 0))
    nblk = (pl.BlockSpec((nl, bb, N_KV_HEADS, HEAD_DIM, L), lambda i: (0, i, 0, 0, 0)) if first else cblk)
    ins = [sinks, q, kvt, ck, cv, bc, bn]
    in_specs = [pl.BlockSpec(memory_space=pltpu.SMEM), qblk,
                pl.BlockSpec((kvt.shape[0], bb * T), lambda i: (0, i)),
                cblk, cblk, _full(bc.shape), _full(bn.shape)]
    aliases = {}
    if not first:
        aliases = {len(ins): 1, len(ins) + 1: 2}
        ins += [nk_prev, nv_prev]
        in_specs += [pl.BlockSpec(memory_space=pl.ANY)] * 2
    return pl.pallas_call(
        functools.partial(_swa_sample_kernel, T, first),
        grid=(B // bb,),
        in_specs=in_specs,
        out_specs=[qblk, nblk, nblk],
        out_shape=[jax.ShapeDtypeStruct((B, T, D_MODEL), BF),
                   jax.ShapeDtypeStruct((nl, B, N_KV_HEADS, HEAD_DIM, L), F32),
                   jax.ShapeDtypeStruct((nl, B, N_KV_HEADS, HEAD_DIM, L), F32)],
        input_output_aliases=aliases,
        compiler_params=_cparams("parallel"),
        name="swa_sample",
    )(*ins)


def _proj_ln_kernel(which, a_ref, w_ref, b_ref, x_ref, g_ref, beta_ref, o_ref):
    y = _dot(a_ref[...].astype(BF), w_ref[...]) + b_ref[...]
    o_ref[...] = _layer_norm(ALPHA * x_ref[...] + y, g_ref[which:which + 1, :], beta_ref[which:which + 1, :])


def _proj_ln(a, w, b, x2, ln_g, ln_b, layer, depth, tm=1024):
    n, kdim = a.shape
    tm = min(tm, max(n // 2, 8))
    rows = lambda i: (i, 0)
    return pl.pallas_call(
        functools.partial(_proj_ln_kernel, 0),
        grid=(n // tm,),
        in_specs=[pl.BlockSpec((tm, kdim), rows), _layer(w.shape, layer), _layer(b.shape, layer),
                  pl.BlockSpec((tm, D_MODEL), rows), _layer(ln_g.shape, depth), _layer(ln_b.shape, depth)],
        out_specs=pl.BlockSpec((tm, D_MODEL), rows),
        out_shape=jax.ShapeDtypeStruct((n, D_MODEL), F32),
        compiler_params=_cparams("parallel"),
        name="proj_ln",
    )(a, w, b, x2, ln_g, ln_b)


def _ffn_kernel(tf, x_ref, wgu_ref, wd_ref, g_ref, beta_ref, o_ref):
    x = x_ref[...]
    xb = x.astype(BF)
    acc = None
    for c in range(0, D_FF, tf):
        e = min(c + tf, D_FF)
        gt = _dot(xb, wgu_ref[:, c:e])
        ut = _dot(xb, wgu_ref[:, D_FF + c:D_FF + e])
        h = (gt * _sigmoid(gt) * ut).astype(BF)
        part = _dot(h, wd_ref[c:e, :])
        acc = part if acc is None else acc + part
    o_ref[...] = _layer_norm(ALPHA * x + acc, g_ref[1:2, :], beta_ref[1:2, :])


def _ffn(x2, w_gu, w_down, ln_g, ln_b, depth, tm=512, tf=6 * MXU_WIDTH):
    n = x2.shape[0]
    tm = min(tm, n)
    assert D_FF % MXU_WIDTH == 0 and tf % MXU_WIDTH == 0
    rows = lambda i: (i, 0)
    return pl.pallas_call(
        functools.partial(_ffn_kernel, tf),
        grid=(n // tm,),
        in_specs=[pl.BlockSpec((tm, D_MODEL), rows),
                  _layer(w_gu.shape, depth), _layer(w_down.shape, depth),
                  _layer(ln_g.shape, depth), _layer(ln_b.shape, depth)],
        out_specs=pl.BlockSpec((tm, D_MODEL), rows),
        out_shape=jax.ShapeDtypeStruct((n, D_MODEL), F32),
        compiler_params=_cparams("parallel"),
        name="ffn",
    )(x2, w_gu, w_down, ln_g, ln_b)


def _head_sum(z, m_ref):
    hi, lo = _split2(z)
    m = m_ref[...]
    w = m.shape[0]
    return jnp.concatenate([_dot(hi[:, c:c + w], m) + _dot(lo[:, c:c + w], m)
                            for c in range(0, z.shape[1], w)], axis=1)


def _rwkv_proj_kernel(has_vres, *refs):
    if has_vres:
        (x_ref, xp_ref, mix_ref, vec_ref, wr_ref, wk_ref, wv_ref, w1_ref, w2_ref, a1_ref, a2_ref,
         g1_ref, g2_ref, m_ref, v1_ref, v2_ref, vf_ref,
         r_ref, w_ref, k_ref, v_ref, kk_ref, a_ref, g_ref, bonus_ref, carry_ref) = refs
    else:
        (x_ref, xp_ref, mix_ref, vec_ref, wr_ref, wk_ref, wv_ref, w1_ref, w2_ref, a1_ref, a2_ref,
         g1_ref, g2_ref, m_ref,
         r_ref, w_ref, k_ref, v_ref, kk_ref, a_ref, g_ref, bonus_ref, carry_ref) = refs
    bb, tt, D = x_ref.shape
    n = bb * tt
    t = pl.program_id(1)
    x3 = x_ref[...]
    prev = jnp.where(t == 0, xp_ref[...], carry_ref[...])
    carry_ref[...] = x3[:, tt - 1:tt, :]
    x = x3.reshape(n, D)
    prev_rows = jnp.broadcast_to(prev, (bb, tt, D)).reshape(n, D)
    row = lax.broadcasted_iota(jnp.int32, (n, D), 0)
    xs = jnp.where(row % tt == 0, prev_rows, pltpu.roll(x, 1, 0))
    xx = xs - x
    xr, xw, xk, xv, xa, xg = ((x + xx * mix_ref[c:c + 1, :]).astype(BF) for c in range(6))
    w0, a0, v0, k_k, k_a = (vec_ref[c:c + 1, :] for c in (V_W0, V_A0, V_V0, V_KK, V_KA))

    a = _sigmoid(a0 + _dot(_dot(xa, a1_ref[...]).astype(BF), a2_ref[...]))
    k = _dot(xk, wk_ref[...])
    r = _dot(xr, wr_ref[...])
    kk = k * k_k
    kk = kk * jnp.minimum(lax.rsqrt(_head_sum(kk * kk, m_ref)), 1e12)
    k = k * (1.0 + (a - 1.0) * k_a)
    v = _dot(xv, wv_ref[...])
    if has_vres:
        gate = _sigmoid(v0 + _dot(_dot(xv, v1_ref[...]).astype(BF), v2_ref[...]))
        v = v + (vf_ref[...].reshape(n, D) - v) * gate
    rk_sum = _head_sum(r * k * vec_ref[V_RK:V_RK + 1, :], m_ref)
    z = w0 + _dot(jnp.tanh(_dot(xw, w1_ref[...])).astype(BF), w2_ref[...])
    logd = -math.exp(-0.5) * _sigmoid(z)
    g = _dot(_sigmoid(_dot(xg, g1_ref[...])).astype(BF), g2_ref[...])
    bonus = rk_sum * v
    for ref, val in ((r_ref, r), (w_ref, logd), (k_ref, k), (v_ref, v), (kk_ref, kk), (a_ref, a),
                     (g_ref, g), (bonus_ref, bonus)):
        ref[...] = val.reshape(bb, tt, D)


def _rwkv_proj(x, x_prev, v_first, P, j, bb, tt):
    B, T, D = x.shape
    has_vres = j > 0
    blk = pl.BlockSpec((bb, tt, D), lambda b, t: (b, t, 0))
    w_rkv = P['rwkv_w_rkv']
    ins = [x, x_prev, P['rwkv_mix'], P['rwkv_vecs'], w_rkv, w_rkv, w_rkv,
           P['rwkv_w1'], P['rwkv_w2'], P['rwkv_a1'], P['rwkv_a2'], P['rwkv_g1'], P['rwkv_g2'],
           P['head_ones']]
    in_specs = [blk, pl.BlockSpec((bb, 1, D), lambda b, t: (b, 0, 0)),
                _layer(ins[2].shape, j), _layer(ins[3].shape, j),
                _layer(w_rkv.shape, j, 0), _layer(w_rkv.shape, j, 1), _layer(w_rkv.shape, j, 2)]
    in_specs += [_layer(a.shape, j) for a in ins[7:13]] + [_full(P['head_ones'].shape)]
    if has_vres:
        ins += [P['rwkv_v1'], P['rwkv_v2'], v_first]
        in_specs += [_layer(P['rwkv_v1'].shape, j - 1), _layer(P['rwkv_v2'].shape, j - 1), blk]
    return pl.pallas_call(
        functools.partial(_rwkv_proj_kernel, has_vres),
        grid=(B // bb, T // tt),
        in_specs=in_specs,
        out_specs=[blk] * 8,
        out_shape=[jax.ShapeDtypeStruct((B, T, D), F32)] * 8,
        scratch_shapes=[pltpu.VMEM((bb, 1, D), F32)],
        compiler_params=_cparams("parallel", "arbitrary"),
        name="rwkv_proj",
    )(*ins)


def _wkv_kernel(has_s0, first, C, *refs):
    r_ref, w_ref, k_ref, v_ref, kk_ref, a_ref = refs[:6]
    s0_ref = refs[6] if has_s0 else None
    o_ref, so_ref, s_scr = refs[-3:]
    groups = r_ref.shape[0]
    R = WKV_ROWS
    N = RWKV_HEAD
    NP = RWKV_HEADS // 2
    nseq = R // C
    t = pl.program_id(1)
    so_cur = so_ref.at[0] if first else so_ref

    @pl.when(t == 0)
    def _():
        if has_s0:
            for s in range(groups * nseq):
                for p in range(NP):
                    s_scr[s, p] = jnp.concatenate([s0_ref[s, 2 * p], s0_ref[s, 2 * p + 1]], axis=1)
        else:
            s_scr[...] = jnp.zeros_like(s_scr)

    D = r_ref.shape[-1]
    GR = groups * R
    flat = lambda ref: ref[...].reshape(GR, D)
    logd = flat(w_ref)
    ri = lax.broadcasted_iota(jnp.int32, (GR, GR), 0)
    ci = lax.broadcasted_iota(jnp.int32, (GR, GR), 1)
    tril = jnp.logical_and(ri // C == ci // C, ci <= ri).astype(BF)
    cum = sum(_dot(tril, part) for part in _split3(logd))
    g_in = jnp.exp(cum)
    g_ex = jnp.exp(cum - logd)
    g_inv = jnp.exp(-cum)
    kk = flat(kk_ref)
    at = -(kk * g_ex)
    rt = flat(r_ref) * g_in
    bt = kk * flat(a_ref) * g_inv
    kt = flat(k_ref) * g_inv
    v = flat(v_ref)

    def lane_masks(width):
        rr = lax.broadcasted_iota(jnp.int32, (R, width), 0)
        cc = lax.broadcasted_iota(jnp.int32, (R, width), 1) % N
        same = rr // C == cc // C
        return jnp.logical_and(same, cc < rr), jnp.logical_and(same, cc <= rr)

    strict2, _ = lane_masks(2 * N)
    _, incl4 = lane_masks(4 * N)

    def bdiag(m):
        lo = lax.broadcasted_iota(jnp.int32, m.shape, 1) < N
        zero = jnp.zeros_like(m)
        return jnp.concatenate([jnp.where(lo, m, zero), jnp.where(lo, zero, m)], axis=0)

    units = [(gi, p) for gi in range(groups) for p in range(NP)]
    seqs = range(nseq)
    tile = lambda z, u: z[u[0] * R:(u[0] + 1) * R, 2 * N * u[1]:2 * N * (u[1] + 1)]
    rows = [slice(C * s, C * (s + 1)) for s in seqs]
    vp = {u: tile(v, u) for u in units}
    vpb = {u: vp[u].astype(BF) for u in units}
    btp = {u: tile(bt, u) for u in units}
    ktp = {u: tile(kt, u) for u in units}
    xa = {u: tile(at, u) for u in units}
    xr = {u: tile(rt, u) for u in units}
    S0 = {(u, s): s_scr[u[0] * nseq + s, u[1]] for u in units for s in seqs}

    xp = {u: jnp.concatenate([xa[u], xr[u]], axis=0).astype(BF) for u in units}
    bmat = {u: jnp.concatenate([bdiag(btp[u].astype(BF)), bdiag(ktp[u].astype(BF))], axis=0)
            for u in units}
    G = {u: lax.dot_general(xp[u], bmat[u], NT_DIMS, preferred_element_type=F32) for u in units}
    Z = {(u, s): lax.dot_general(
        xp[u] if nseq == 1 else
        jnp.concatenate([xa[u][rows[s]], xr[u][rows[s]]], axis=0).astype(BF),
        bdiag(S0[u, s].astype(BF)), NT_DIMS, preferred_element_type=F32) for u in units for s in seqs}
    za = {u: Z[u, 0][:C] if nseq == 1 else jnp.concatenate([Z[u, s][:C] for s in seqs], axis=0)
          for u in units}
    zr = {u: Z[u, 0][C:] if nseq == 1 else jnp.concatenate([Z[u, s][C:] for s in seqs], axis=0)
          for u in units}
    A = {u: jnp.where(strict2, G[u][:R, :2 * N], 0.0) for u in units}
    aak = {u: jnp.where(strict2, G[u][:R, 2 * N:], 0.0).astype(BF) for u in units}
    T2 = {u: jnp.where(incl4, G[u][R:], 0.0).astype(BF) for u in units}
    W = {u: za[u] + _dot(aak[u], bdiag(vpb[u])) for u in units}
    nsteps = max(1, int(math.log2(C)))
    for step in range(nsteps):
        Ab = {u: A[u].astype(BF) for u in units}
        if step + 1 < nsteps:
            res = {u: _dot(Ab[u], jnp.concatenate([bdiag(W[u].astype(BF)), bdiag(Ab[u])], axis=1))
                   for u in units}
            W = {u: W[u] + res[u][:, :2 * N] for u in units}
            A = {u: res[u][:, 2 * N:] for u in units}
        else:
            W = {u: W[u] + _dot(Ab[u], bdiag(W[u].astype(BF))) for u in units}
    wb = {u: W[u].astype(BF) for u in units}
    for u in units:
        uv = jnp.concatenate([bdiag(wb[u]), bdiag(vpb[u])], axis=0)
        o_ref[u[0], :, 2 * N * u[1]:2 * N * (u[1] + 1)] = zr[u] + _dot(T2[u], uv)
    lo = lax.broadcasted_iota(jnp.int32, (N, 2 * N), 1) < N
    for u in units:
        for s in seqs:
            uvs = jnp.concatenate([W[u][rows[s]], vp[u][rows[s]]], axis=0).astype(BF)
            ys = jnp.concatenate([btp[u][rows[s]], ktp[u][rows[s]]], axis=0).astype(BF)
            full = lax.dot_general(uvs, ys, TN_DIMS, preferred_element_type=F32)
            upd = jnp.where(lo, full[:N], full[N:])
            last = u[0] * R + C * (s + 1) - 1
            g_end = g_in[last:last + 1, 2 * N * u[1]:2 * N * (u[1] + 1)]
            s_scr[u[0] * nseq + s, u[1]] = (S0[u, s] + upd) * g_end

    @pl.when(t == pl.num_programs(1) - 1)
    def _():
        for s in range(groups * nseq):
            for p in range(NP):
                so_cur[s, 2 * p] = s_scr[s, p][:, :N]
                so_cur[s, 2 * p + 1] = s_scr[s, p][:, N:]
        if first:
            for lyr in range(1, so_ref.shape[0]):
                so_ref[lyr] = jnp.zeros(so_ref.shape[1:], F32)


def _wkv(r, logd, k, v, kk, a, s0, layer, nlayers, s_prev, C, groups):
    B, T, D = r.shape
    has_s0 = s0 is not None
    first = s_prev is None
    R = WKV_ROWS
    nseq = R // C
    nt = T // C
    assert (nseq == 1 or nt == 1) and T % C == 0
    nbk = B * T // (R * nt)
    assert nbk % groups == 0
    gs = groups * nseq
    blk = pl.BlockSpec((groups, None, R, D), lambda b, t: (b, t, 0, 0))
    sdims = (RWKV_HEADS, RWKV_HEAD, RWKV_HEAD)
    sblk = pl.BlockSpec((None, gs) + sdims, lambda b, t: (layer, b, 0, 0, 0))
    oblk = pl.BlockSpec((nlayers, gs) + sdims, lambda b, t: (0, b, 0, 0, 0)) if first else sblk
    ins = [x.reshape(nbk, nt, R, D) for x in (r, logd, k, v, kk, a)]
    in_specs = [blk] * 6
    if has_s0:
        ins.append(s0[layer])
        in_specs.append(pl.BlockSpec((gs,) + sdims, lambda b, t: (b, 0, 0, 0)))
    aliases = {}
    if not first:
        aliases = {len(ins): 1}
        ins.append(s_prev)
        in_specs.append(pl.BlockSpec(memory_space=pl.ANY))
    o, s_new = pl.pallas_call(
        functools.partial(_wkv_kernel, has_s0, first, C),
        grid=(nbk // groups, nt),
        in_specs=in_specs,
        out_specs=[blk, oblk],
        out_shape=[jax.ShapeDtypeStruct((nbk, nt, R, D), F32),
                   jax.ShapeDtypeStruct((nlayers, B) + sdims, F32)],
        scratch_shapes=[pltpu.VMEM((gs, RWKV_HEADS // 2, RWKV_HEAD, 2 * RWKV_HEAD), F32)],
        input_output_aliases=aliases,
        compiler_params=_cparams("parallel", "arbitrary"),
        name="wkv",
    )(*ins)
    return o.reshape(B, T, D), s_new


def _rwkv_out_kernel(o_ref, bonus_ref, gate_ref, x_ref, vec_ref, m_ref, wo_ref, g_ref, beta_ref, y_ref):
    ln_w, ln_b = (vec_ref[c:c + 1, :] for c in (V_LNW, V_LNB))
    o = o_ref[...]
    inv_n = 1.0 / RWKV_HEAD
    mu = _head_sum(o, m_ref) * inv_n
    oc = o - mu
    var = _head_sum(oc * oc, m_ref) * inv_n
    on = oc * lax.rsqrt(var + GN_EPS) * ln_w + ln_b
    y = _dot(((on + bonus_ref[...]) * gate_ref[...]).astype(BF), wo_ref[...])
    y_ref[...] = _layer_norm(ALPHA * x_ref[...] + y, g_ref[0:1, :], beta_ref[0:1, :])


def _rwkv_out(o, bonus, gate, x2, P, j, depth, tm=1024):
    n = o.shape[0]
    tm = min(tm, max(n // 2, 8))
    rows = pl.BlockSpec((tm, D_MODEL), lambda i: (i, 0))
    vecs, w_o, ln_g, ln_b = P['rwkv_vecs'], P['rwkv_w_o'], P['ln_g'], P['ln_b']
    return pl.pallas_call(
        _rwkv_out_kernel,
        grid=(n // tm,),
        in_specs=[rows] * 4 + [_layer(vecs.shape, j), _full(P['head_ones'].shape), _layer(w_o.shape, j),
                               _layer(ln_g.shape, depth), _layer(ln_b.shape, depth)],
        out_specs=rows,
        out_shape=jax.ShapeDtypeStruct((n, D_MODEL), F32),
        compiler_params=_cparams("parallel"),
        name="rwkv_out",
    )(o, bonus, gate, x2, vecs, P['head_ones'], w_o, ln_g, ln_b)


def _trunk(x, win_k, win_v, shift, wkv, P, bias):
    prompt = win_k is None
    B, T, D = x.shape
    n = B * T
    nkv = N_KV_HEADS * HEAD_DIM
    n_rwkv = DEPTH // 2
    v_first = None
    nk, nv, ns = [], [], []
    nk_all = nv_all = s_all = None
    for i in range(DEPTH):
        j = i // 2
        x2 = x.reshape(n, D)
        if i % 2 == 0:
            if prompt:
                y, k_win, v_win = _attn_prompt(x, P, j, i, bias)
                x2 = y.reshape(n, D)
                nk.append(k_win.reshape(B, WINDOW, N_KV_HEADS, HEAD_DIM))
                nv.append(v_win.reshape(B, WINDOW, N_KV_HEADS, HEAD_DIM))
            else:
                q, kvt = _qkv_proj_t(x2, P['attn_w_qkv'], P['attn_b_qkv'], j)
                o, nk_all, nv_all = _swa_sample(q.reshape(B, T, D), kvt, win_k, win_v, j, bias,
                                                P['attn_sinks'][j], nk_all, nv_all)
                x2 = _proj_ln(o.reshape(n, D), P['attn_w_o'], P['attn_b_o'], x2, P['ln_g'], P['ln_b'],
                              j, i)
        else:
            x_prev = jnp.zeros((B, 1, D), x.dtype) if prompt else shift[j].reshape(B, 1, D)
            bb, tt = (1, math.gcd(T, 512)) if prompt else (min(32, B), T)
            r, logd, k, v, kk, a, gate, bonus = _rwkv_proj(x, x_prev, v_first, P, j, bb, tt)
            if j == 0:
                v_first = v
            C, groups = (WKV_ROWS, math.gcd(B, 4)) if prompt else (T, 1)
            o, s_all = _wkv(r, logd, k, v, kk, a, wkv, j, n_rwkv, s_all, C, groups)
            ns.append(x[:, -1])
            x2 = _rwkv_out(o.reshape(n, D), bonus.reshape(n, D), gate.reshape(n, D), x2, P, j, i)
        x2 = _ffn(x2, P['ffn_w_gu'], P['ffn_w_down'], P['ln_g'], P['ln_b'], i)
        x = x2.reshape(B, T, D)
    if prompt:
        nk_all, nv_all = jnp.stack(nk), jnp.stack(nv)
    else:
        nk_all = jnp.transpose(nk_all, (0, 1, 4, 2, 3))
        nv_all = jnp.transpose(nv_all, (0, 1, 4, 2, 3))
    return x, nk_all, nv_all, jnp.stack(ns), s_all


def kernel(x_prompt, x_sample, cache_win_k, cache_win_v, state_shift, state_wkv, rel_bias, ln_g, ln_b, attn_w_qkv, attn_b_qkv, attn_w_o, attn_b_o, attn_sinks, rwkv_mix, rwkv_w_rkv, rwkv_w0, rwkv_w1, rwkv_w2, rwkv_a0, rwkv_a1, rwkv_a2, rwkv_v0, rwkv_v1, rwkv_v2, rwkv_g1, rwkv_g2, rwkv_k_k, rwkv_k_a, rwkv_r_k, rwkv_ln_w, rwkv_ln_b, rwkv_w_o, ffn_w_gu, ffn_w_down):
    bf = lambda w: w.astype(BF)
    lane_head = np.arange(MXU_WIDTH) // RWKV_HEAD
    head_ones = jnp.asarray(lane_head[:, None] == lane_head[None, :], dtype=BF)
    n_rwkv = rwkv_w0.shape[0]
    v0_rows = jnp.concatenate([jnp.zeros((1, D_MODEL), F32), rwkv_v0], axis=0)
    rwkv_vecs = jnp.stack([rwkv_w0, rwkv_a0, v0_rows, rwkv_k_k, rwkv_k_a,
                           rwkv_r_k.reshape(n_rwkv, D_MODEL), rwkv_ln_w, rwkv_ln_b], axis=1)
    P = dict(ln_g=ln_g, ln_b=ln_b,
             attn_w_qkv=bf(attn_w_qkv), attn_b_qkv=attn_b_qkv[:, None, :],
             attn_w_o=bf(attn_w_o), attn_b_o=attn_b_o[:, None, :], attn_sinks=attn_sinks,
             rwkv_mix=rwkv_mix, rwkv_vecs=rwkv_vecs, rwkv_w_rkv=bf(rwkv_w_rkv),
             rwkv_w1=bf(rwkv_w1), rwkv_w2=bf(rwkv_w2), rwkv_a1=bf(rwkv_a1), rwkv_a2=bf(rwkv_a2),
             rwkv_v1=bf(rwkv_v1), rwkv_v2=bf(rwkv_v2), rwkv_g1=bf(rwkv_g1), rwkv_g2=bf(rwkv_g2),
             rwkv_w_o=bf(rwkv_w_o), ffn_w_gu=bf(ffn_w_gu), ffn_w_down=bf(ffn_w_down),
             head_ones=head_ones)

    T = x_sample.shape[1]
    L = cache_win_k.shape[2]
    qi, kc = np.arange(WINDOW)[:, None], np.arange(WINDOW)[None, :]
    d_prompt = np.where(kc <= qi, qi - kc, qi + WINDOW - kc)
    d_sample = np.arange(T)[:, None] + L - np.arange(L + T)[None, :]
    bias_prompt = _bias_table(rel_bias, d_prompt)
    bias_sample = _bias_table(rel_bias, d_sample)

    y_prompt, pk, pv, ps, pw = _trunk(x_prompt, None, None, None, None, P, bias_prompt)
    win_k = jnp.transpose(cache_win_k, (0, 1, 3, 4, 2))
    win_v = jnp.transpose(cache_win_v, (0, 1, 3, 4, 2))
    y_sample, sk, sv, ss, sw = _trunk(x_sample, win_k, win_v, state_shift, state_wkv,
                                      P, bias_sample)
    return (y_prompt, y_sample, pk, pv, ps, pw, sk, sv, ss, sw)
```

```python
import functools
import math

import numpy as np
import jax
import jax.numpy as jnp
from jax import lax
from jax.experimental import pallas as pl
from jax.experimental.pallas import tpu as pltpu

D_MODEL = 1024
DEPTH = 4
HEAD_DIM = 64
N_HEADS = 16
N_KV_HEADS = 4
GROUP = 4
WINDOW = 128
N_BUCKETS = 32
MAX_DISTANCE = 128
RWKV_HEAD = 64
RWKV_HEADS = 16
GN_EPS = 64e-5
D_FF = 2816
ALPHA = (2 * DEPTH) ** 0.25
LN_EPS = 1e-5
NEG = -1e30
MXU_WIDTH = 256
WKV_ROWS = 64

BF = jnp.bfloat16
F32 = jnp.float32
VMEM_LIMIT = 56 * 1024 * 1024

NT_DIMS = (((1,), (1,)), ((), ()))
TN_DIMS = (((0,), (0,)), ((), ()))

V_W0, V_A0, V_V0, V_KK, V_KA, V_RK, V_LNW, V_LNB = range(8)


def _cparams(*sem):
    return pltpu.CompilerParams(dimension_semantics=sem, vmem_limit_bytes=VMEM_LIMIT)


def _dot(a, b):
    return jnp.dot(a, b, preferred_element_type=F32)


def _layer_norm(z, g, b):
    mu = jnp.mean(z, axis=-1, keepdims=True)
    zc = z - mu
    var = jnp.mean(zc * zc, axis=-1, keepdims=True)
    return zc * lax.rsqrt(var + LN_EPS) * g + b


def _sigmoid(z):
    return 1.0 / (1.0 + jnp.exp(-z))


def _split2(z):
    hi = z.astype(BF)
    lo = (z - hi.astype(F32)).astype(BF)
    return hi, lo


def _split3(z):
    hi = z.astype(BF)
    r1 = z - hi.astype(F32)
    mid = r1.astype(BF)
    lo = (r1 - mid.astype(F32)).astype(BF)
    return hi, mid, lo


def _full(shape):
    n = len(shape)
    return pl.BlockSpec(shape, lambda *_: (0,) * n)


def _layer(shape, *lead):
    k = len(lead)
    rest = len(shape) - k
    return pl.BlockSpec((None,) * k + tuple(shape[k:]), lambda *_: tuple(lead) + (0,) * rest,
                        pipeline_mode=pl.Buffered(1))


def _bias_table_kernel(rb_ref, bucket_ref, valid_ref, o_ref):
    bucket = bucket_ref[...]
    valid = valid_ref[...] > 0
    for h in range(N_HEADS):
        acc = jnp.zeros(bucket.shape, F32)
        for b in range(N_BUCKETS):
            acc = jnp.where(bucket == b, rb_ref[b, h], acc)
        o_ref[h] = jnp.where(valid, acc, NEG)


def _bias_table(rel_bias, d):
    dc = np.maximum(d, 0)
    exact = N_BUCKETS // 2
    df = np.maximum(dc, 1).astype(np.float32)
    large = exact + (np.log(df / np.float32(exact)) / np.float32(math.log(MAX_DISTANCE / exact))
                     * np.float32(N_BUCKETS - exact)).astype(np.int32)
    bucket = np.where(dc < exact, dc, np.minimum(large, N_BUCKETS - 1)).astype(np.int32)
    valid = ((d >= 0) & (d < WINDOW)).astype(np.int32)
    return pl.pallas_call(
        _bias_table_kernel,
        out_shape=jax.ShapeDtypeStruct((N_HEADS,) + d.shape, F32),
        in_specs=[pl.BlockSpec(memory_space=pltpu.SMEM),
                  pl.BlockSpec(memory_space=pltpu.VMEM),
                  pl.BlockSpec(memory_space=pltpu.VMEM)],
        out_specs=pl.BlockSpec(memory_space=pltpu.VMEM),
        name="bias_table",
    )(rel_bias, jnp.asarray(bucket), jnp.asarray(valid))


def _attn_prompt_kernel(sink_ref, x_ref, wqkv_ref, bqkv_ref, bias_ref, wo_ref, bo_ref, g_ref, beta_ref,
                        y_ref, kwin_ref, vwin_ref, kprev_ref, vprev_ref, obuf_ref):
    step = pl.program_id(1)
    W = WINDOW
    R = x_ref.shape[1]
    nsub = R // W
    nq = N_HEADS * HEAD_DIM
    nk = N_KV_HEADS * HEAD_DIM

    @pl.when(step == 0)
    def _():
        kprev_ref[...] = jnp.zeros_like(kprev_ref)
        vprev_ref[...] = jnp.zeros_like(vprev_ref)

    row = lax.broadcasted_iota(jnp.int32, (W, W), 0)
    col = lax.broadcasted_iota(jnp.int32, (W, W), 1)
    own = col <= row
    zero = jnp.zeros((W, W), BF)
    ones = jnp.ones((2 * W, HEAD_DIM), BF)
    lo = lax.broadcasted_iota(jnp.int32, (W, 2 * HEAD_DIM), 1) < HEAD_DIM
    subs = range(nsub)
    ctx = [dict() for _ in range(x_ref.shape[0])]

    def project(t):
        c = ctx[t]
        c['x'] = x_ref[t]
        qkv = _dot(c['x'].astype(BF), wqkv_ref[...]) + bqkv_ref[...]
        c['q'] = (qkv[:, :nq] * HEAD_DIM ** -0.5).astype(BF)
        c['k'] = qkv[:, nq:nq + nk]
        c['v'] = qkv[:, nq + nk:]
        kwin_ref[t] = c['k'][R - W:]
        vwin_ref[t] = c['v'][R - W:]
        c['kall'] = jnp.concatenate([kprev_ref[t], c['k']], axis=0).astype(BF)
        c['vall'] = jnp.concatenate([vprev_ref[t], c['v']], axis=0).astype(BF)

    def score(t):
        c = ctx[t]
        c['s'] = {}
        for sub in subs:
            for h in range(N_HEADS):
                g = h // GROUP
                qh = c['q'][sub * W:(sub + 1) * W, h * HEAD_DIM:(h + 1) * HEAD_DIM]
                kh = c['kall'][sub * W:(sub + 2) * W, g * HEAD_DIM:(g + 1) * HEAD_DIM]
                s2 = lax.dot_general(qh, kh, NT_DIMS, preferred_element_type=F32)
                s_prev = s2[:, :W]
                if sub == 0:
                    s_prev = jnp.where(step == 0, NEG, s_prev)
                c['s'][sub, h] = jnp.where(own, s2[:, W:], s_prev) + bias_ref[h]

    def softmax(t):
        c = ctx[t]
        c['m'], c['pb'] = {}, {}
        for sub in subs:
            for g in range(N_KV_HEADS):
                s = jnp.concatenate([c['s'][sub, g * GROUP + j] for j in range(GROUP)], axis=0)
                mg = jnp.max(s, axis=-1, keepdims=True)
                c['m'][sub, g] = mg
                c['pb'][sub, g] = jnp.exp(s - mg).astype(BF)

    def values(t):
        c = ctx[t]
        for sub in subs:
            for g in range(N_KV_HEADS):
                vh = c['vall'][sub * W:(sub + 2) * W, g * HEAD_DIM:(g + 1) * HEAD_DIM]
                vext = jnp.concatenate([vh, ones, ones, vh], axis=1)
                res, esink = [], []
                for j in range(GROUP):
                    hs = slice(j * W, (j + 1) * W)
                    ph = c['pb'][sub, g][hs]
                    p2 = jnp.concatenate([jnp.where(own, zero, ph), jnp.where(own, ph, zero)], axis=1)
                    res.append(_dot(p2, vext))
                    esink.append(jnp.exp(sink_ref[g * GROUP + j] - c['m'][sub, g][hs]))
                for j in range(0, GROUP, 2):
                    o_pair = jnp.where(lo, res[j][:, :2 * HEAD_DIM], res[j + 1][:, 2 * HEAD_DIM:])
                    t_pair = jnp.where(lo, res[j][:, 2 * HEAD_DIM:], res[j + 1][:, :2 * HEAD_DIM])
                    den = t_pair + jnp.where(lo, esink[j], esink[j + 1])
                    c0 = (g * GROUP + j) * HEAD_DIM
                    obuf_ref[t, sub * W:(sub + 1) * W, c0:c0 + 2 * HEAD_DIM] = (o_pair / den).astype(BF)
        kprev_ref[t] = c['k'][R - W:]
        vprev_ref[t] = c['v'][R - W:]

    def output(t):
        y = _dot(obuf_ref[t], wo_ref[...]) + bo_ref[...]
        y_ref[t] = _layer_norm(ALPHA * ctx[t]['x'] + y, g_ref[0:1, :], beta_ref[0:1, :])

    order = [(project, 0), (score, 0), (softmax, 0)]
    for t in range(1, len(ctx)):
        order += [(project, t), (values, t - 1), (score, t), (softmax, t), (output, t - 1)]
    order += [(values, len(ctx) - 1), (output, len(ctx) - 1)]
    for fn, t in order:
        fn(t)


def _attn_prompt(x, P, j, depth, bias, nsub=4):
    B, T, D = x.shape
    nb = T // WINDOW
    nsub = math.gcd(nb, nsub)
    R = nsub * WINDOW
    nk = N_KV_HEADS * HEAD_DIM
    w, b, w_o, b_o = P['attn_w_qkv'], P['attn_b_qkv'], P['attn_w_o'], P['attn_b_o']
    ln_g, ln_b = P['ln_g'], P['ln_b']
    nseq = math.gcd(B, 2)
    rows = pl.BlockSpec((nseq, R, D), lambda bi, i: (bi, i, 0))
    win = pl.BlockSpec((nseq, WINDOW, nk), lambda bi, i: (bi, 0, 0))
    return pl.pallas_call(
        _attn_prompt_kernel,
        grid=(B // nseq, nb // nsub),
        in_specs=[pl.BlockSpec(memory_space=pltpu.SMEM), rows,
                  _layer(w.shape, j), _layer(b.shape, j), _full(bias.shape),
                  _layer(w_o.shape, j), _layer(b_o.shape, j),
                  _layer(ln_g.shape, depth), _layer(ln_b.shape, depth)],
        out_specs=[rows, win, win],
        out_shape=[jax.ShapeDtypeStruct((B, T, D), F32),
                   jax.ShapeDtypeStruct((B, WINDOW, nk), F32),
                   jax.ShapeDtypeStruct((B, WINDOW, nk), F32)],
        scratch_shapes=[pltpu.VMEM((nseq, WINDOW, nk), F32), pltpu.VMEM((nseq, WINDOW, nk), F32),
                        pltpu.VMEM((nseq, R, D), BF)],
        compiler_params=_cparams("parallel", "arbitrary"),
        name="attn_prompt",
    )(P['attn_sinks'][j], x, w, b, bias, w_o, b_o, ln_g, ln_b)


def _qkv_t_kernel(x_ref, wq_ref, wkv_ref, bq_ref, bkv_ref, q_ref, kvt_ref):
    xb = x_ref[...].astype(BF)
    q_ref[...] = (_dot(xb, wq_ref[...]) + bq_ref[...]) * HEAD_DIM ** -0.5
    kvt_ref[...] = (_dot(xb, wkv_ref[...]) + bkv_ref[...]).T


def _qkv_proj_t(x2, w, b, layer, tm=256):
    n = x2.shape[0]
    tm = min(tm, n)
    nq = N_HEADS * HEAD_DIM
    nkv = 2 * N_KV_HEADS * HEAD_DIM
    assert nq % nkv == 0
    once = pl.Buffered(1)
    return pl.pallas_call(
        _qkv_t_kernel,
        grid=(n // tm,),
        in_specs=[pl.BlockSpec((tm, D_MODEL), lambda i: (i, 0)),
                  pl.BlockSpec((None, D_MODEL, nq), lambda i: (layer, 0, 0), pipeline_mode=once),
                  pl.BlockSpec((None, D_MODEL, nkv), lambda i: (layer, 0, nq // nkv), pipeline_mode=once),
                  pl.BlockSpec((None, 1, nq), lambda i: (layer, 0, 0), pipeline_mode=once),
                  pl.BlockSpec((None, 1, nkv), lambda i: (layer, 0, nq // nkv), pipeline_mode=once)],
        out_specs=[pl.BlockSpec((tm, nq), lambda i: (i, 0)),
                   pl.BlockSpec((nkv, tm), lambda i: (0, i))],
        out_shape=[jax.ShapeDtypeStruct((n, nq), F32),
                   jax.ShapeDtypeStruct((nkv, n), F32)],
        compiler_params=_cparams("parallel"),
        name="qkv_proj_t",
    )(x2, w, w, b, b)


def _swa_sample_kernel(T, first, sink_ref, q_ref, kvt_ref, ck_ref, cv_ref, bc_ref, bn_ref, *rest):
    o_ref, nk_ref, nv_ref = rest[-3:]
    bb = q_ref.shape[0]
    L = ck_ref.shape[-1]
    HD = HEAD_DIM
    if first:
        for lyr in range(1, nk_ref.shape[0]):
            nk_ref[lyr] = jnp.zeros(nk_ref.shape[1:], F32)
            nv_ref[lyr] = jnp.zeros(nv_ref.shape[1:], F32)
        nk_out, nv_out = nk_ref.at[0], nv_ref.at[0]
    else:
        nk_out, nv_out = nk_ref, nv_ref
    units = [(b, g) for b in range(bb) for g in range(N_KV_HEADS)]
    lane = lax.broadcasted_iota(jnp.int32, (HD, L), 1)
    knew = [kvt_ref[g * HD:(g + 1) * HD, :] for g in range(N_KV_HEADS)]
    vnew = [kvt_ref[(N_KV_HEADS + g) * HD:(N_KV_HEADS + g + 1) * HD, :] for g in range(N_KV_HEADS)]

    def shift_in(old, new, b):
        moved = pltpu.roll(new, (L - T - b * T) % L, 1)
        return jnp.where(lane >= L - T, moved, pltpu.roll(old, L - T, 1))

    sc, sn = {}, {}
    for b in range(bb):
        q = q_ref[b]
        for g in range(N_KV_HEADS):
            qg = jnp.concatenate([q[:, (g * GROUP + j) * HD:(g * GROUP + j + 1) * HD]
                                  for j in range(GROUP)], axis=0).astype(BF)
            kc = ck_ref[b, g]
            nk_out[b, g] = shift_in(kc, knew[g], b)
            sc[b, g] = _dot(qg, kc.astype(BF)) + bc_ref[g]
            sn[b, g] = _dot(qg, knew[g][:, b * T:(b + 1) * T].astype(BF)) + bn_ref[g]
    sc_all = jnp.concatenate([sc[u] for u in units], axis=0)
    sn_all = jnp.concatenate([sn[u] for u in units], axis=0)
    m = jnp.maximum(jnp.max(sc_all, axis=-1, keepdims=True), jnp.max(sn_all, axis=-1, keepdims=True))
    pc_all = jnp.exp(sc_all - m)
    pn_all = jnp.exp(sn_all - m)
    tot = jnp.sum(pc_all, axis=-1, keepdims=True) + jnp.sum(pn_all, axis=-1, keepdims=True)
    pc_all = pc_all.astype(BF)
    pn_all = pn_all.astype(BF)
    gt = GROUP * T
    for b in range(bb):
        pieces = []
        for g in range(N_KV_HEADS):
            r0 = (b * N_KV_HEADS + g) * gt
            vc = cv_ref[b, g]
            nv_out[b, g] = shift_in(vc, vnew[g], b)
            og = (lax.dot_general(pc_all[r0:r0 + gt], vc.astype(BF), NT_DIMS, preferred_element_type=F32)
                  + lax.dot_general(pn_all[r0:r0 + gt], vnew[g][:, b * T:(b + 1) * T].astype(BF),
                                    NT_DIMS, preferred_element_type=F32))
            for j in range(GROUP):
                rj = slice(r0 + j * T, r0 + (j + 1) * T)
                inv = 1.0 / (tot[rj] + jnp.exp(sink_ref[g * GROUP + j] - m[rj]))
                pieces.append(og[j * T:(j + 1) * T] * inv)
        o_ref[b] = jnp.concatenate(pieces, axis=-1).astype(o_ref.dtype)


def _swa_sample(q, kvt, ck, cv, layer, bias, sinks, nk_prev, nv_prev):
    B, T, _ = q.shape
    nl, L = ck.shape[0], ck.shape[-1]
    first = nk_prev is None
    bb = L // T
    assert B % bb == 0 and L % T == 0
    bias = bias.reshape(N_KV_HEADS, GROUP * T, L + T)
    bc, bn = bias[:, :, :L], bias[:, :, L:]
    qblk = pl.BlockSpec((bb, T, D_MODEL), lambda i: (i, 0, 0))
    cblk = pl.BlockSpec((None, bb, N_KV_HEADS, HEAD_DIM, L), lambda i: (layer, i, 0, 0, 0))
    nblk = (pl.BlockSpec((nl, bb, N_KV_HEADS, HEAD_DIM, L), lambda i: (0, i, 0, 0, 0)) if first else cblk)
    ins = [sinks, q, kvt, ck, cv, bc, bn]
    in_specs = [pl.BlockSpec(memory_space=pltpu.SMEM), qblk,
                pl.BlockSpec((kvt.shape[0], bb * T), lambda i: (0, i)),
                cblk, cblk, _full(bc.shape), _full(bn.shape)]
    aliases = {}
    if not first:
        aliases = {len(ins): 1, len(ins) + 1: 2}
        ins += [nk_prev, nv_prev]
        in_specs += [pl.BlockSpec(memory_space=pl.ANY)] * 2
    return pl.pallas_call(
        functools.partial(_swa_sample_kernel, T, first),
        grid=(B // bb,),
        in_specs=in_specs,
        out_specs=[qblk, nblk, nblk],
        out_shape=[jax.ShapeDtypeStruct((B, T, D_MODEL), BF),
                   jax.ShapeDtypeStruct((nl, B, N_KV_HEADS, HEAD_DIM, L), F32),
                   jax.ShapeDtypeStruct((nl, B, N_KV_HEADS, HEAD_DIM, L), F32)],
        input_output_aliases=aliases,
        compiler_params=_cparams("parallel"),
        name="swa_sample",
    )(*ins)


def _proj_ln_kernel(which, a_ref, w_ref, b_ref, x_ref, g_ref, beta_ref, o_ref):
    y = _dot(a_ref[...].astype(BF), w_ref[...]) + b_ref[...]
    o_ref[...] = _layer_norm(ALPHA * x_ref[...] + y, g_ref[which:which + 1, :], beta_ref[which:which + 1, :])


def _proj_ln(a, w, b, x2, ln_g, ln_b, layer, depth, tm=1024):
    n, kdim = a.shape
    tm = min(tm, max(n // 2, 8))
    rows = lambda i: (i, 0)
    return pl.pallas_call(
        functools.partial(_proj_ln_kernel, 0),
        grid=(n // tm,),
        in_specs=[pl.BlockSpec((tm, kdim), rows), _layer(w.shape, layer), _layer(b.shape, layer),
                  pl.BlockSpec((tm, D_MODEL), rows), _layer(ln_g.shape, depth), _layer(ln_b.shape, depth)],
        out_specs=pl.BlockSpec((tm, D_MODEL), rows),
        out_shape=jax.ShapeDtypeStruct((n, D_MODEL), F32),
        compiler_params=_cparams("parallel"),
        name="proj_ln",
    )(a, w, b, x2, ln_g, ln_b)


def _ffn_kernel(tf, x_ref, wgu_ref, wd_ref, g_ref, beta_ref, o_ref):
    x = x_ref[...]
    xb = x.astype(BF)
    acc = None
    for c in range(0, D_FF, tf):
        e = min(c + tf, D_FF)
        gt = _dot(xb, wgu_ref[:, c:e])
        ut = _dot(xb, wgu_ref[:, D_FF + c:D_FF + e])
        h = (gt * _sigmoid(gt) * ut).astype(BF)
        part = _dot(h, wd_ref[c:e, :])
        acc = part if acc is None else acc + part
    o_ref[...] = _layer_norm(ALPHA * x + acc, g_ref[1:2, :], beta_ref[1:2, :])


def _ffn(x2, w_gu, w_down, ln_g, ln_b, depth, tm=512, tf=6 * MXU_WIDTH):
    n = x2.shape[0]
    tm = min(tm, n)
    assert D_FF % MXU_WIDTH == 0 and tf % MXU_WIDTH == 0
    rows = lambda i: (i, 0)
    return pl.pallas_call(
        functools.partial(_ffn_kernel, tf),
        grid=(n // tm,),
        in_specs=[pl.BlockSpec((tm, D_MODEL), rows),
                  _layer(w_gu.shape, depth), _layer(w_down.shape, depth),
                  _layer(ln_g.shape, depth), _layer(ln_b.shape, depth)],
        out_specs=pl.BlockSpec((tm, D_MODEL), rows),
        out_shape=jax.ShapeDtypeStruct((n, D_MODEL), F32),
        compiler_params=_cparams("parallel"),
        name="ffn",
    )(x2, w_gu, w_down, ln_g, ln_b)


def _head_sum(z, m_ref):
    hi, lo = _split2(z)
    m = m_ref[...]
    w = m.shape[0]
    return jnp.concatenate([_dot(hi[:, c:c + w], m) + _dot(lo[:, c:c + w], m)
                            for c in range(0, z.shape[1], w)], axis=1)


def _rwkv_proj_kernel(has_vres, *refs):
    if has_vres:
        (x_ref, xp_ref, mix_ref, vec_ref, wr_ref, wk_ref, wv_ref, w1_ref, w2_ref, a1_ref, a2_ref,
         g1_ref, g2_ref, m_ref, v1_ref, v2_ref, vf_ref,
         r_ref, w_ref, k_ref, v_ref, kk_ref, a_ref, g_ref, bonus_ref, carry_ref) = refs
    else:
        (x_ref, xp_ref, mix_ref, vec_ref, wr_ref, wk_ref, wv_ref, w1_ref, w2_ref, a1_ref, a2_ref,
         g1_ref, g2_ref, m_ref,
         r_ref, w_ref, k_ref, v_ref, kk_ref, a_ref, g_ref, bonus_ref, carry_ref) = refs
    bb, tt, D = x_ref.shape
    n = bb * tt
    t = pl.program_id(1)
    x3 = x_ref[...]
    prev = jnp.where(t == 0, xp_ref[...], carry_ref[...])
    carry_ref[...] = x3[:, tt - 1:tt, :]
    x = x3.reshape(n, D)
    prev_rows = jnp.broadcast_to(prev, (bb, tt, D)).reshape(n, D)
    row = lax.broadcasted_iota(jnp.int32, (n, D), 0)
    xs = jnp.where(row % tt == 0, prev_rows, pltpu.roll(x, 1, 0))
    xx = xs - x
    xr, xw, xk, xv, xa, xg = ((x + xx * mix_ref[c:c + 1, :]).astype(BF) for c in range(6))
    w0, a0, v0, k_k, k_a = (vec_ref[c:c + 1, :] for c in (V_W0, V_A0, V_V0, V_KK, V_KA))

    a = _sigmoid(a0 + _dot(_dot(xa, a1_ref[...]).astype(BF), a2_ref[...]))
    k = _dot(xk, wk_ref[...])
    r = _dot(xr, wr_ref[...])
    kk = k * k_k
    kk = kk * jnp.minimum(lax.rsqrt(_head_sum(kk * kk, m_ref)), 1e12)
    k = k * (1.0 + (a - 1.0) * k_a)
    v = _dot(xv, wv_ref[...])
    if has_vres:
        gate = _sigmoid(v0 + _dot(_dot(xv, v1_ref[...]).astype(BF), v2_ref[...]))
        v = v + (vf_ref[...].reshape(n, D) - v) * gate
    rk_sum = _head_sum(r * k * vec_ref[V_RK:V_RK + 1, :], m_ref)
    z = w0 + _dot(jnp.tanh(_dot(xw, w1_ref[...])).astype(BF), w2_ref[...])
    logd = -math.exp(-0.5) * _sigmoid(z)
    g = _dot(_sigmoid(_dot(xg, g1_ref[...])).astype(BF), g2_ref[...])
    bonus = rk_sum * v
    for ref, val in ((r_ref, r), (w_ref, logd), (k_ref, k), (v_ref, v), (kk_ref, kk), (a_ref, a),
                     (g_ref, g), (bonus_ref, bonus)):
        ref[...] = val.reshape(bb, tt, D)


def _rwkv_proj(x, x_prev, v_first, P, j, bb, tt):
    B, T, D = x.shape
    has_vres = j > 0
    blk = pl.BlockSpec((bb, tt, D), lambda b, t: (b, t, 0))
    w_rkv = P['rwkv_w_rkv']
    ins = [x, x_prev, P['rwkv_mix'], P['rwkv_vecs'], w_rkv, w_rkv, w_rkv,
           P['rwkv_w1'], P['rwkv_w2'], P['rwkv_a1'], P['rwkv_a2'], P['rwkv_g1'], P['rwkv_g2'],
           P['head_ones']]
    in_specs = [blk, pl.BlockSpec((bb, 1, D), lambda b, t: (b, 0, 0)),
                _layer(ins[2].shape, j), _layer(ins[3].shape, j),
                _layer(w_rkv.shape, j, 0), _layer(w_rkv.shape, j, 1), _layer(w_rkv.shape, j, 2)]
    in_specs += [_layer(a.shape, j) for a in ins[7:13]] + [_full(P['head_ones'].shape)]
    if has_vres:
        ins += [P['rwkv_v1'], P['rwkv_v2'], v_first]
        in_specs += [_layer(P['rwkv_v1'].shape, j - 1), _layer(P['rwkv_v2'].shape, j - 1), blk]
    return pl.pallas_call(
        functools.partial(_rwkv_proj_kernel, has_vres),
        grid=(B // bb, T // tt),
        in_specs=in_specs,
        out_specs=[blk] * 8,
        out_shape=[jax.ShapeDtypeStruct((B, T, D), F32)] * 8,
        scratch_shapes=[pltpu.VMEM((bb, 1, D), F32)],
        compiler_params=_cparams("parallel", "arbitrary"),
        name="rwkv_proj",
    )(*ins)


def _wkv_kernel(has_s0, first, C, *refs):
    r_ref, w_ref, k_ref, v_ref, kk_ref, a_ref = refs[:6]
    s0_ref = refs[6] if has_s0 else None
    o_ref, so_ref, s_scr = refs[-3:]
    groups = r_ref.shape[0]
    R = WKV_ROWS
    N = RWKV_HEAD
    NP = RWKV_HEADS // 2
    nseq = R // C
    t = pl.program_id(1)
    so_cur = so_ref.at[0] if first else so_ref

    @pl.when(t == 0)
    def _():
        if has_s0:
            for s in range(groups * nseq):
                for p in range(NP):
                    s_scr[s, p] = jnp.concatenate([s0_ref[s, 2 * p], s0_ref[s, 2 * p + 1]], axis=1)
        else:
            s_scr[...] = jnp.zeros_like(s_scr)

    D = r_ref.shape[-1]
    GR = groups * R
    flat = lambda ref: ref[...].reshape(GR, D)
    logd = flat(w_ref)
    ri = lax.broadcasted_iota(jnp.int32, (GR, GR), 0)
    ci = lax.broadcasted_iota(jnp.int32, (GR, GR), 1)
    tril = jnp.logical_and(ri // C == ci // C, ci <= ri).astype(BF)
    cum = sum(_dot(tril, part) for part in _split3(logd))
    g_in = jnp.exp(cum)
    g_ex = jnp.exp(cum - logd)
    g_inv = jnp.exp(-cum)
    kk = flat(kk_ref)
    at = -(kk * g_ex)
    rt = flat(r_ref) * g_in
    bt = kk * flat(a_ref) * g_inv
    kt = flat(k_ref) * g_inv
    v = flat(v_ref)

    def lane_masks(width):
        rr = lax.broadcasted_iota(jnp.int32, (R, width), 0)
        cc = lax.broadcasted_iota(jnp.int32, (R, width), 1) % N
        same = rr // C == cc // C
        return jnp.logical_and(same, cc < rr), jnp.logical_and(same, cc <= rr)

    strict2, _ = lane_masks(2 * N)
    _, incl4 = lane_masks(4 * N)

    def bdiag(m):
        lo = lax.broadcasted_iota(jnp.int32, m.shape, 1) < N
        zero = jnp.zeros_like(m)
        return jnp.concatenate([jnp.where(lo, m, zero), jnp.where(lo, zero, m)], axis=0)

    units = [(gi, p) for gi in range(groups) for p in range(NP)]
    seqs = range(nseq)
    tile = lambda z, u: z[u[0] * R:(u[0] + 1) * R, 2 * N * u[1]:2 * N * (u[1] + 1)]
    rows = [slice(C * s, C * (s + 1)) for s in seqs]
    vp = {u: tile(v, u) for u in units}
    vpb = {u: vp[u].astype(BF) for u in units}
    btp = {u: tile(bt, u) for u in units}
    ktp = {u: tile(kt, u) for u in units}
    xa = {u: tile(at, u) for u in units}
    xr = {u: tile(rt, u) for u in units}
    S0 = {(u, s): s_scr[u[0] * nseq + s, u[1]] for u in units for s in seqs}

    xp = {u: jnp.concatenate([xa[u], xr[u]], axis=0).astype(BF) for u in units}
    bmat = {u: jnp.concatenate([bdiag(btp[u].astype(BF)), bdiag(ktp[u].astype(BF))], axis=0)
            for u in units}
    G = {u: lax.dot_general(xp[u], bmat[u], NT_DIMS, preferred_element_type=F32) for u in units}
    Z = {(u, s): lax.dot_general(
        xp[u] if nseq == 1 else
        jnp.concatenate([xa[u][rows[s]], xr[u][rows[s]]], axis=0).astype(BF),
        bdiag(S0[u, s].astype(BF)), NT_DIMS, preferred_element_type=F32) for u in units for s in seqs}
    za = {u: Z[u, 0][:C] if nseq == 1 else jnp.concatenate([Z[u, s][:C] for s in seqs], axis=0)
          for u in units}
    zr = {u: Z[u, 0][C:] if nseq == 1 else jnp.concatenate([Z[u, s][C:] for s in seqs], axis=0)
          for u in units}
    A = {u: jnp.where(strict2, G[u][:R, :2 * N], 0.0) for u in units}
    aak = {u: jnp.where(strict2, G[u][:R, 2 * N:], 0.0).astype(BF) for u in units}
    T2 = {u: jnp.where(incl4, G[u][R:], 0.0).astype(BF) for u in units}
    W = {u: za[u] + _dot(aak[u], bdiag(vpb[u])) for u in units}
    nsteps = max(1, int(math.log2(C)))
    for step in range(nsteps):
        Ab = {u: A[u].astype(BF) for u in units}
        if step + 1 < nsteps:
            res = {u: _dot(Ab[u], jnp.concatenate([bdiag(W[u].astype(BF)), bdiag(Ab[u])], axis=1))
                   for u in units}
            W = {u: W[u] + res[u][:, :2 * N] for u in units}
            A = {u: res[u][:, 2 * N:] for u in units}
        else:
            W = {u: W[u] + _dot(Ab[u], bdiag(W[u].astype(BF))) for u in units}
    wb = {u: W[u].astype(BF) for u in units}
    for u in units:
        uv = jnp.concatenate([bdiag(wb[u]), bdiag(vpb[u])], axis=0)
        o_ref[u[0], :, 2 * N * u[1]:2 * N * (u[1] + 1)] = zr[u] + _dot(T2[u], uv)
    lo = lax.broadcasted_iota(jnp.int32, (N, 2 * N), 1) < N
    for u in units:
        for s in seqs:
            uvs = jnp.concatenate([W[u][rows[s]], vp[u][rows[s]]], axis=0).astype(BF)
            ys = jnp.concatenate([btp[u][rows[s]], ktp[u][rows[s]]], axis=0).astype(BF)
            full = lax.dot_general(uvs, ys, TN_DIMS, preferred_element_type=F32)
            upd = jnp.where(lo, full[:N], full[N:])
            last = u[0] * R + C * (s + 1) - 1
            g_end = g_in[last:last + 1, 2 * N * u[1]:2 * N * (u[1] + 1)]
            s_scr[u[0] * nseq + s, u[1]] = (S0[u, s] + upd) * g_end

    @pl.when(t == pl.num_programs(1) - 1)
    def _():
        for s in range(groups * nseq):
            for p in range(NP):
                so_cur[s, 2 * p] = s_scr[s, p][:, :N]
                so_cur[s, 2 * p + 1] = s_scr[s, p][:, N:]
        if first:
            for lyr in range(1, so_ref.shape[0]):
                so_ref[lyr] = jnp.zeros(so_ref.shape[1:], F32)


def _wkv(r, logd, k, v, kk, a, s0, layer, nlayers, s_prev, C, groups):
    B, T, D = r.shape
    has_s0 = s0 is not None
    first = s_prev is None
    R = WKV_ROWS
    nseq = R // C
    nt = T // C
    assert (nseq == 1 or nt == 1) and T % C == 0
    nbk = B * T // (R * nt)
    assert nbk % groups == 0
    gs = groups * nseq
    blk = pl.BlockSpec((groups, None, R, D), lambda b, t: (b, t, 0, 0))
    sdims = (RWKV_HEADS, RWKV_HEAD, RWKV_HEAD)
    sblk = pl.BlockSpec((None, gs) + sdims, lambda b, t: (layer, b, 0, 0, 0))
    oblk = pl.BlockSpec((nlayers, gs) + sdims, lambda b, t: (0, b, 0, 0, 0)) if first else sblk
    ins = [x.reshape(nbk, nt, R, D) for x in (r, logd, k, v, kk, a)]
    in_specs = [blk] * 6
    if has_s0:
        ins.append(s0[layer])
        in_specs.append(pl.BlockSpec((gs,) + sdims, lambda b, t: (b, 0, 0, 0)))
    aliases = {}
    if not first:
        aliases = {len(ins): 1}
        ins.append(s_prev)
        in_specs.append(pl.BlockSpec(memory_space=pl.ANY))
    o, s_new = pl.pallas_call(
        functools.partial(_wkv_kernel, has_s0, first, C),
        grid=(nbk // groups, nt),
        in_specs=in_specs,
        out_specs=[blk, oblk],
        out_shape=[jax.ShapeDtypeStruct((nbk, nt, R, D), F32),
                   jax.ShapeDtypeStruct((nlayers, B) + sdims, F32)],
        scratch_shapes=[pltpu.VMEM((gs, RWKV_HEADS // 2, RWKV_HEAD, 2 * RWKV_HEAD), F32)],
        input_output_aliases=aliases,
        compiler_params=_cparams("parallel", "arbitrary"),
        name="wkv",
    )(*ins)
    return o.reshape(B, T, D), s_new


def _rwkv_out_kernel(o_ref, bonus_ref, gate_ref, x_ref, vec_ref, m_ref, wo_ref, g_ref, beta_ref, y_ref):
    ln_w, ln_b = (vec_ref[c:c + 1, :] for c in (V_LNW, V_LNB))
    o = o_ref[...]
    inv_n = 1.0 / RWKV_HEAD
    mu = _head_sum(o, m_ref) * inv_n
    oc = o - mu
    var = _head_sum(oc * oc, m_ref) * inv_n
    on = oc * lax.rsqrt(var + GN_EPS) * ln_w + ln_b
    y = _dot(((on + bonus_ref[...]) * gate_ref[...]).astype(BF), wo_ref[...])
    y_ref[...] = _layer_norm(ALPHA * x_ref[...] + y, g_ref[0:1, :], beta_ref[0:1, :])


def _rwkv_out(o, bonus, gate, x2, P, j, depth, tm=1024):
    n = o.shape[0]
    tm = min(tm, max(n // 2, 8))
    rows = pl.BlockSpec((tm, D_MODEL), lambda i: (i, 0))
    vecs, w_o, ln_g, ln_b = P['rwkv_vecs'], P['rwkv_w_o'], P['ln_g'], P['ln_b']
    return pl.pallas_call(
        _rwkv_out_kernel,
        grid=(n // tm,),
        in_specs=[rows] * 4 + [_layer(vecs.shape, j), _full(P['head_ones'].shape), _layer(w_o.shape, j),
                               _layer(ln_g.shape, depth), _layer(ln_b.shape, depth)],
        out_specs=rows,
        out_shape=jax.ShapeDtypeStruct((n, D_MODEL), F32),
        compiler_params=_cparams("parallel"),
        name="rwkv_out",
    )(o, bonus, gate, x2, vecs, P['head_ones'], w_o, ln_g, ln_b)


def _trunk(x, win_k, win_v, shift, wkv, P, bias):
    prompt = win_k is None
    B, T, D = x.shape
    n = B * T
    nkv = N_KV_HEADS * HEAD_DIM
    n_rwkv = DEPTH // 2
    v_first = None
    nk, nv, ns = [], [], []
    nk_all = nv_all = s_all = None
    for i in range(DEPTH):
        j = i // 2
        x2 = x.reshape(n, D)
        if i % 2 == 0:
            if prompt:
                y, k_win, v_win = _attn_prompt(x, P, j, i, bias)
                x2 = y.reshape(n, D)
                nk.append(k_win.reshape(B, WINDOW, N_KV_HEADS, HEAD_DIM))
                nv.append(v_win.reshape(B, WINDOW, N_KV_HEADS, HEAD_DIM))
            else:
                q, kvt = _qkv_proj_t(x2, P['attn_w_qkv'], P['attn_b_qkv'], j)
                o, nk_all, nv_all = _swa_sample(q.reshape(B, T, D), kvt, win_k, win_v, j, bias,
                                                P['attn_sinks'][j], nk_all, nv_all)
                x2 = _proj_ln(o.reshape(n, D), P['attn_w_o'], P['attn_b_o'], x2, P['ln_g'], P['ln_b'],
                              j, i)
        else:
            x_prev = jnp.zeros((B, 1, D), x.dtype) if prompt else shift[j].reshape(B, 1, D)
            bb, tt = (1, math.gcd(T, 512)) if prompt else (min(32, B), T)
            r, logd, k, v, kk, a, gate, bonus = _rwkv_proj(x, x_prev, v_first, P, j, bb, tt)
            if j == 0:
                v_first = v
            C, groups = (WKV_ROWS, math.gcd(B, 4)) if prompt else (T, 1)
            o, s_all = _wkv(r, logd, k, v, kk, a, wkv, j, n_rwkv, s_all, C, groups)
            ns.append(x[:, -1])
            x2 = _rwkv_out(o.reshape(n, D), bonus.reshape(n, D), gate.reshape(n, D), x2, P, j, i)
        x2 = _ffn(x2, P['ffn_w_gu'], P['ffn_w_down'], P['ln_g'], P['ln_b'], i)
        x = x2.reshape(B, T, D)
    if prompt:
        nk_all, nv_all = jnp.stack(nk), jnp.stack(nv)
    else:
        nk_all = jnp.transpose(nk_all, (0, 1, 4, 2, 3))
        nv_all = jnp.transpose(nv_all, (0, 1, 4, 2, 3))
    return x, nk_all, nv_all, jnp.stack(ns), s_all


def kernel(x_prompt, x_sample, cache_win_k, cache_win_v, state_shift, state_wkv, rel_bias, ln_g, ln_b, attn_w_qkv, attn_b_qkv, attn_w_o, attn_b_o, attn_sinks, rwkv_mix, rwkv_w_rkv, rwkv_w0, rwkv_w1, rwkv_w2, rwkv_a0, rwkv_a1, rwkv_a2, rwkv_v0, rwkv_v1, rwkv_v2, rwkv_g1, rwkv_g2, rwkv_k_k, rwkv_k_a, rwkv_r_k, rwkv_ln_w, rwkv_ln_b, rwkv_w_o, ffn_w_gu, ffn_w_down):
    bf = lambda w: w.astype(BF)
    lane_head = np.arange(MXU_WIDTH) // RWKV_HEAD
    head_ones = jnp.asarray(lane_head[:, None] == lane_head[None, :], dtype=BF)
    n_rwkv = rwkv_w0.shape[0]
    v0_rows = jnp.concatenate([jnp.zeros((1, D_MODEL), F32), rwkv_v0], axis=0)
    rwkv_vecs = jnp.stack([rwkv_w0, rwkv_a0, v0_rows, rwkv_k_k, rwkv_k_a,
                           rwkv_r_k.reshape(n_rwkv, D_MODEL), rwkv_ln_w, rwkv_ln_b], axis=1)
    P = dict(ln_g=ln_g, ln_b=ln_b,
             attn_w_qkv=bf(attn_w_qkv), attn_b_qkv=attn_b_qkv[:, None, :],
             attn_w_o=bf(attn_w_o), attn_b_o=attn_b_o[:, None, :], attn_sinks=attn_sinks,
             rwkv_mix=rwkv_mix, rwkv_vecs=rwkv_vecs, rwkv_w_rkv=bf(rwkv_w_rkv),
             rwkv_w1=bf(rwkv_w1), rwkv_w2=bf(rwkv_w2), rwkv_a1=bf(rwkv_a1), rwkv_a2=bf(rwkv_a2),
             rwkv_v1=bf(rwkv_v1), rwkv_v2=bf(rwkv_v2), rwkv_g1=bf(rwkv_g1), rwkv_g2=bf(rwkv_g2),
             rwkv_w_o=bf(rwkv_w_o), ffn_w_gu=bf(ffn_w_gu), ffn_w_down=bf(ffn_w_down),
             head_ones=head_ones)

    T = x_sample.shape[1]
    L = cache_win_k.shape[2]
    qi, kc = np.arange(WINDOW)[:, None], np.arange(WINDOW)[None, :]
    d_prompt = np.where(kc <= qi, qi - kc, qi + WINDOW - kc)
    d_sample = np.arange(T)[:, None] + L - np.arange(L + T)[None, :]
    bias_prompt = _bias_table(rel_bias, d_prompt)
    bias_sample = _bias_table(rel_bias, d_sample)

    y_prompt, pk, pv, ps, pw = _trunk(x_prompt, None, None, None, None, P, bias_prompt)
    win_k = jnp.transpose(cache_win_k, (0, 1, 3, 4, 2))
    win_v = jnp.transpose(cache_win_v, (0, 1, 3, 4, 2))
    y_sample, sk, sv, ss, sw = _trunk(x_sample, win_k, win_v, state_shift, state_wkv,
                                      P, bias_sample)
    return (y_prompt, y_sample, pk, pv, ps, pw, sk, sv, ss, sw)
```
